```python
import math
import jax, jax.numpy as jnp
from jax import lax
import numpy as np


D_MODEL = 1024
BATCH = 4
SEQ = 8192
DEPTH = 2

GLA_HEADS = 4
GLA_DK = 64
GLA_DV = 128
GLA_RANK = 16
GLA_TAU = 16.0
GLA_CHUNK = 64
DIFF_HEADS = 4
DIFF_DH = 64
DIFF_DV = 2 * DIFF_DH
Q_BLOCK = 128
REL_BUCKETS = 32
REL_MAX_DIST = 128
N_GROUPS = 4
EXPERTS_PER_GROUP = 8
N_EXPERTS = N_GROUPS * EXPERTS_PER_GROUP
TOP_K = 2
D_EXPERT = D_MODEL // 2
RMS_EPS = 1e-6

GLA_QK_W = GLA_HEADS * GLA_DK
GLA_V_W = GLA_HEADS * GLA_DV
DIFF_QK_W = DIFF_HEADS * 2 * DIFF_DH
DIFF_V_W = DIFF_HEADS * DIFF_DV
IN_SPLITS = (GLA_QK_W, GLA_QK_W, GLA_V_W, GLA_V_W, GLA_RANK, GLA_RANK,
             DIFF_QK_W, DIFF_QK_W, DIFF_V_W, D_MODEL, D_MODEL)
D_IN = GLA_QK_W * 2 + GLA_V_W * 2 + GLA_RANK * 2 + DIFF_QK_W * 2 + DIFF_V_W + D_MODEL * 2

kernel_name = 'hybrid_gla_diffattn_hmoe_encoder'


def rms_norm(x, g):
    xf = x.astype(jnp.float32)
    y = xf * lax.rsqrt(jnp.mean(xf * xf, axis=-1, keepdims=True) + RMS_EPS)
    return (y * g.astype(jnp.float32)).astype(x.dtype)


def modulate(h, shift, scale):
    return h * (1.0 + scale[:, None, :]) + shift[:, None, :]


def split_heads(t, n):
    b, s, _ = t.shape
    return t.reshape(b, s, n, -1).transpose(0, 2, 1, 3)


def merge_heads(t):
    b, h, s, d = t.shape
    return t.transpose(0, 2, 1, 3).reshape(b, s, h * d)


def t5_bucket(rel):
    nb = REL_BUCKETS // 2
    max_exact = nb // 2
    ret = jnp.where(rel > 0, nb, 0)
    n = jnp.abs(rel)
    nf = jnp.maximum(n, 1).astype(jnp.float32)
    large = max_exact + (jnp.log(nf / max_exact) / math.log(REL_MAX_DIST / max_exact)
                         * (nb - max_exact)).astype(jnp.int32)
    large = jnp.minimum(large, nb - 1)
    return ret + jnp.where(n < max_exact, n, large)


def gla_chunked(q, k, v, lg, inclusive):
    out_dtype = v.dtype
    b_, h_, s_, dk = q.shape
    dv = v.shape[-1]
    n = s_ // GLA_CHUNK
    f32 = jnp.float32
    q = q.astype(f32).reshape(b_, h_, n, GLA_CHUNK, dk)
    k = k.astype(f32).reshape(b_, h_, n, GLA_CHUNK, dk)
    v = v.astype(f32).reshape(b_, h_, n, GLA_CHUNK, dv)
    cum = jnp.cumsum(lg.astype(f32).reshape(b_, h_, n, GLA_CHUNK, dk), axis=-2)
    cum_last = cum[..., -1:, :]
    q_dec = q * jnp.exp(cum)
    k_inv = k * jnp.exp(-cum)
    k_end = k * jnp.exp(cum_last - cum)
    mask = jnp.tril(jnp.ones((GLA_CHUNK, GLA_CHUNK), bool), 0 if inclusive else -1)
    a = jnp.where(mask, jnp.einsum('bhnid,bhnjd->bhnij', q_dec, k_inv), 0.0)
    o_intra = jnp.einsum('bhnij,bhnjv->bhniv', a, v)
    kv = jnp.einsum('bhncd,bhncv->bhndv', k_end, v)
    chunk_decay = jnp.exp(cum_last[..., 0, :])

    def step(state, inp):
        kv_n, dec_n = inp
        return dec_n[..., None] * state + kv_n, state

    _, states = lax.scan(step, jnp.zeros((b_, h_, dk, dv), f32),
                         (jnp.moveaxis(kv, 2, 0), jnp.moveaxis(chunk_decay, 2, 0)))
    states = jnp.moveaxis(states, 0, 2)
    o_inter = jnp.einsum('bhncd,bhndv->bhncv', q_dec, states)
    return (o_intra + o_inter).reshape(b_, h_, s_, dv).astype(out_dtype)


def diff_attention(q, k, v, lam, rel_bias):
    b_, h_, _, s_, dh = q.shape
    nb = s_ // Q_BLOCK
    qb = q.reshape(b_, h_, 2, nb, Q_BLOCK, dh).transpose(3, 0, 1, 2, 4, 5)
    kpos = jnp.arange(s_, dtype=jnp.int32)
    scale = dh ** -0.5

    def block(args):
        q_blk, start = args
        qpos = start + jnp.arange(Q_BLOCK, dtype=jnp.int32)
        bucket = t5_bucket(kpos[None, :] - qpos[:, None])
        bias = jnp.moveaxis(rel_bias[bucket], -1, 0).astype(jnp.float32)
        logits = (jnp.einsum('bhcqd,bhckd->bhcqk', q_blk, k).astype(jnp.float32) * scale
                  + bias[None, :, None])
        p = jax.nn.softmax(logits, axis=-1)
        w = p[:, :, 0] - lam.astype(jnp.float32) * p[:, :, 1]
        return jnp.einsum('bhqk,bhkv->bhqv', w.astype(v.dtype), v)

    starts = jnp.arange(nb, dtype=jnp.int32) * Q_BLOCK
    out = lax.map(block, (qb, starts))
    return out.transpose(1, 2, 0, 3, 4).reshape(b_, h_, s_, v.shape[-1])


def token_mixer(h, w_in, gla_w_up, gla_b_up, gla_norm_g, qn_g, kn_g, lam_vecs,
                subn_g, rel_bias, w_a, w_b, w_o, lam_init):
    b_, s_, _ = h.shape
    proj = h @ w_in
    idx = np.cumsum(IN_SPLITS)[:-1].tolist()
    (g_q, g_k, g_v, g_g, lr_f, lr_b, d_q, d_k, d_v, gate_a, gate_b) = jnp.split(proj, idx, axis=-1)

    q = split_heads(g_q, GLA_HEADS) * (GLA_DK ** -0.5)
    k = split_heads(g_k, GLA_HEADS)
    v = split_heads(g_v, GLA_HEADS)
    lg_f = split_heads(jax.nn.log_sigmoid((lr_f @ gla_w_up[0] + gla_b_up[0]).astype(jnp.float32)) / GLA_TAU, GLA_HEADS)
    lg_b = split_heads(jax.nn.log_sigmoid((lr_b @ gla_w_up[1] + gla_b_up[1]).astype(jnp.float32)) / GLA_TAU, GLA_HEADS)
    flip = lambda t: jnp.flip(t, axis=2)
    o_f = gla_chunked(q, k, v, lg_f, True)
    o_b = flip(gla_chunked(flip(q), flip(k), flip(v), flip(lg_b), False))
    o_a = merge_heads(rms_norm(o_f + o_b, gla_norm_g)) * jax.nn.silu(g_g)
    y_a = o_a @ w_a

    dq = rms_norm(d_q.reshape(b_, s_, DIFF_HEADS, 2, DIFF_DH).transpose(0, 2, 3, 1, 4), qn_g)
    dk = rms_norm(d_k.reshape(b_, s_, DIFF_HEADS, 2, DIFF_DH).transpose(0, 2, 3, 1, 4), kn_g)
    dv = split_heads(d_v, DIFF_HEADS)
    lam = (jnp.exp(jnp.sum(lam_vecs[0] * lam_vecs[1]).astype(jnp.float32))
           - jnp.exp(jnp.sum(lam_vecs[2] * lam_vecs[3]).astype(jnp.float32)) + lam_init)
    o_b2 = diff_attention(dq, dk, dv, lam, rel_bias)
    o_b2 = merge_heads(rms_norm(o_b2, subn_g) * (1.0 - lam_init))
    y_b = o_b2 @ w_b

    merged = jax.nn.sigmoid(gate_a) * y_a + jax.nn.sigmoid(gate_b) * y_b
    return merged @ w_o


def hier_moe(h, w_rg, b_rg, w_re, b_re, w1, w2):
    b_, s_, d_ = h.shape
    t = h.reshape(-1, d_)
    n_tok = t.shape[0]
    g_logits = (t @ w_rg).astype(jnp.float32) + b_rg.astype(jnp.float32)
    g_prob = jax.nn.softmax(g_logits, axis=-1)
    grp = jnp.argmax(g_logits, axis=-1)
    p_grp = jnp.take_along_axis(g_prob, grp[:, None], axis=-1)
    e_logits = ((t @ w_re).astype(jnp.float32) + b_re.astype(jnp.float32)).reshape(n_tok, N_GROUPS, EXPERTS_PER_GROUP)
    e_logits = jnp.take_along_axis(e_logits, grp[:, None, None], axis=1)[:, 0]
    e_prob = jax.nn.softmax(e_logits, axis=-1)
    top_p, top_e = lax.top_k(e_prob, TOP_K)
    gate = p_grp * (top_p / jnp.sum(top_p, axis=-1, keepdims=True))
    expert = grp[:, None] * EXPERTS_PER_GROUP + top_e
    flat_e = expert.reshape(-1)
    flat_tok = jnp.repeat(jnp.arange(n_tok, dtype=jnp.int32), TOP_K)
    flat_gate = gate.reshape(-1)
    order = jnp.argsort(flat_e)
    tok_s = flat_tok[order]
    xs = t[tok_s]
    sizes = jnp.bincount(flat_e, length=N_EXPERTS).astype(jnp.int32)
    hu = lax.ragged_dot(xs, w1, sizes)
    h_gate, h_up = jnp.split(hu, 2, axis=-1)
    ys = lax.ragged_dot(jax.nn.silu(h_gate) * h_up, w2, sizes)
    ys = ys * flat_gate[order][:, None].astype(ys.dtype)
    out = jnp.zeros_like(t).at[tok_s].add(ys)
    return out.reshape(b_, s_, d_)


def setup_inputs(seed: int = 0) -> dict:
    key = jax.random.key(seed)
    ks = jax.random.split(key, 26)
    nrm = lambda k, shape, s: jax.random.normal(k, shape, jnp.float32) * s
    gain = lambda k, shape: 1.0 + 0.05 * jax.random.normal(k, shape, jnp.float32)
    D = D_MODEL
    return {
        'x': nrm(ks[0], (BATCH, SEQ, D), 1.0),
        'c': nrm(ks[1], (BATCH, D), 1.0),
        'w_ada': nrm(ks[2], (DEPTH, D, 6 * D), 0.5 * D ** -0.5),
        'b_ada': nrm(ks[3], (DEPTH, 6 * D), 0.01),
        'norm1_g': gain(ks[4], (DEPTH, D)),
        'norm2_g': gain(ks[5], (DEPTH, D)),
        'w_in': nrm(ks[6], (DEPTH, D, D_IN), D ** -0.5),
        'gla_w_up': nrm(ks[7], (DEPTH, 2, GLA_RANK, GLA_QK_W), GLA_RANK ** -0.5),
        'gla_b_up': nrm(ks[8], (DEPTH, 2, GLA_QK_W), 0.1),
        'gla_norm_g': gain(ks[9], (DEPTH, GLA_DV)),
        'diff_qnorm_g': gain(ks[10], (DEPTH, DIFF_DH)),
        'diff_knorm_g': gain(ks[11], (DEPTH, DIFF_DH)),
        'diff_lambda': nrm(ks[12], (DEPTH, 4, DIFF_DH), 0.1),
        'diff_subnorm_g': gain(ks[13], (DEPTH, DIFF_DV)),
        'rel_bias': nrm(ks[14], (REL_BUCKETS, DIFF_HEADS), 0.5),
        'w_branch_a': nrm(ks[15], (DEPTH, GLA_V_W, D), GLA_V_W ** -0.5),
        'w_branch_b': nrm(ks[16], (DEPTH, DIFF_V_W, D), DIFF_V_W ** -0.5),
        'w_out': nrm(ks[17], (DEPTH, D, D), D ** -0.5),
        'w_router_group': nrm(ks[18], (DEPTH, D, N_GROUPS), D ** -0.5),
        'b_router_group': nrm(ks[19], (DEPTH, N_GROUPS), 0.01),
        'w_router_expert': nrm(ks[20], (DEPTH, D, N_EXPERTS), D ** -0.5),
        'b_router_expert': nrm(ks[21], (DEPTH, N_EXPERTS), 0.01),
        'w_expert_in': nrm(ks[22], (DEPTH, N_EXPERTS, D, 2 * D_EXPERT), D ** -0.5),
        'w_expert_out': nrm(ks[23], (DEPTH, N_EXPERTS, D_EXPERT, D), D_EXPERT ** -0.5),
    }


def reference(x, c, w_ada, b_ada, norm1_g, norm2_g, w_in, gla_w_up, gla_b_up, gla_norm_g,
              diff_qnorm_g, diff_knorm_g, diff_lambda, diff_subnorm_g, rel_bias,
              w_branch_a, w_branch_b, w_out, w_router_group, b_router_group,
              w_router_expert, b_router_expert, w_expert_in, w_expert_out):
    c_act = jax.nn.silu(c)
    for l in range(DEPTH):
        lam_init = 0.8 - 0.6 * math.exp(-0.3 * l)
        mod = c_act @ w_ada[l] + b_ada[l]
        sh1, sc1, g1, sh2, sc2, g2 = jnp.split(mod, 6, axis=-1)
        h = modulate(rms_norm(x, norm1_g[l]), sh1, sc1)
        y = token_mixer(h, w_in[l], gla_w_up[l], gla_b_up[l], gla_norm_g[l],
                        diff_qnorm_g[l], diff_knorm_g[l], diff_lambda[l], diff_subnorm_g[l],
                        rel_bias, w_branch_a[l], w_branch_b[l], w_out[l], lam_init)
        x = x + g1[:, None, :] * y
        h = modulate(rms_norm(x, norm2_g[l]), sh2, sc2)
        y = hier_moe(h, w_router_group[l], b_router_group[l], w_router_expert[l],
                     b_router_expert[l], w_expert_in[l], w_expert_out[l])
        x = x + g2[:, None, :] * y
    return x
```

```python
import functools
import math

import numpy as np
import jax
import jax.numpy as jnp
from jax import lax
from jax.experimental import pallas as pl
from jax.experimental.pallas import tpu as pltpu

F32 = jnp.float32
BF16 = jnp.bfloat16
HIGHEST = lax.Precision.HIGHEST

D_MODEL = 1024
DEPTH = 2
GLA_HEADS, GLA_DK, GLA_DV, GLA_RANK, GLA_TAU, GLA_CHUNK = 4, 64, 128, 16, 16.0, 64
DIFF_HEADS, DIFF_DH, DIFF_DV = 4, 64, 128
REL_BUCKETS, REL_MAX_DIST = 32, 128
N_GROUPS, EXPERTS_PER_GROUP = 4, 8
N_EXPERTS = N_GROUPS * EXPERTS_PER_GROUP
D_EXPERT = D_MODEL // 2
RMS_EPS = 1e-6
LANES = 128
SUBLANES = 8
VMEM_LIMIT = 52 * 1024 * 1024
EXPERT_TILE = 256
ZERO_ROWS = EXPERT_TILE + SUBLANES

COL_GQK, COL_GV, COL_DQ, COL_DK, COL_GG, COL_DV, COL_GA, COL_GB = 0, 4, 8, 12, 16, 20, 24, 32
N_PROJ = 5120


def _cparams(sem, vmem=VMEM_LIMIT):
    return pltpu.CompilerParams(dimension_semantics=sem, vmem_limit_bytes=vmem)


def _ada_kernel(c_ref, w_ref, b_ref, o_ref):
    c = c_ref[...]
    ca = c * jax.nn.sigmoid(c)
    o_ref[0] = jnp.dot(ca, w_ref[0], precision=HIGHEST, preferred_element_type=F32) + b_ref[0]


def _ada(c, w_ada, b_ada):
    depth, d, n = w_ada.shape
    bsz = c.shape[0]
    tn = 1536
    return pl.pallas_call(
        _ada_kernel,
        grid=(depth, n // tn),
        in_specs=[pl.BlockSpec((bsz, d), lambda l, j: (0, 0)),
                  pl.BlockSpec((1, d, tn), lambda l, j: (l, 0, j)),
                  pl.BlockSpec((1, 1, tn), lambda l, j: (l, 0, j))],
        out_specs=pl.BlockSpec((1, bsz, tn), lambda l, j: (l, 0, j)),
        out_shape=jax.ShapeDtypeStruct((depth, bsz, n), F32),
        compiler_params=_cparams(("parallel", "parallel")),
        name="ada",
    )(c, w_ada, b_ada.reshape(depth, 1, n))


def _inproj_kernel(x_ref, mod_ref, g_ref, w_ref, wlr_ref, gmat_ref, qkg_ref, proj_ref, lr_ref):
    x = x_ref[...]
    ms = jnp.mean(x * x, axis=-1, keepdims=True)
    h = x * lax.rsqrt(ms + RMS_EPS) * g_ref[...]
    h = h * (1.0 + mod_ref[0, 1:2, :]) + mod_ref[0, 0:1, :]
    hb = h.astype(BF16)
    for j in range(N_PROJ // 1024):
        acc = jnp.dot(hb, w_ref[:, j * 1024:(j + 1) * 1024], preferred_element_type=F32)
        if j == COL_DQ * LANES // 1024:
            ms2 = jnp.dot((acc * acc).astype(BF16), gmat_ref[...], preferred_element_type=F32)
            acc = acc * lax.rsqrt(ms2 + RMS_EPS) * qkg_ref[...]
        proj_ref[:, j * 1024:(j + 1) * 1024] = acc.astype(BF16)
    lr_ref[...] = jnp.dot(hb, wlr_ref[...], preferred_element_type=F32)


def _inproj(x, mod, g, w, wlr, gmat, qkg, seq):
    t, d = x.shape
    tm = min(256, seq)
    tpb = seq // tm
    return pl.pallas_call(
        _inproj_kernel,
        grid=(t // tm,),
        in_specs=[pl.BlockSpec((tm, d), lambda i: (i, 0)),
                  pl.BlockSpec((1, 6, d), lambda i: (i // tpb, 0, 0)),
                  pl.BlockSpec((1, d), lambda i: (0, 0)),
                  pl.BlockSpec((d, N_PROJ), lambda i: (0, 0)),
                  pl.BlockSpec((d, LANES), lambda i: (0, 0)),
                  pl.BlockSpec((1024, 1024), lambda i: (0, 0)),
                  pl.BlockSpec((1, 1024), lambda i: (0, 0))],
        out_specs=[pl.BlockSpec((tm, N_PROJ), lambda i: (i, 0)),
                   pl.BlockSpec((tm, LANES), lambda i: (i, 0))],
        out_shape=[jax.ShapeDtypeStruct((t, N_PROJ), BF16),
                   jax.ShapeDtypeStruct((t, LANES), F32)],
        compiler_params=_cparams(("parallel",)),
        name="inproj",
    )(x, mod, g, w, wlr, gmat, qkg)


def _gla_chunk(qk, v, lr, wup, bup, smat, mask, tot_row, s_ref):
    q = qk[:, :GLA_HEADS * GLA_DK].astype(F32)
    k = qk[:, GLA_HEADS * GLA_DK:].astype(F32)
    z = jnp.dot(lr, wup, precision=HIGHEST, preferred_element_type=F32) + bup
    lg = (jnp.minimum(z, 0.0) - jnp.log(1.0 + jnp.exp(-jnp.abs(z)))) * (1.0 / GLA_TAU)
    cum = jnp.dot(smat, lg, precision=HIGHEST, preferred_element_type=F32)
    dec = jnp.exp(cum[tot_row:tot_row + 1, :])
    q_dec = (q * jnp.exp(cum) * (GLA_DK ** -0.5)).astype(BF16)
    k_inv = k * jnp.exp(-cum)
    k_end = (k_inv * dec).astype(BF16)
    k_inv = k_inv.astype(BF16)
    outs = []
    for h in range(GLA_HEADS):
        sl = slice(h * GLA_DK, (h + 1) * GLA_DK)
        qh, kh, keh = q_dec[:, sl], k_inv[:, sl], k_end[:, sl]
        vh = v[:, h * GLA_DV:(h + 1) * GLA_DV]
        a = lax.dot_general(qh, kh, (((1,), (1,)), ((), ())), preferred_element_type=F32)
        a = jnp.where(mask, a, 0.0).astype(BF16)
        st = s_ref[h]
        o = jnp.dot(a, vh, preferred_element_type=F32)
        o = o + lax.dot_general(qh, st.astype(BF16), (((1,), (1,)), ((), ())),
                                preferred_element_type=F32)
        kv_t = lax.dot_general(vh, keh, (((0,), (0,)), ((), ())), preferred_element_type=F32)
        s_ref[h] = st * dec[:, sl] + kv_t
        outs.append(o)
    return jnp.concatenate(outs, axis=-1)


def _gla_kernel(qkf_ref, vf_ref, lrf_ref, qkb_ref, vb_ref, lrb_ref, wup_ref, bup_ref,
                of_ref, ob_ref, sf_ref, sb_ref, *, n_chunks):
    @pl.when(pl.program_id(1) == 0)
    def _():
        sf_ref[...] = jnp.zeros_like(sf_ref)
        sb_ref[...] = jnp.zeros_like(sb_ref)

    c = GLA_CHUNK
    row = lax.broadcasted_iota(jnp.int32, (c, c), 0)
    col = lax.broadcasted_iota(jnp.int32, (c, c), 1)
    pre = (col <= row).astype(F32)
    suf = (col >= row).astype(F32)
    mask_f = col <= row
    mask_b = col > row

    def body(ci, carry):
        rf = pl.multiple_of(ci * c, c)
        rb = pl.multiple_of((n_chunks - 1 - ci) * c, c)
        of = _gla_chunk(qkf_ref[pl.ds(rf, c), :], vf_ref[pl.ds(rf, c), :],
                        lrf_ref[pl.ds(rf, c), :][:, 0:GLA_RANK], wup_ref[0], bup_ref[0],
                        pre, mask_f, c - 1, sf_ref)
        of_ref[pl.ds(rf, c), :] = of.astype(of_ref.dtype)
        ob = _gla_chunk(qkb_ref[pl.ds(rb, c), :], vb_ref[pl.ds(rb, c), :],
                        lrb_ref[pl.ds(rb, c), :][:, GLA_RANK:2 * GLA_RANK], wup_ref[1], bup_ref[1],
                        suf, mask_b, 0, sb_ref)
        ob_ref[pl.ds(rb, c), :] = ob.astype(ob_ref.dtype)
        return carry

    lax.fori_loop(0, n_chunks, body, 0)


def _gla(proj, lr, wup, bup, bsz, seq):
    t = proj.shape[0]
    lb = min(512, seq)
    nblk = seq // lb
    vw = GLA_HEADS * GLA_DV
    fwd = lambda b, i: b * nblk + i
    bwd = lambda b, i: b * nblk + nblk - 1 - i
    return pl.pallas_call(
        functools.partial(_gla_kernel, n_chunks=lb // GLA_CHUNK),
        grid=(bsz, nblk),
        in_specs=[pl.BlockSpec((lb, 512), lambda b, i: (fwd(b, i), COL_GQK // 4)),
                  pl.BlockSpec((lb, 512), lambda b, i: (fwd(b, i), COL_GV // 4)),
                  pl.BlockSpec((lb, LANES), lambda b, i: (fwd(b, i), 0)),
                  pl.BlockSpec((lb, 512), lambda b, i: (bwd(b, i), COL_GQK // 4)),
                  pl.BlockSpec((lb, 512), lambda b, i: (bwd(b, i), COL_GV // 4)),
                  pl.BlockSpec((lb, LANES), lambda b, i: (bwd(b, i), 0)),
                  pl.BlockSpec((2, GLA_RANK, GLA_HEADS * GLA_DK), lambda b, i: (0, 0, 0)),
                  pl.BlockSpec((2, 1, GLA_HEADS * GLA_DK), lambda b, i: (0, 0, 0))],
        out_specs=[pl.BlockSpec((lb, vw), lambda b, i: (fwd(b, i), 0)),
                   pl.BlockSpec((lb, vw), lambda b, i: (bwd(b, i), 0))],
        out_shape=[jax.ShapeDtypeStruct((t, vw), BF16), jax.ShapeDtypeStruct((t, vw), BF16)],
        scratch_shapes=[pltpu.VMEM((GLA_HEADS, GLA_DV, GLA_DK), F32),
                        pltpu.VMEM((GLA_HEADS, GLA_DV, GLA_DK), F32)],
        compiler_params=_cparams(("parallel", "arbitrary")),
        name="gla",
    )(proj, proj, lr, proj, proj, lr, wup, bup)


def _t5_bucket_np(rel):
    nb = REL_BUCKETS // 2
    max_exact = nb // 2
    ret = np.where(rel > 0, nb, 0)
    n = np.abs(rel)
    nf = np.maximum(n, 1).astype(np.float64)
    large = max_exact + (np.log(nf / max_exact) / math.log(REL_MAX_DIST / max_exact)
                         * (nb - max_exact)).astype(np.int64)
    large = np.minimum(large, nb - 1)
    return ret + np.where(n < max_exact, n, large)


def _attn_kernel(q_ref, k_ref, v_ref, tile_ref, far_ref, lam_ref, sg_ref, o_ref,
                 m_ref, l_ref, acc_ref, *, tq, tk, nk, lam_init):
    qi = pl.program_id(2)
    ratio = tq // tk if tq >= tk else 1
    q = q_ref[...]
    qs = (q[:, :DIFF_DH], q[:, DIFF_DH:])
    m_ref[...] = jnp.full_like(m_ref, -1e30)
    l_ref[...] = jnp.zeros_like(l_ref)
    acc_ref[...] = jnp.zeros_like(acc_ref)

    def update(kb, bias_tile, bias_const):
        r0 = pl.multiple_of(kb * tk, tk)
        k = k_ref[pl.ds(r0, tk), :]
        v = v_ref[pl.ds(r0, tk), :]
        for c in range(2):
            s = lax.dot_general(qs[c], k[:, c * DIFF_DH:(c + 1) * DIFF_DH],
                                (((1,), (1,)), ((), ())), preferred_element_type=F32)
            if bias_tile is not None:
                s = s + bias_tile
                cst = 0.0
            else:
                cst = bias_const
            m_old = m_ref[c]
            m_new = jnp.maximum(m_old, jnp.max(s, axis=-1, keepdims=True) + cst)
            p = jnp.exp(s - (m_new - cst))
            alpha = jnp.exp(m_old - m_new)
            l_ref[c] = alpha * l_ref[c] + jnp.sum(p, axis=-1, keepdims=True)
            acc_ref[c] = alpha * acc_ref[c] + jnp.dot(p.astype(BF16), v, preferred_element_type=F32)
            m_ref[c] = m_new

    lo_end = jnp.maximum(qi - 1, 0)
    hi_start = jnp.minimum(qi + 2, nk)

    head = pl.program_id(1)

    def lo_body(kb, carry):
        update(kb, None, far_ref[head, 0])
        return carry

    def hi_body(kb, carry):
        update(kb, None, far_ref[head, 1])
        return carry

    lax.fori_loop(0, lo_end, lo_body, 0)
    for dlt in (-1, 0, 1):
        @pl.when(jnp.logical_and(qi + dlt >= 0, qi + dlt < nk))
        def _(dlt=dlt):
            update(qi + dlt, tile_ref[0, dlt + 1], None)
    lax.fori_loop(hi_start, nk, hi_body, 0)

    lv = lam_ref[...]
    lam = (jnp.exp(jnp.sum(lv[0:1] * lv[1:2], axis=-1, keepdims=True))
           - jnp.exp(jnp.sum(lv[2:3] * lv[3:4], axis=-1, keepdims=True)) + lam_init)
    out = acc_ref[0] / l_ref[0] - lam * (acc_ref[1] / l_ref[1])
    ms = jnp.mean(out * out, axis=-1, keepdims=True)
    out = out * lax.rsqrt(ms + RMS_EPS) * sg_ref[...] * (1.0 - lam_init)
    o_ref[...] = out.astype(o_ref.dtype)


def _attn(proj, tiles, far, lam_vecs, subn_g, bsz, seq, lam_init):
    t = proj.shape[0]
    tq = tk = min(256, seq)
    nq = seq // tq
    nk = seq // tk
    return pl.pallas_call(
        functools.partial(_attn_kernel, tq=tq, tk=tk, nk=nk, lam_init=lam_init),
        grid=(bsz, DIFF_HEADS, nq),
        in_specs=[pl.BlockSpec((tq, LANES), lambda b, h, i: (b * nq + i, COL_DQ + h)),
                  pl.BlockSpec((seq, LANES), lambda b, h, i: (b, COL_DK + h)),
                  pl.BlockSpec((seq, LANES), lambda b, h, i: (b, COL_DV + h)),
                  pl.BlockSpec((1, 3, tq, tk), lambda b, h, i: (h, 0, 0, 0)),
                  pl.BlockSpec(memory_space=pltpu.SMEM),
                  pl.BlockSpec((4, DIFF_DH), lambda b, h, i: (0, 0)),
                  pl.BlockSpec((1, DIFF_DV), lambda b, h, i: (0, 0))],
        out_specs=pl.BlockSpec((tq, DIFF_DV), lambda b, h, i: (b * nq + i, h)),
        out_shape=jax.ShapeDtypeStruct((t, DIFF_HEADS * DIFF_DV), BF16),
        scratch_shapes=[pltpu.VMEM((2, tq, 1), F32), pltpu.VMEM((2, tq, 1), F32),
                        pltpu.VMEM((2, tq, DIFF_DV), F32)],
        compiler_params=_cparams(("parallel", "parallel", "arbitrary")),
        name="attn",
    )(proj, proj, proj, tiles, far, lam_vecs, subn_g)


def _bias_tables(rel_bias, tq, tk):
    i = np.arange(tq)[:, None]
    j = np.arange(tk)[None, :]
    buckets = np.stack([_t5_bucket_np((d * tk + j) - i) for d in (-1, 0, 1)])
    tiles = jnp.transpose(rel_bias[buckets], (3, 0, 1, 2)).astype(F32)
    nb = REL_BUCKETS // 2
    far = jnp.stack([rel_bias[nb - 1], rel_bias[2 * nb - 1]], axis=-1).astype(F32)
    return tiles, far


def _mix_kernel(x_ref, of_ref, ob_ref, gg_ref, ob2_ref, ga_ref, gb_ref, mod_ref, gn_ref, n2_ref,
                wa_ref, wb_ref, wo_ref, wr_ref, br_ref,
                x1_ref, h2_ref, info_ref, cnt_ref, base_ref, *, tm):
    @pl.when(pl.program_id(0) == 0)
    def _():
        base_ref[...] = jnp.zeros_like(base_ref)

    osum = of_ref[...].astype(F32) + ob_ref[...].astype(F32)
    parts = []
    for h in range(GLA_HEADS):
        sl = osum[:, h * GLA_DV:(h + 1) * GLA_DV]
        ms = jnp.mean(sl * sl, axis=-1, keepdims=True)
        parts.append(sl * lax.rsqrt(ms + RMS_EPS) * gn_ref[...])
    gg = gg_ref[...].astype(F32)
    o_a = jnp.concatenate(parts, axis=-1) * (gg * jax.nn.sigmoid(gg))
    y_a = jnp.dot(o_a.astype(BF16), wa_ref[...], preferred_element_type=F32)
    y_b = jnp.dot(ob2_ref[...], wb_ref[...], preferred_element_type=F32)
    merged = (jax.nn.sigmoid(ga_ref[...].astype(F32)) * y_a
              + jax.nn.sigmoid(gb_ref[...].astype(F32)) * y_b)
    y = jnp.dot(merged.astype(BF16), wo_ref[...], preferred_element_type=F32)
    x1 = x_ref[...] + mod_ref[0, 2:3, :] * y
    x1_ref[...] = x1

    ms = jnp.mean(x1 * x1, axis=-1, keepdims=True)
    h2 = x1 * lax.rsqrt(ms + RMS_EPS) * n2_ref[...]
    h2 = h2 * (1.0 + mod_ref[0, 4:5, :]) + mod_ref[0, 3:4, :]
    h2_ref[...] = h2

    logits = jnp.dot(h2, wr_ref[...], precision=HIGHEST, preferred_element_type=F32) + br_ref[...]
    lane_i = lax.broadcasted_iota(jnp.int32, logits.shape, 1)
    lane = lane_i.astype(F32)
    neg = jnp.float32(-3e38)
    big = jnp.float32(1 << 20)
    is_g = lane < N_GROUPS
    gl = jnp.where(is_g, logits, neg)
    gmax = jnp.max(gl, axis=-1, keepdims=True)
    grp = jnp.min(jnp.where(jnp.logical_and(is_g, gl == gmax), lane, big), axis=-1, keepdims=True)
    p_grp = 1.0 / jnp.sum(jnp.where(is_g, jnp.exp(gl - gmax), 0.0), axis=-1, keepdims=True)
    lo = N_GROUPS + grp * EXPERTS_PER_GROUP
    in_grp = jnp.logical_and(lane >= lo, lane < lo + EXPERTS_PER_GROUP)
    el = jnp.where(in_grp, logits, neg)
    e1 = jnp.max(el, axis=-1, keepdims=True)
    i1 = jnp.min(jnp.where(jnp.logical_and(in_grp, el == e1), lane, big), axis=-1, keepdims=True)
    rest = jnp.logical_and(in_grp, lane != i1)
    el2 = jnp.where(rest, logits, neg)
    e2 = jnp.max(el2, axis=-1, keepdims=True)
    i2 = jnp.min(jnp.where(jnp.logical_and(rest, el2 == e2), lane, big), axis=-1, keepdims=True)
    r = jnp.exp(e2 - e1)
    w1 = p_grp / (1.0 + r)
    w2 = w1 * r

    oh1 = (lane == i1).astype(F32)
    oh2 = (lane == i2).astype(F32)
    rr = lax.broadcasted_iota(jnp.int32, (tm, tm), 0)
    cc = lax.broadcasted_iota(jnp.int32, (tm, tm), 1)
    tril = (cc < rr).astype(BF16)
    before = jnp.dot(tril, (oh1 + oh2).astype(BF16), preferred_element_type=F32) + base_ref[...]
    rank1 = jnp.sum(oh1 * before, axis=-1, keepdims=True)
    rank2 = jnp.sum(oh2 * before, axis=-1, keepdims=True)
    base_new = base_ref[...] + jnp.sum(oh1 + oh2, axis=0, keepdims=True)
    base_ref[...] = base_new
    cnt_ref[...] = jnp.broadcast_to(base_new, cnt_ref.shape)

    ex1 = i1 - N_GROUPS
    ex2 = i2 - N_GROUPS
    info = jnp.zeros(logits.shape, F32)
    for idx, val in enumerate((ex1, ex2, rank1, rank2, w1, w2)):
        info = jnp.where(lane_i == idx, val, info)
    info_ref[...] = info


def _mix(x, of, ob, proj, ob2, mod, gn, n2, wa, wb, wo, wr, br, seq):
    t, d = x.shape
    tm = min(256, seq)
    tpb = seq // tm
    row = lambda i: (i, 0)
    const = lambda i: (0, 0)
    return pl.pallas_call(
        functools.partial(_mix_kernel, tm=tm),
        grid=(t // tm,),
        in_specs=[pl.BlockSpec((tm, d), row),
                  pl.BlockSpec((tm, 512), row),
                  pl.BlockSpec((tm, 512), row),
                  pl.BlockSpec((tm, 512), lambda i: (i, COL_GG // 4)),
                  pl.BlockSpec((tm, 512), row),
                  pl.BlockSpec((tm, 1024), lambda i: (i, COL_GA // 8)),
                  pl.BlockSpec((tm, 1024), lambda i: (i, COL_GB // 8)),
                  pl.BlockSpec((1, 6, d), lambda i: (i // tpb, 0, 0)),
                  pl.BlockSpec((1, GLA_DV), const),
                  pl.BlockSpec((1, d), const),
                  pl.BlockSpec((512, d), const),
                  pl.BlockSpec((512, d), const),
                  pl.BlockSpec((d, d), const),
                  pl.BlockSpec((d, LANES), const),
                  pl.BlockSpec((1, LANES), const)],
        out_specs=[pl.BlockSpec((tm, d), row),
                   pl.BlockSpec((tm, d), row),
                   pl.BlockSpec((tm, LANES), row),
                   pl.BlockSpec((8, LANES), const)],
        out_shape=[jax.ShapeDtypeStruct((t, d), F32),
                   jax.ShapeDtypeStruct((t, d), F32),
                   jax.ShapeDtypeStruct((t, LANES), F32),
                   jax.ShapeDtypeStruct((8, LANES), F32)],
        scratch_shapes=[pltpu.VMEM((1, LANES), F32)],
        compiler_params=_cparams(("arbitrary",)),
        name="mix",
    )(x, of, ob, proj, ob2, proj, proj, mod, gn, n2, wa, wb, wo, wr, br)


def _dispatch_kernel(pos_ref, zoff_ref, h_ref, xs_ref, zbuf_ref, sem_ref, zsem_ref, *, tb, n_tiles):
    @pl.when(pl.program_id(0) == 0)
    def _():
        zbuf_ref[...] = jnp.zeros_like(zbuf_ref)
        def zero_copy(e):
            off = pl.multiple_of(zoff_ref[e], SUBLANES)
            return pltpu.make_async_copy(zbuf_ref, xs_ref.at[pl.ds(off, ZERO_ROWS), :], zsem_ref)
        def tile_copy(i):
            return pltpu.make_async_copy(zbuf_ref, xs_ref.at[pl.ds(i * EXPERT_TILE, ZERO_ROWS), :], zsem_ref)

        def tile_start(i, carry):
            tile_copy(i).start()
            return carry

        def tile_wait(i, carry):
            tile_copy(i).wait()
            return carry

        for e in range(N_EXPERTS):
            zero_copy(e).start()
        lax.fori_loop(zoff_ref[N_EXPERTS], n_tiles + 1, tile_start, 0)
        for e in range(N_EXPERTS):
            zero_copy(e).wait()
        lax.fori_loop(zoff_ref[N_EXPERTS], n_tiles + 1, tile_wait, 0)

    def row_copy(r, k):
        return pltpu.make_async_copy(h_ref.at[pl.ds(r, 1), :],
                                     xs_ref.at[pl.ds(pos_ref[0, 0, 2 * r + k], 1), :], sem_ref)

    def start(r, carry):
        row_copy(r, 0).start()
        row_copy(r, 1).start()
        return carry

    def wait(r, carry):
        row_copy(r, 0).wait()
        row_copy(r, 1).wait()
        return carry

    lax.fori_loop(0, tb, start, 0)
    lax.fori_loop(0, tb, wait, 0)


def _dispatch(h2, pos3, zoff, n_rows, tb):
    t, d = h2.shape
    return pl.pallas_call(
        functools.partial(_dispatch_kernel, tb=tb, n_tiles=n_rows // EXPERT_TILE),
        grid_spec=pltpu.PrefetchScalarGridSpec(
            num_scalar_prefetch=0,
            grid=(t // tb,),
            in_specs=[pl.BlockSpec((1, 1, 2 * tb), lambda i: (i, 0, 0), memory_space=pltpu.SMEM),
                      pl.BlockSpec(memory_space=pltpu.SMEM),
                      pl.BlockSpec((tb, d), lambda i: (i, 0))],
            out_specs=pl.BlockSpec(memory_space=pl.ANY),
            scratch_shapes=[pltpu.VMEM((ZERO_ROWS, d), F32),
                            pltpu.SemaphoreType.DMA(()),
                            pltpu.SemaphoreType.DMA(())]),
        out_shape=jax.ShapeDtypeStruct((n_rows + ZERO_ROWS, d), F32),
        compiler_params=_cparams(("arbitrary",)),
        name="dispatch",
    )(pos3, zoff, h2)


def _expert_kernel(te_ref, nv_ref, xs_ref, w1_ref, w2_ref, ys_ref):
    @pl.when(pl.program_id(0) >= nv_ref[0])
    def _():
        ys_ref[...] = jnp.zeros_like(ys_ref)

    @pl.when(pl.program_id(0) < nv_ref[0])
    def _():
        xb = xs_ref[...].astype(BF16)
        hu = jnp.dot(xb, w1_ref[0], preferred_element_type=F32)
        hg = hu[:, :D_EXPERT]
        act = (hg * jax.nn.sigmoid(hg)) * hu[:, D_EXPERT:]
        ys_ref[...] = jnp.dot(act.astype(BF16), w2_ref[0], preferred_element_type=F32)


def _experts(xs, tile_expert, n_valid, w1, w2, n_tiles):
    d = xs.shape[1]
    blk = lambda i, te, nv: (jnp.minimum(i, nv[0] - 1), 0)
    return pl.pallas_call(
        _expert_kernel,
        grid_spec=pltpu.PrefetchScalarGridSpec(
            num_scalar_prefetch=2,
            grid=(n_tiles,),
            in_specs=[pl.BlockSpec((EXPERT_TILE, d), blk),
                      pl.BlockSpec((1, d, 2 * D_EXPERT), lambda i, te, nv: (te[i], 0, 0)),
                      pl.BlockSpec((1, D_EXPERT, d), lambda i, te, nv: (te[i], 0, 0))],
            out_specs=pl.BlockSpec((EXPERT_TILE, d), lambda i, te, nv: (i, 0))),
        out_shape=jax.ShapeDtypeStruct((n_tiles * EXPERT_TILE, d), F32),
        compiler_params=_cparams(("arbitrary",)),
        name="experts",
    )(tile_expert, n_valid, xs, w1, w2)


def _combine_kernel(pos_ref, ys_ref, x1_ref, info_ref, mod_ref, o_ref, buf_ref, sem_ref, *, tb):
    def row_copy(r, k):
        return pltpu.make_async_copy(ys_ref.at[pl.ds(pos_ref[0, 0, 2 * r + k], 1), :],
                                     buf_ref.at[k, pl.ds(r, 1), :], sem_ref)

    def start(r, carry):
        row_copy(r, 0).start()
        row_copy(r, 1).start()
        return carry

    def wait(r, carry):
        row_copy(r, 0).wait()
        row_copy(r, 1).wait()
        return carry

    lax.fori_loop(0, tb, start, 0)
    lax.fori_loop(0, tb, wait, 0)
    info = info_ref[...]
    y = buf_ref[0] * info[:, 4:5] + buf_ref[1] * info[:, 5:6]
    o_ref[...] = x1_ref[...] + mod_ref[0, 5:6, :] * y


def _combine(ys, pos3, x1, info, mod, seq, tb):
    t, d = x1.shape
    tpb = seq // tb
    return pl.pallas_call(
        functools.partial(_combine_kernel, tb=tb),
        grid_spec=pltpu.PrefetchScalarGridSpec(
            num_scalar_prefetch=0,
            grid=(t // tb,),
            in_specs=[pl.BlockSpec((1, 1, 2 * tb), lambda i: (i, 0, 0), memory_space=pltpu.SMEM),
                      pl.BlockSpec(memory_space=pl.ANY),
                      pl.BlockSpec((tb, d), lambda i: (i, 0)),
                      pl.BlockSpec((tb, LANES), lambda i: (i, 0)),
                      pl.BlockSpec((1, 6, d), lambda i: (i // tpb, 0, 0))],
            out_specs=pl.BlockSpec((tb, d), lambda i: (i, 0)),
            scratch_shapes=[pltpu.VMEM((2, tb, d), F32), pltpu.SemaphoreType.DMA(())]),
        out_shape=jax.ShapeDtypeStruct((t, d), F32),
        compiler_params=_cparams(("arbitrary",)),
        name="combine",
    )(pos3, ys, x1, info, mod)


def _moe(h2, x1, info, counts, mod, w1, w2, seq):
    t, d = h2.shape
    tb = min(256, seq)
    n_tiles = (2 * t + N_EXPERTS * (EXPERT_TILE - 1)) // EXPERT_TILE
    cnt = counts[0, N_GROUPS:N_GROUPS + N_EXPERTS].astype(jnp.int32)
    padded = ((cnt + EXPERT_TILE - 1) // EXPERT_TILE) * EXPERT_TILE
    ends = jnp.cumsum(padded)
    starts = ends - padded
    n_valid = (ends[-1] // EXPERT_TILE).astype(jnp.int32).reshape(1)
    tile_start = jnp.arange(n_tiles, dtype=jnp.int32) * EXPERT_TILE
    tile_expert = jnp.searchsorted(ends, tile_start, side="right").astype(jnp.int32)
    tile_expert = jnp.minimum(tile_expert, tile_expert[jnp.maximum(n_valid[0] - 1, 0)])
    eid = info[:, 0:2].astype(jnp.int32)
    pos = starts[eid] + info[:, 2:4].astype(jnp.int32)
    pos3 = pos.reshape(t // tb, 1, 2 * tb)
    zoff = jnp.concatenate([(starts + cnt) // SUBLANES * SUBLANES, n_valid]).astype(jnp.int32)
    xs = _dispatch(h2, pos3, zoff, n_tiles * EXPERT_TILE, tb)
    ys = _experts(xs, tile_expert, n_valid, w1, w2, n_tiles)
    return _combine(ys, pos3, x1, info, mod, seq, tb)


def _prep_w_in(w_in_l):
    w = w_in_l
    c = np.cumsum([0, 256, 256, 512, 512, 16, 16, 512, 512, 512, 1024, 1024])
    seg = lambda i: w[:, c[i]:c[i + 1]]
    gq, gk, gv, gg, lrf, lrb, dq, dk, dv, ga, gb = (seg(i) for i in range(11))
    main = jnp.concatenate([gq, gk, gv, dq, dk, gg, dv, ga, gb], axis=1).astype(BF16)
    lr = jnp.concatenate([lrf, lrb, jnp.zeros((w.shape[0], LANES - 2 * GLA_RANK), w.dtype)], axis=1)
    return main, lr.astype(BF16)


def kernel(x, c, w_ada, b_ada, norm1_g, norm2_g, w_in, gla_w_up, gla_b_up, gla_norm_g, diff_qnorm_g,
           diff_knorm_g, diff_lambda, diff_subnorm_g, rel_bias, w_branch_a, w_branch_b, w_out,
           w_router_group, b_router_group, w_router_expert, b_router_expert, w_expert_in, w_expert_out):
    bsz, seq, d = x.shape
    t = bsz * seq
    depth = w_ada.shape[0]
    mod_all = _ada(c, w_ada, b_ada).reshape(depth, bsz, 6, d)

    tq = min(256, seq)
    tiles, far = _bias_tables(rel_bias, tq, tq)
    grp = np.arange(1024) // DIFF_DH
    gmat = jnp.asarray((grp[:, None] == grp[None, :]).astype(np.float32) / DIFF_DH, dtype=BF16)

    xf = x.reshape(t, d)
    for l in range(depth):
        lam_init = 0.8 - 0.6 * math.exp(-0.3 * l)
        mod = mod_all[l]
        w_main, w_lr = _prep_w_in(w_in[l])
        qkg = jnp.concatenate([jnp.tile(diff_qnorm_g[l], 2 * DIFF_HEADS) * (DIFF_DH ** -0.5),
                               jnp.tile(diff_knorm_g[l], 2 * DIFF_HEADS)]).reshape(1, 1024)
        proj, lr = _inproj(xf, mod, norm1_g[l].reshape(1, d), w_main, w_lr, gmat, qkg, seq)
        o_f, o_b = _gla(proj, lr, gla_w_up[l], gla_b_up[l].reshape(2, 1, -1), bsz, seq)
        o_b2 = _attn(proj, tiles, far, diff_lambda[l], diff_subnorm_g[l].reshape(1, -1), bsz, seq, lam_init)
        w_r = jnp.concatenate([w_router_group[l], w_router_expert[l],
                               jnp.zeros((d, LANES - N_GROUPS - N_EXPERTS), F32)], axis=1)
        b_r = jnp.concatenate([b_router_group[l], b_router_expert[l],
                               jnp.zeros((LANES - N_GROUPS - N_EXPERTS,), F32)]).reshape(1, LANES)
        x1, h2, info, counts = _mix(xf, o_f, o_b, proj, o_b2, mod, gla_norm_g[l].reshape(1, -1),
                                    norm2_g[l].reshape(1, d), w_branch_a[l].astype(BF16),
                                    w_branch_b[l].astype(BF16), w_out[l].astype(BF16), w_r, b_r, seq)
        xf = _moe(h2, x1, info, counts, mod, w_expert_in[l].astype(BF16),
                  w_expert_out[l].astype(BF16), seq)
    return xf.reshape(bsz, seq, d)
```

```python
import functools
import math

import numpy as np
import jax
import jax.numpy as jnp
from jax import lax
from jax.experimental import pallas as pl
from jax.experimental.pallas import tpu as pltpu

F32 = jnp.float32
BF16 = jnp.bfloat16
HIGHEST = lax.Precision.HIGHEST

D_MODEL = 1024
DEPTH = 2
GLA_HEADS, GLA_DK, GLA_DV, GLA_RANK, GLA_TAU, GLA_CHUNK = 4, 64, 128, 16, 16.0, 64
DIFF_HEADS, DIFF_DH, DIFF_DV = 4, 64, 128
REL_BUCKETS, REL_MAX_DIST = 32, 128
N_GROUPS, EXPERTS_PER_GROUP = 4, 8
N_EXPERTS = N_GROUPS * EXPERTS_PER_GROUP
D_EXPERT = D_MODEL // 2
RMS_EPS = 1e-6
LANES = 128
VMEM_LIMIT = 52 * 1024 * 1024
EXPERT_TILE = 256

COL_GQK, COL_GV, COL_GA, COL_GB, COL_DKA, COL_GG, COL_DQ, COL_DV = 0, 4, 8, 16, 24, 32, 36, 40
N_PROJ = 44 * LANES
PROJ_TILE = 512
N_FEAT = 3
SOFTMAX_SAFE_RANGE = 60.0


def _cparams(sem, vmem=VMEM_LIMIT):
    return pltpu.CompilerParams(dimension_semantics=sem, vmem_limit_bytes=vmem)


def _ada_kernel(c_ref, w_ref, b_ref, o_ref):
    c = c_ref[...]
    ca = c * jax.nn.sigmoid(c)
    o_ref[0] = jnp.dot(ca, w_ref[0], precision=HIGHEST, preferred_element_type=F32) + b_ref[0]


def _ada(c, w_ada, b_ada):
    depth, d, n = w_ada.shape
    bsz = c.shape[0]
    tn = 1536
    return pl.pallas_call(
        _ada_kernel,
        grid=(depth, n // tn),
        in_specs=[pl.BlockSpec((bsz, d), lambda l, j: (0, 0)),
                  pl.BlockSpec((1, d, tn), lambda l, j: (l, 0, j)),
                  pl.BlockSpec((1, 1, tn), lambda l, j: (l, 0, j))],
        out_specs=pl.BlockSpec((1, bsz, tn), lambda l, j: (l, 0, j)),
        out_shape=jax.ShapeDtypeStruct((depth, bsz, n), F32),
        compiler_params=_cparams(("parallel", "parallel")),
        name="ada",
    )(c, w_ada, b_ada.reshape(depth, 1, n))


def _inproj_kernel(x_ref, mod_ref, g_ref, w_ref, wlr_ref, gmat_ref, qkg_ref, proj_ref, lr_ref, vt_ref):
    x = x_ref[...]
    ms = jnp.mean(x * x, axis=-1, keepdims=True)
    h = x * lax.rsqrt(ms + RMS_EPS) * g_ref[...]
    h = h * (1.0 + mod_ref[0, 1:2, :]) + mod_ref[0, 0:1, :]
    hb = h.astype(BF16)
    w = PROJ_TILE
    qk_tiles = {COL_DQ * LANES // w: 0, COL_DKA * LANES // w: 1, COL_DKA * LANES // w + 1: 2}
    for j in range(N_PROJ // w):
        acc = jnp.dot(hb, w_ref[:, j * w:(j + 1) * w], preferred_element_type=F32)
        if j in qk_tiles:
            ms2 = jnp.dot((acc * acc).astype(BF16), gmat_ref[...], preferred_element_type=F32)
            acc = acc * lax.rsqrt(ms2 + RMS_EPS) * qkg_ref[qk_tiles[j], 0:1, :] + qkg_ref[qk_tiles[j], 1:2, :]
        proj_ref[:, j * w:(j + 1) * w] = acc.astype(BF16)
        if j == COL_DV * LANES // w:
            vt_ref[0] = acc.T.astype(BF16)
    lr_ref[...] = jnp.dot(hb, wlr_ref[...], preferred_element_type=F32)


def _inproj(x, mod, g, w, wlr, gmat, qkg, seq):
    t, d = x.shape
    tm = min(256, seq)
    tpb = seq // tm
    return pl.pallas_call(
        _inproj_kernel,
        grid=(t // tm,),
        in_specs=[pl.BlockSpec((tm, d), lambda i: (i, 0)),
                  pl.BlockSpec((1, 6, d), lambda i: (i // tpb, 0, 0)),
                  pl.BlockSpec((1, d), lambda i: (0, 0)),
                  pl.BlockSpec((d, N_PROJ), lambda i: (0, 0)),
                  pl.BlockSpec((d, LANES), lambda i: (0, 0)),
                  pl.BlockSpec((PROJ_TILE, PROJ_TILE), lambda i: (0, 0)),
                  pl.BlockSpec((3, 2, PROJ_TILE), lambda i: (0, 0, 0))],
        out_specs=[pl.BlockSpec((tm, N_PROJ), lambda i: (i, 0)),
                   pl.BlockSpec((tm, LANES), lambda i: (i, 0)),
                   pl.BlockSpec((1, DIFF_HEADS * DIFF_DV, tm), lambda i: (i, 0, 0))],
        out_shape=[jax.ShapeDtypeStruct((t, N_PROJ), BF16),
                   jax.ShapeDtypeStruct((t, LANES), F32),
                   jax.ShapeDtypeStruct((t // tm, DIFF_HEADS * DIFF_DV, tm), BF16)],
        compiler_params=_cparams(("parallel",)),
        name="inproj",
    )(x, mod, g, w, wlr, gmat, qkg)


def _gla_chunk(qk, v, lr, wup, bup, smat, mask, tot_row, s_ref):
    q = qk[:, :GLA_HEADS * GLA_DK].astype(F32)
    k = qk[:, GLA_HEADS * GLA_DK:].astype(F32)
    z = jnp.dot(lr, wup, precision=HIGHEST, preferred_element_type=F32) + bup
    lg = (jnp.minimum(z, 0.0) - jnp.log(1.0 + jnp.exp(-jnp.abs(z)))) * (1.0 / GLA_TAU)
    cum = jnp.dot(smat, lg, precision=HIGHEST, preferred_element_type=F32)
    dec = jnp.exp(cum[tot_row:tot_row + 1, :])
    q_dec = (q * jnp.exp(cum) * (GLA_DK ** -0.5)).astype(BF16)
    k_inv = k * jnp.exp(-cum)
    k_end = (k_inv * dec).astype(BF16)
    k_inv = k_inv.astype(BF16)
    outs = []
    for h in range(GLA_HEADS):
        sl = slice(h * GLA_DK, (h + 1) * GLA_DK)
        qh, kh, keh = q_dec[:, sl], k_inv[:, sl], k_end[:, sl]
        vh = v[:, h * GLA_DV:(h + 1) * GLA_DV]
        a = lax.dot_general(qh, kh, (((1,), (1,)), ((), ())), preferred_element_type=F32)
        a = jnp.where(mask, a, 0.0).astype(BF16)
        st = s_ref[h]
        o = jnp.dot(a, vh, preferred_element_type=F32)
        o = o + lax.dot_general(qh, st.astype(BF16), (((1,), (1,)), ((), ())),
                                preferred_element_type=F32)
        kv_t = lax.dot_general(vh, keh, (((0,), (0,)), ((), ())), preferred_element_type=F32)
        s_ref[h] = st * dec[:, sl] + kv_t
        outs.append(o)
    return jnp.concatenate(outs, axis=-1)


def _gla_kernel(qkf_ref, vf_ref, lrf_ref, qkb_ref, vb_ref, lrb_ref, wup_ref, bup_ref,
                of_ref, ob_ref, sf_ref, sb_ref, *, n_chunks):
    @pl.when(pl.program_id(1) == 0)
    def _():
        sf_ref[...] = jnp.zeros_like(sf_ref)
        sb_ref[...] = jnp.zeros_like(sb_ref)

    c = GLA_CHUNK
    row = lax.broadcasted_iota(jnp.int32, (c, c), 0)
    col = lax.broadcasted_iota(jnp.int32, (c, c), 1)
    pre = (col <= row).astype(F32)
    suf = (col >= row).astype(F32)
    mask_f = col <= row
    mask_b = col > row

    def body(ci, carry):
        rf = pl.multiple_of(ci * c, c)
        rb = pl.multiple_of((n_chunks - 1 - ci) * c, c)
        of = _gla_chunk(qkf_ref[pl.ds(rf, c), :], vf_ref[pl.ds(rf, c), :],
                        lrf_ref[pl.ds(rf, c), :][:, 0:GLA_RANK], wup_ref[0], bup_ref[0],
                        pre, mask_f, c - 1, sf_ref)
        of_ref[pl.ds(rf, c), :] = of.astype(of_ref.dtype)
        ob = _gla_chunk(qkb_ref[pl.ds(rb, c), :], vb_ref[pl.ds(rb, c), :],
                        lrb_ref[pl.ds(rb, c), :][:, GLA_RANK:2 * GLA_RANK], wup_ref[1], bup_ref[1],
                        suf, mask_b, 0, sb_ref)
        ob_ref[pl.ds(rb, c), :] = ob.astype(ob_ref.dtype)
        return carry

    lax.fori_loop(0, n_chunks, body, 0)


def _gla(proj, lr, wup, bup, bsz, seq):
    t = proj.shape[0]
    lb = min(512, seq)
    nblk = seq // lb
    vw = GLA_HEADS * GLA_DV
    fwd = lambda b, i: b * nblk + i
    bwd = lambda b, i: b * nblk + nblk - 1 - i
    return pl.pallas_call(
        functools.partial(_gla_kernel, n_chunks=lb // GLA_CHUNK),
        grid=(bsz, nblk),
        in_specs=[pl.BlockSpec((lb, 512), lambda b, i: (fwd(b, i), COL_GQK // 4)),
                  pl.BlockSpec((lb, 512), lambda b, i: (fwd(b, i), COL_GV // 4)),
                  pl.BlockSpec((lb, LANES), lambda b, i: (fwd(b, i), 0)),
                  pl.BlockSpec((lb, 512), lambda b, i: (bwd(b, i), COL_GQK // 4)),
                  pl.BlockSpec((lb, 512), lambda b, i: (bwd(b, i), COL_GV // 4)),
                  pl.BlockSpec((lb, LANES), lambda b, i: (bwd(b, i), 0)),
                  pl.BlockSpec((2, GLA_RANK, GLA_HEADS * GLA_DK), lambda b, i: (0, 0, 0)),
                  pl.BlockSpec((2, 1, GLA_HEADS * GLA_DK), lambda b, i: (0, 0, 0))],
        out_specs=[pl.BlockSpec((lb, vw), lambda b, i: (fwd(b, i), 0)),
                   pl.BlockSpec((lb, vw), lambda b, i: (bwd(b, i), 0))],
        out_shape=[jax.ShapeDtypeStruct((t, vw), BF16), jax.ShapeDtypeStruct((t, vw), BF16)],
        scratch_shapes=[pltpu.VMEM((GLA_HEADS, GLA_DV, GLA_DK), F32),
                        pltpu.VMEM((GLA_HEADS, GLA_DV, GLA_DK), F32)],
        compiler_params=_cparams(("parallel", "arbitrary")),
        name="gla",
    )(proj, proj, lr, proj, proj, lr, wup, bup)


def _t5_bucket_np(rel):
    nb = REL_BUCKETS // 2
    max_exact = nb // 2
    ret = np.where(rel > 0, nb, 0)
    n = np.abs(rel)
    nf = np.maximum(n, 1).astype(np.float64)
    large = max_exact + (np.log(nf / max_exact) / math.log(REL_MAX_DIST / max_exact)
                         * (nb - max_exact)).astype(np.int64)
    large = np.minimum(large, nb - 1)
    return ret + np.where(n < max_exact, n, large)


def _attn_kernel(fast_ref, far_ref, q_ref, k_ref, vt_ref, tile_ref, feat_ref, lam_ref, sg_ref, o_ref,
                 m_ref, l_ref, acc_ref, *, tq, tk, nk, lam_init):
    qi = pl.program_id(2)
    head = pl.program_id(1)
    q = q_ref[...]
    n_sub = tk // vt_ref.shape[-1]
    ts = tk // n_sub
    tasks = [(u, c) for u in range(n_sub) for c in range(2)]
    near, lo, hi = 0, 1, 2

    def q_aug(variant):
        feat = jnp.broadcast_to(feat_ref[0, variant:variant + 1, :], (tq, DIFF_DH)).astype(BF16)
        return [jnp.concatenate([q[:, c * DIFF_DH:(c + 1) * DIFF_DH], feat], axis=1) for c in range(2)]

    def logits(kb, u, c, qa):
        r0 = pl.multiple_of(kb * tk + u * ts, ts)
        k = k_ref[pl.ds(r0, ts), c * LANES:(c + 1) * LANES]
        return lax.dot_general(k, qa[c], (((1,), (1,)), ((), ())), preferred_element_type=F32)

    def pipelined(blocks):
        work = [(kb, qa, bias, step, u, c) for kb, qa, bias, step in blocks for u, c in tasks]
        s_next = logits(work[0][0], work[0][4], work[0][5], work[0][1])
        for t, (kb, qa, bias, step, u, c) in enumerate(work):
            s = s_next
            if t + 1 < len(work):
                nxt = work[t + 1]
                s_next = logits(nxt[0], nxt[4], nxt[5], nxt[1])
            if bias is not None:
                s = s + bias[u * ts:(u + 1) * ts, :]
            step(s, c, vt_ref[kb * n_sub + u])

    def bounded_step(s, c, vt):
        p = jnp.exp2(s)
        l_ref[c] = l_ref[c] + jnp.sum(p, axis=0, keepdims=True)
        acc_ref[c] = acc_ref[c] + jnp.dot(vt, p.astype(BF16), preferred_element_type=F32)

    def online_step(cst):
        def step(s, c, vt):
            m_old = m_ref[c]
            m_new = jnp.maximum(m_old, jnp.max(s, axis=0, keepdims=True) + cst)
            p = jnp.exp2(s - (m_new - cst))
            alpha = jnp.exp2(m_old - m_new)
            l_ref[c] = alpha * l_ref[c] + jnp.sum(p, axis=0, keepdims=True)
            acc_ref[c] = alpha * acc_ref[c] + jnp.dot(vt, p.astype(BF16), preferred_element_type=F32)
            m_ref[c] = m_new
        return step

    def sweep(far_lo_step, far_hi_step, near_step, group):
        q_near, q_lo, q_hi = q_aug(near), q_aug(lo), q_aug(hi)

        def far(start, count, qa, step):
            def body(i, carry):
                pipelined([(start + group * i + g, qa, None, step) for g in range(group)])
                return carry

            lax.fori_loop(0, count // group, body, 0)
            for rem in range(1, group):
                @pl.when(count % group == rem)
                def _(rem=rem):
                    pipelined([(start + count - rem + g, qa, None, step) for g in range(rem)])

        far(0, jnp.maximum(qi - 1, 0), q_lo, far_lo_step)
        for has_left in (False, True):
            for has_right in (False, True):
                dlts = ([-1] if has_left else []) + [0] + ([1] if has_right else [])

                left_ok = (qi > 0) if has_left else (qi == 0)
                right_ok = (qi < nk - 1) if has_right else (qi >= nk - 1)

                @pl.when(jnp.logical_and(left_ok, right_ok))
                def _(dlts=dlts):
                    pipelined([(qi + dlt, q_near, tile_ref.at[0, dlt + 1], near_step) for dlt in dlts])
        hi_start = jnp.minimum(qi + 2, nk)
        far(hi_start, nk - hi_start, q_hi, far_hi_step)

    m_ref[...] = jnp.full_like(m_ref, -1e30)
    l_ref[...] = jnp.zeros_like(l_ref)
    acc_ref[...] = jnp.zeros_like(acc_ref)

    @pl.when(fast_ref[0] == 1)
    def _():
        sweep(bounded_step, bounded_step, bounded_step, 2)

    @pl.when(fast_ref[0] == 0)
    def _():
        sweep(online_step(far_ref[head, 0]), online_step(far_ref[head, 1]), online_step(0.0), 1)

    lv = lam_ref[...]
    lam = (jnp.exp(jnp.sum(lv[0:1] * lv[1:2], axis=-1, keepdims=True))
           - jnp.exp(jnp.sum(lv[2:3] * lv[3:4], axis=-1, keepdims=True)) + lam_init)
    out_t = acc_ref[0] * (1.0 / l_ref[0]) - acc_ref[1] * (lam / l_ref[1])
    out = out_t.T
    ms = jnp.mean(out * out, axis=-1, keepdims=True)
    out = out * lax.rsqrt(ms + RMS_EPS) * sg_ref[...] * (1.0 - lam_init)
    o_ref[...] = out.astype(o_ref.dtype)


def _attn_tile(seq):
    return min(512, seq)


def _attn(proj, vt, tiles, far, fast, feat, lam_vecs, subn_g, bsz, seq, lam_init):
    t = proj.shape[0]
    tq = tk = _attn_tile(seq)
    nq = seq // tq
    nk = seq // tk
    vb = vt.shape[-1]
    return pl.pallas_call(
        functools.partial(_attn_kernel, tq=tq, tk=tk, nk=nk, lam_init=lam_init),
        grid=(bsz, DIFF_HEADS, nq),
        in_specs=[pl.BlockSpec(memory_space=pltpu.SMEM),
                  pl.BlockSpec(memory_space=pltpu.SMEM),
                  pl.BlockSpec((tq, LANES), lambda b, h, i: (b * nq + i, COL_DQ + h)),
                  pl.BlockSpec((seq, 2 * LANES), lambda b, h, i: (b, COL_DKA // 2 + h)),
                  pl.BlockSpec((seq // vb, DIFF_DV, vb), lambda b, h, i: (b, h, 0)),
                  pl.BlockSpec((1, 3, tk, tq), lambda b, h, i: (h, 0, 0, 0)),
                  pl.BlockSpec((1, 3, DIFF_DH), lambda b, h, i: (h, 0, 0)),
                  pl.BlockSpec((4, DIFF_DH), lambda b, h, i: (0, 0)),
                  pl.BlockSpec((1, DIFF_DV), lambda b, h, i: (0, 0))],
        out_specs=pl.BlockSpec((tq, DIFF_DV), lambda b, h, i: (b * nq + i, h)),
        out_shape=jax.ShapeDtypeStruct((t, DIFF_HEADS * DIFF_DV), BF16),
        scratch_shapes=[pltpu.VMEM((2, 1, tq), F32), pltpu.VMEM((2, 1, tq), F32),
                        pltpu.VMEM((2, DIFF_DV, tq), F32)],
        compiler_params=_cparams(("parallel", "parallel", "arbitrary")),
        name="attn",
    )(fast, far, proj, proj, vt, tiles, feat, lam_vecs, subn_g)


def _bias_tables(rel_bias, tq, tk):
    n = tk + tq - 1
    log2e = math.log2(math.e)
    tiles = []
    for d in (-1, 0, 1):
        onehot = np.eye(REL_BUCKETS, dtype=np.float32)[_t5_bucket_np(d * tk + np.arange(n) - (tq - 1))]
        g = jnp.dot(jnp.asarray(onehot), rel_bias.astype(F32), precision=HIGHEST).T * log2e
        hank = jnp.tile(g, (1, tk + 1))[:, :tk * (n + 1)].reshape(-1, tk, n + 1)
        tiles.append(hank[:, :, :tq][:, :, ::-1])
    tiles = jnp.stack(tiles, axis=1)
    nb = REL_BUCKETS // 2
    far = jnp.stack([rel_bias[nb - 1], rel_bias[2 * nb - 1]], axis=-1).astype(F32) * log2e
    return tiles, far


def _softmax_features(far, rel_bias, qn_g, kn_g):
    log2e = math.log2(math.e)
    bound = (1.02 * DIFF_DH ** 0.5 * log2e * jnp.max(jnp.abs(qn_g)) * jnp.max(jnp.abs(kn_g))
             + log2e * jnp.max(jnp.abs(rel_bias)))
    fast = bound <= SOFTMAX_SAFE_RANGE
    neg_bound = -bound.astype(BF16).astype(F32)
    c_hi = far.astype(BF16).astype(F32)
    c_lo = (far - c_hi).astype(BF16).astype(F32)
    zeros = jnp.zeros_like(c_hi[:, 0])
    nbv = jnp.broadcast_to(neg_bound, zeros.shape)
    rows = jnp.stack([jnp.stack([nbv, zeros, zeros], -1),
                      jnp.stack([nbv, c_hi[:, 0], c_lo[:, 0]], -1),
                      jnp.stack([nbv, c_hi[:, 1], c_lo[:, 1]], -1)], axis=1)
    rows = jnp.where(fast, rows, 0.0)
    feat = jnp.pad(rows, ((0, 0), (0, 0), (0, DIFF_DH - N_FEAT)))
    return fast.astype(jnp.int32).reshape(1), feat


def _mix_kernel(x_ref, of_ref, ob_ref, gg_ref, ob2_ref, ga_ref, gb_ref, mod_ref, gn_ref, n2_ref,
                wa_ref, wb_ref, wo_ref, wr_ref, br_ref,
                x1_ref, h2_ref, info_ref, cnt_ref, base_ref, *, tm):
    @pl.when(pl.program_id(0) == 0)
    def _():
        base_ref[...] = jnp.zeros_like(base_ref)

    osum = of_ref[...].astype(F32) + ob_ref[...].astype(F32)
    parts = []
    for h in range(GLA_HEADS):
        sl = osum[:, h * GLA_DV:(h + 1) * GLA_DV]
        ms = jnp.mean(sl * sl, axis=-1, keepdims=True)
        parts.append(sl * lax.rsqrt(ms + RMS_EPS) * gn_ref[...])
    gg = gg_ref[...].astype(F32)
    o_a = jnp.concatenate(parts, axis=-1) * (gg * jax.nn.sigmoid(gg))
    y_a = jnp.dot(o_a.astype(BF16), wa_ref[...], preferred_element_type=F32)
    y_b = jnp.dot(ob2_ref[...], wb_ref[...], preferred_element_type=F32)
    merged = (jax.nn.sigmoid(ga_ref[...].astype(F32)) * y_a
              + jax.nn.sigmoid(gb_ref[...].astype(F32)) * y_b)
    y = jnp.dot(merged.astype(BF16), wo_ref[...], preferred_element_type=F32)
    x1 = x_ref[...] + mod_ref[0, 2:3, :] * y
    x1_ref[...] = x1

    ms = jnp.mean(x1 * x1, axis=-1, keepdims=True)
    h2 = x1 * lax.rsqrt(ms + RMS_EPS) * n2_ref[...]
    h2 = h2 * (1.0 + mod_ref[0, 4:5, :]) + mod_ref[0, 3:4, :]
    h2_ref[...] = h2

    logits = jnp.dot(h2, wr_ref[...], precision=HIGHEST, preferred_element_type=F32) + br_ref[...]
    lane_i = lax.broadcasted_iota(jnp.int32, logits.shape, 1)
    lane = lane_i.astype(F32)
    neg = jnp.float32(-3e38)
    big = jnp.float32(1 << 20)
    is_g = lane < N_GROUPS
    gl = jnp.where(is_g, logits, neg)
    gmax = jnp.max(gl, axis=-1, keepdims=True)
    grp = jnp.min(jnp.where(jnp.logical_and(is_g, gl == gmax), lane, big), axis=-1, keepdims=True)
    p_grp = 1.0 / jnp.sum(jnp.where(is_g, jnp.exp(gl - gmax), 0.0), axis=-1, keepdims=True)
    lo = N_GROUPS + grp * EXPERTS_PER_GROUP
    in_grp = jnp.logical_and(lane >= lo, lane < lo + EXPERTS_PER_GROUP)
    el = jnp.where(in_grp, logits, neg)
    e1 = jnp.max(el, axis=-1, keepdims=True)
    i1 = jnp.min(jnp.where(jnp.logical_and(in_grp, el == e1), lane, big), axis=-1, keepdims=True)
    rest = jnp.logical_and(in_grp, lane != i1)
    el2 = jnp.where(rest, logits, neg)
    e2 = jnp.max(el2, axis=-1, keepdims=True)
    i2 = jnp.min(jnp.where(jnp.logical_and(rest, el2 == e2), lane, big), axis=-1, keepdims=True)
    r = jnp.exp(e2 - e1)
    w1 = p_grp / (1.0 + r)
    w2 = w1 * r

    oh1 = (lane == i1).astype(F32)
    oh2 = (lane == i2).astype(F32)
    rr = lax.broadcasted_iota(jnp.int32, (tm, tm), 0)
    cc = lax.broadcasted_iota(jnp.int32, (tm, tm), 1)
    tril = (cc < rr).astype(BF16)
    before = jnp.dot(tril, (oh1 + oh2).astype(BF16), preferred_element_type=F32) + base_ref[...]
    rank1 = jnp.sum(oh1 * before, axis=-1, keepdims=True)
    rank2 = jnp.sum(oh2 * before, axis=-1, keepdims=True)
    base_new = base_ref[...] + jnp.sum(oh1 + oh2, axis=0, keepdims=True)
    base_ref[...] = base_new
    cnt_ref[...] = jnp.broadcast_to(base_new, cnt_ref.shape)

    ex1 = i1 - N_GROUPS
    ex2 = i2 - N_GROUPS
    info = jnp.zeros(logits.shape, F32)
    for idx, val in enumerate((ex1, ex2, rank1, rank2, w1, w2)):
        info = jnp.where(lane_i == idx, val, info)
    info_ref[...] = info


def _mix(x, of, ob, proj, ob2, mod, gn, n2, wa, wb, wo, wr, br, seq):
    t, d = x.shape
    tm = min(256, seq)
    tpb = seq // tm
    row = lambda i: (i, 0)
    const = lambda i: (0, 0)
    return pl.pallas_call(
        functools.partial(_mix_kernel, tm=tm),
        grid=(t // tm,),
        in_specs=[pl.BlockSpec((tm, d), row),
                  pl.BlockSpec((tm, 512), row),
                  pl.BlockSpec((tm, 512), row),
                  pl.BlockSpec((tm, 512), lambda i: (i, COL_GG // 4)),
                  pl.BlockSpec((tm, 512), row),
                  pl.BlockSpec((tm, 1024), lambda i: (i, COL_GA // 8)),
                  pl.BlockSpec((tm, 1024), lambda i: (i, COL_GB // 8)),
                  pl.BlockSpec((1, 6, d), lambda i: (i // tpb, 0, 0)),
                  pl.BlockSpec((1, GLA_DV), const),
                  pl.BlockSpec((1, d), const),
                  pl.BlockSpec((512, d), const),
                  pl.BlockSpec((512, d), const),
                  pl.BlockSpec((d, d), const),
                  pl.BlockSpec((d, LANES), const),
                  pl.BlockSpec((1, LANES), const)],
        out_specs=[pl.BlockSpec((tm, d), row),
                   pl.BlockSpec((tm, d), row),
                   pl.BlockSpec((tm, LANES), row),
                   pl.BlockSpec((8, LANES), const)],
        out_shape=[jax.ShapeDtypeStruct((t, d), F32),
                   jax.ShapeDtypeStruct((t, d), F32),
                   jax.ShapeDtypeStruct((t, LANES), F32),
                   jax.ShapeDtypeStruct((8, LANES), F32)],
        scratch_shapes=[pltpu.VMEM((1, LANES), F32)],
        compiler_params=_cparams(("arbitrary",)),
        name="mix",
    )(x, of, ob, proj, ob2, proj, proj, mod, gn, n2, wa, wb, wo, wr, br)


def _dispatch_kernel(pos_ref, h_ref, zeros_ref, xs_ref, sem_ref, *, tb):
    del zeros_ref

    def row_copy(r, k):
        return pltpu.make_async_copy(h_ref.at[pl.ds(r, 1), :],
                                     xs_ref.at[pl.ds(pos_ref[0, 0, 2 * r + k], 1), :], sem_ref)

    def start(r, carry):
        row_copy(r, 0).start()
        row_copy(r, 1).start()
        return carry

    def wait(r, carry):
        row_copy(r, 0).wait()
        row_copy(r, 1).wait()
        return carry

    lax.fori_loop(0, tb, start, 0)
    lax.fori_loop(0, tb, wait, 0)


def _dispatch(h2, pos3, n_rows, tb):
    t, d = h2.shape
    return pl.pallas_call(
        functools.partial(_dispatch_kernel, tb=tb),
        grid_spec=pltpu.PrefetchScalarGridSpec(
            num_scalar_prefetch=0,
            grid=(t // tb,),
            in_specs=[pl.BlockSpec((1, 1, 2 * tb), lambda i: (i, 0, 0), memory_space=pltpu.SMEM),
                      pl.BlockSpec((tb, d), lambda i: (i, 0)),
                      pl.BlockSpec(memory_space=pl.ANY)],
            out_specs=pl.BlockSpec(memory_space=pl.ANY),
            scratch_shapes=[pltpu.SemaphoreType.DMA(())]),
        out_shape=jax.ShapeDtypeStruct((n_rows, d), F32),
        input_output_aliases={2: 0},
        compiler_params=_cparams(("arbitrary",)),
        name="dispatch",
    )(pos3, h2, jnp.zeros((n_rows, d), F32))


def _expert_kernel(te_ref, nv_ref, xs_ref, w1_ref, w2_ref, ys_ref):
    @pl.when(pl.program_id(0) >= nv_ref[0])
    def _():
        ys_ref[...] = jnp.zeros_like(ys_ref)

    @pl.when(pl.program_id(0) < nv_ref[0])
    def _():
        xb = xs_ref[...].astype(BF16)
        hu = jnp.dot(xb, w1_ref[0], preferred_element_type=F32)
        hg = hu[:, :D_EXPERT]
        act = (hg * jax.nn.sigmoid(hg)) * hu[:, D_EXPERT:]
        ys_ref[...] = jnp.dot(act.astype(BF16), w2_ref[0], preferred_element_type=F32)


def _experts(xs, tile_expert, n_valid, w1, w2, n_tiles):
    d = xs.shape[1]
    blk = lambda i, te, nv: (jnp.minimum(i, nv[0] - 1), 0)
    return pl.pallas_call(
        _expert_kernel,
        grid_spec=pltpu.PrefetchScalarGridSpec(
            num_scalar_prefetch=2,
            grid=(n_tiles,),
            in_specs=[pl.BlockSpec((EXPERT_TILE, d), blk),
                      pl.BlockSpec((1, d, 2 * D_EXPERT), lambda i, te, nv: (te[i], 0, 0)),
                      pl.BlockSpec((1, D_EXPERT, d), lambda i, te, nv: (te[i], 0, 0))],
            out_specs=pl.BlockSpec((EXPERT_TILE, d), lambda i, te, nv: (i, 0))),
        out_shape=jax.ShapeDtypeStruct((n_tiles * EXPERT_TILE, d), F32),
        compiler_params=_cparams(("arbitrary",)),
        name="experts",
    )(tile_expert, n_valid, xs, w1, w2)


def _combine_kernel(pos_ref, ys_ref, x1_ref, info_ref, mod_ref, o_ref, buf_ref, sem_ref, *, tb):
    def row_copy(r, k):
        return pltpu.make_async_copy(ys_ref.at[pl.ds(pos_ref[0, 0, 2 * r + k], 1), :],
                                     buf_ref.at[k, pl.ds(r, 1), :], sem_ref)

    def start(r, carry):
        row_copy(r, 0).start()
        row_copy(r, 1).start()
        return carry

    def wait(r, carry):
        row_copy(r, 0).wait()
        row_copy(r, 1).wait()
        return carry

    lax.fori_loop(0, tb, start, 0)
    lax.fori_loop(0, tb, wait, 0)
    info = info_ref[...]
    y = buf_ref[0] * info[:, 4:5] + buf_ref[1] * info[:, 5:6]
    o_ref[...] = x1_ref[...] + mod_ref[0, 5:6, :] * y


def _combine(ys, pos3, x1, info, mod, seq, tb):
    t, d = x1.shape
    tpb = seq // tb
    return pl.pallas_call(
        functools.partial(_combine_kernel, tb=tb),
        grid_spec=pltpu.PrefetchScalarGridSpec(
            num_scalar_prefetch=0,
            grid=(t // tb,),
            in_specs=[pl.BlockSpec((1, 1, 2 * tb), lambda i: (i, 0, 0), memory_space=pltpu.SMEM),
                      pl.BlockSpec(memory_space=pl.ANY),
                      pl.BlockSpec((tb, d), lambda i: (i, 0)),
                      pl.BlockSpec((tb, LANES), lambda i: (i, 0)),
                      pl.BlockSpec((1, 6, d), lambda i: (i // tpb, 0, 0))],
            out_specs=pl.BlockSpec((tb, d), lambda i: (i, 0)),
            scratch_shapes=[pltpu.VMEM((2, tb, d), F32), pltpu.SemaphoreType.DMA(())]),
        out_shape=jax.ShapeDtypeStruct((t, d), F32),
        compiler_params=_cparams(("arbitrary",)),
        name="combine",
    )(pos3, ys, x1, info, mod)


def _moe(h2, x1, info, counts, mod, w1, w2, seq):
    t, d = h2.shape
    tb = min(256, seq)
    n_tiles = (2 * t + N_EXPERTS * (EXPERT_TILE - 1)) // EXPERT_TILE
    cnt = counts[0, N_GROUPS:N_GROUPS + N_EXPERTS].astype(jnp.int32)
    padded = ((cnt + EXPERT_TILE - 1) // EXPERT_TILE) * EXPERT_TILE
    ends = jnp.cumsum(padded)
    starts = ends - padded
    n_valid = (ends[-1] // EXPERT_TILE).astype(jnp.int32).reshape(1)
    tile_start = jnp.arange(n_tiles, dtype=jnp.int32) * EXPERT_TILE
    tile_expert = jnp.sum(ends[None, :] <= tile_start[:, None], axis=1).astype(jnp.int32)
    tile_expert = jnp.minimum(tile_expert, tile_expert[jnp.maximum(n_valid[0] - 1, 0)])
    eid = info[:, 0:2].astype(jnp.int32)
    pos = starts[eid] + info[:, 2:4].astype(jnp.int32)
    pos3 = pos.reshape(t // tb, 1, 2 * tb)
    xs = _dispatch(h2, pos3, n_tiles * EXPERT_TILE, tb)
    ys = _experts(xs, tile_expert, n_valid, w1, w2, n_tiles)
    return _combine(ys, pos3, x1, info, mod, seq, tb)


def _qk_norm_rows(qn_g, kn_g):
    zeros = jnp.zeros((DIFF_DH,), F32)
    feat = jnp.asarray(np.arange(DIFF_DH) < N_FEAT, F32)
    q_gain = jnp.tile(qn_g, 2 * DIFF_HEADS) * (DIFF_DH ** -0.5 * math.log2(math.e))
    k_gain = jnp.tile(jnp.concatenate([kn_g, zeros]), DIFF_HEADS)
    k_add = jnp.tile(jnp.concatenate([zeros, feat]), DIFF_HEADS)
    return jnp.stack([jnp.stack([q_gain, jnp.zeros_like(q_gain)]),
                      jnp.stack([k_gain, k_add]), jnp.stack([k_gain, k_add])])


def _prep_w_in(w_in_l):
    w = w_in_l
    c = np.cumsum([0, 256, 256, 512, 512, 16, 16, 512, 512, 512, 1024, 1024])
    seg = lambda i: w[:, c[i]:c[i + 1]]
    gq, gk, gv, gg, lrf, lrb, dq, dk, dv, ga, gb = (seg(i) for i in range(11))
    dk_aug = jnp.pad(dk.reshape(-1, 2 * DIFF_HEADS, 1, DIFF_DH), ((0, 0), (0, 0), (0, 1), (0, 0)))
    dk_aug = dk_aug.reshape(-1, 4 * DIFF_HEADS * DIFF_DH)
    main = jnp.concatenate([gq, gk, gv, ga, gb, dk_aug, gg, dq, dv], axis=1).astype(BF16)
    lr = jnp.concatenate([lrf, lrb, jnp.zeros((w.shape[0], LANES - 2 * GLA_RANK), w.dtype)], axis=1)
    return main, lr.astype(BF16)


def kernel(x, c, w_ada, b_ada, norm1_g, norm2_g, w_in, gla_w_up, gla_b_up, gla_norm_g, diff_qnorm_g,
           diff_knorm_g, diff_lambda, diff_subnorm_g, rel_bias, w_branch_a, w_branch_b, w_out,
           w_router_group, b_router_group, w_router_expert, b_router_expert, w_expert_in, w_expert_out):
    bsz, seq, d = x.shape
    t = bsz * seq
    depth = w_ada.shape[0]
    mod_all = _ada(c, w_ada, b_ada).reshape(depth, bsz, 6, d)

    tq = _attn_tile(seq)
    tiles, far = _bias_tables(rel_bias, tq, tq)
    grp = np.arange(PROJ_TILE) // DIFF_DH
    gmat = jnp.asarray((grp[:, None] == grp[None, :]).astype(np.float32) / DIFF_DH, dtype=BF16)

    xf = x.reshape(t, d)
    for l in range(depth):
        lam_init = 0.8 - 0.6 * math.exp(-0.3 * l)
        mod = mod_all[l]
        w_main, w_lr = _prep_w_in(w_in[l])
        qkg = _qk_norm_rows(diff_qnorm_g[l], diff_knorm_g[l])
        proj, lr, vt = _inproj(xf, mod, norm1_g[l].reshape(1, d), w_main, w_lr, gmat, qkg, seq)
        o_f, o_b = _gla(proj, lr, gla_w_up[l], gla_b_up[l].reshape(2, 1, -1), bsz, seq)
        fast, feat = _softmax_features(far, rel_bias, diff_qnorm_g[l], diff_knorm_g[l])
        o_b2 = _attn(proj, vt, tiles, far, fast, feat, diff_lambda[l], diff_subnorm_g[l].reshape(1, -1),
                     bsz, seq, lam_init)
        w_r = jnp.concatenate([w_router_group[l], w_router_expert[l],
                               jnp.zeros((d, LANES - N_GROUPS - N_EXPERTS), F32)], axis=1)
        b_r = jnp.concatenate([b_router_group[l], b_router_expert[l],
                               jnp.zeros((LANES - N_GROUPS - N_EXPERTS,), F32)]).reshape(1, LANES)
        x1, h2, info, counts = _mix(xf, o_f, o_b, proj, o_b2, mod, gla_norm_g[l].reshape(1, -1),
                                    norm2_g[l].reshape(1, d), w_branch_a[l].astype(BF16),
                                    w_branch_b[l].astype(BF16), w_out[l].astype(BF16), w_r, b_r, seq)
        xf = _moe(h2, x1, info, counts, mod, w_expert_in[l].astype(BF16),
                  w_expert_out[l].astype(BF16), seq)
    return xf.reshape(bsz, seq, d)
```

```python
import functools
import math

import numpy as np
import jax
import jax.numpy as jnp
from jax import lax
from jax.experimental import pallas as pl
from jax.experimental.pallas import tpu as pltpu

F32 = jnp.float32
BF16 = jnp.bfloat16
HIGHEST = lax.Precision.HIGHEST

D_MODEL = 1024
DEPTH = 2
GLA_HEADS, GLA_DK, GLA_DV, GLA_RANK, GLA_TAU, GLA_CHUNK = 4, 64, 128, 16, 16.0, 64
DIFF_HEADS, DIFF_DH, DIFF_DV = 4, 64, 128
REL_BUCKETS, REL_MAX_DIST = 32, 128
N_GROUPS, EXPERTS_PER_GROUP = 4, 8
N_EXPERTS = N_GROUPS * EXPERTS_PER_GROUP
D_EXPERT = D_MODEL // 2
RMS_EPS = 1e-6
LANES = 128
VMEM_LIMIT = 52 * 1024 * 1024
EXPERT_TILE = 256
DMA_ISSUE_UNROLL = 8

COL_GQK, COL_GV, COL_GA, COL_GB, COL_DKA, COL_GG, COL_DQ, COL_DV = 0, 4, 8, 16, 24, 32, 36, 40
N_PROJ = 44 * LANES
PROJ_TILE = 512
N_FEAT = 3
SOFTMAX_SAFE_RANGE = 60.0


def _cparams(sem, vmem=VMEM_LIMIT):
    return pltpu.CompilerParams(dimension_semantics=sem, vmem_limit_bytes=vmem)


def _ada_kernel(c_ref, w_ref, b_ref, o_ref):
    c = c_ref[...]
    ca = c * jax.nn.sigmoid(c)
    o_ref[0] = jnp.dot(ca, w_ref[0], precision=HIGHEST, preferred_element_type=F32) + b_ref[0]


def _ada(c, w_ada, b_ada):
    depth, d, n = w_ada.shape
    bsz = c.shape[0]
    tn = 1536
    return pl.pallas_call(
        _ada_kernel,
        grid=(depth, n // tn),
        in_specs=[pl.BlockSpec((bsz, d), lambda l, j: (0, 0)),
                  pl.BlockSpec((1, d, tn), lambda l, j: (l, 0, j)),
                  pl.BlockSpec((1, 1, tn), lambda l, j: (l, 0, j))],
        out_specs=pl.BlockSpec((1, bsz, tn), lambda l, j: (l, 0, j)),
        out_shape=jax.ShapeDtypeStruct((depth, bsz, n), F32),
        compiler_params=_cparams(("parallel", "parallel")),
        name="ada",
    )(c, w_ada, b_ada.reshape(depth, 1, n))


def _inproj_kernel(x_ref, mod_ref, g_ref, w_ref, wlr_ref, gmat_ref, qkg_ref, proj_ref, lr_ref, vt_ref):
    x = x_ref[...]
    ms = jnp.mean(x * x, axis=-1, keepdims=True)
    h = x * lax.rsqrt(ms + RMS_EPS) * g_ref[...]
    h = h * (1.0 + mod_ref[0, 1:2, :]) + mod_ref[0, 0:1, :]
    hb = h.astype(BF16)
    w = PROJ_TILE
    qk_tiles = {COL_DQ * LANES // w: 0, COL_DKA * LANES // w: 1, COL_DKA * LANES // w + 1: 2}
    for j in range(N_PROJ // w):
        acc = jnp.dot(hb, w_ref[:, j * w:(j + 1) * w], preferred_element_type=F32)
        if j in qk_tiles:
            ms2 = jnp.dot((acc * acc).astype(BF16), gmat_ref[...], preferred_element_type=F32)
            acc = acc * lax.rsqrt(ms2 + RMS_EPS) * qkg_ref[qk_tiles[j], 0:1, :] + qkg_ref[qk_tiles[j], 1:2, :]
        proj_ref[:, j * w:(j + 1) * w] = acc.astype(BF16)
        if j == COL_DV * LANES // w:
            vt_ref[0] = acc.T.astype(BF16)
    lr_ref[...] = jnp.dot(hb, wlr_ref[...], preferred_element_type=F32)


def _inproj(x, mod, g, w, wlr, gmat, qkg, seq):
    t, d = x.shape
    tm = min(256, seq)
    tpb = seq // tm
    return pl.pallas_call(
        _inproj_kernel,
        grid=(t // tm,),
        in_specs=[pl.BlockSpec((tm, d), lambda i: (i, 0)),
                  pl.BlockSpec((1, 6, d), lambda i: (i // tpb, 0, 0)),
                  pl.BlockSpec((1, d), lambda i: (0, 0)),
                  pl.BlockSpec((d, N_PROJ), lambda i: (0, 0)),
                  pl.BlockSpec((d, LANES), lambda i: (0, 0)),
                  pl.BlockSpec((PROJ_TILE, PROJ_TILE), lambda i: (0, 0)),
                  pl.BlockSpec((3, 2, PROJ_TILE), lambda i: (0, 0, 0))],
        out_specs=[pl.BlockSpec((tm, N_PROJ), lambda i: (i, 0)),
                   pl.BlockSpec((tm, LANES), lambda i: (i, 0)),
                   pl.BlockSpec((1, DIFF_HEADS * DIFF_DV, tm), lambda i: (i, 0, 0))],
        out_shape=[jax.ShapeDtypeStruct((t, N_PROJ), BF16),
                   jax.ShapeDtypeStruct((t, LANES), F32),
                   jax.ShapeDtypeStruct((t // tm, DIFF_HEADS * DIFF_DV, tm), BF16)],
        compiler_params=_cparams(("parallel",)),
        name="inproj",
    )(x, mod, g, w, wlr, gmat, qkg)


class _GlaDirection:
    def __init__(self, qk_ref, v_ref, lr, wup, bup, o_ref, s_ref, oi_ref, kv_ref, n_chunks, reverse):
        self.qk_ref, self.v_ref, self.lr, self.wup, self.bup = qk_ref, v_ref, lr, wup, bup
        self.o_ref, self.s_ref, self.oi_ref, self.kv_ref = o_ref, s_ref, oi_ref, kv_ref
        self.n_chunks, self.reverse = n_chunks, reverse
        c = GLA_CHUNK
        row = lax.broadcasted_iota(jnp.int32, (c, c), 0)
        col = lax.broadcasted_iota(jnp.int32, (c, c), 1)
        if reverse:
            self.smat, self.mask, self.tot_row = (col >= row).astype(BF16), col > row, 0
        else:
            self.smat, self.mask, self.tot_row = (col <= row).astype(BF16), col <= row, c - 1

    def _rows(self, n):
        return slice(n * GLA_CHUNK, (n + 1) * GLA_CHUNK)

    def _head(self, h, width):
        return slice(h * width, (h + 1) * width)

    def cumulative_decay(self):
        z = jnp.dot(self.lr, self.wup, precision=HIGHEST, preferred_element_type=F32) + self.bup
        lg = (jnp.minimum(z, 0.0) - jnp.log(1.0 + jnp.exp(-jnp.abs(z)))) * (1.0 / GLA_TAU)
        lg_hi = lg.astype(BF16)
        lg_lo = (lg - lg_hi.astype(F32)).astype(BF16)
        self.cums = [jnp.dot(self.smat, lg_hi[self._rows(n)], preferred_element_type=F32)
                     + jnp.dot(self.smat, lg_lo[self._rows(n)], preferred_element_type=F32)
                     for n in range(self.n_chunks)]

    def scale_operands(self):
        hd = GLA_HEADS * GLA_DK
        self.decs, self.q_dec, self.k_inv, self.k_end = [], [], [], []
        for n, cum in enumerate(self.cums):
            dec = jnp.exp(cum[self.tot_row:self.tot_row + 1, :])
            qk = self.qk_ref[self._rows(n), :].astype(F32)
            k_inv = qk[:, hd:] * jnp.exp(-cum)
            self.decs.append(dec)
            self.q_dec.append((qk[:, :hd] * jnp.exp(cum) * (GLA_DK ** -0.5)).astype(BF16))
            self.k_end.append((k_inv * dec).astype(BF16))
            self.k_inv.append(k_inv.astype(BF16))

    def scores_and_outer_products(self):
        self.scores = []
        for n in range(self.n_chunks):
            v = self.v_ref[self._rows(n), :]
            for h in range(GLA_HEADS):
                sl = self._head(h, GLA_DK)
                self.scores.append(lax.dot_general(self.q_dec[n][:, sl], self.k_inv[n][:, sl],
                                                   (((1,), (1,)), ((), ())), preferred_element_type=F32))
                self.kv_ref[n, h] = lax.dot_general(v[:, self._head(h, GLA_DV)], self.k_end[n][:, sl],
                                                    (((0,), (0,)), ((), ())), preferred_element_type=F32)

    def intra_chunk(self):
        for n in range(self.n_chunks):
            v = self.v_ref[self._rows(n), :]
            for h in range(GLA_HEADS):
                a = jnp.where(self.mask, self.scores[n * GLA_HEADS + h], 0.0).astype(BF16)
                self.oi_ref[self._rows(n), self._head(h, GLA_DV)] = jnp.dot(
                    a, v[:, self._head(h, GLA_DV)], preferred_element_type=F32)

    def inter_chunk(self):
        states = [self.s_ref[h] for h in range(GLA_HEADS)]
        order = range(self.n_chunks - 1, -1, -1) if self.reverse else range(self.n_chunks)
        for n in order:
            for h in range(GLA_HEADS):
                sl = self._head(h, GLA_DK)
                o = self.oi_ref[self._rows(n), self._head(h, GLA_DV)] + lax.dot_general(
                    self.q_dec[n][:, sl], states[h].astype(BF16), (((1,), (1,)), ((), ())),
                    preferred_element_type=F32)
                self.o_ref[self._rows(n), self._head(h, GLA_DV)] = o.astype(self.o_ref.dtype)
                states[h] = states[h] * self.decs[n][:, sl] + self.kv_ref[n, h]
        for h in range(GLA_HEADS):
            self.s_ref[h] = states[h]


def _gla_kernel(qkf_ref, vf_ref, lrf_ref, qkb_ref, vb_ref, lrb_ref, wup_ref, bup_ref,
                of_ref, ob_ref, sf_ref, sb_ref, oi_ref, kv_ref, *, n_chunks):
    @pl.when(pl.program_id(1) == 0)
    def _():
        sf_ref[...] = jnp.zeros_like(sf_ref)
        sb_ref[...] = jnp.zeros_like(sb_ref)

    fwd = _GlaDirection(qkf_ref, vf_ref, lrf_ref[:, 0:GLA_RANK], wup_ref[0], bup_ref[0], of_ref, sf_ref,
                        oi_ref.at[0], kv_ref.at[0], n_chunks, False)
    bwd = _GlaDirection(qkb_ref, vb_ref, lrb_ref[:, GLA_RANK:2 * GLA_RANK], wup_ref[1], bup_ref[1], ob_ref,
                        sb_ref, oi_ref.at[1], kv_ref.at[1], n_chunks, True)
    for phase in ("cumulative_decay", "scale_operands", "scores_and_outer_products", "intra_chunk",
                  "inter_chunk"):
        getattr(fwd, phase)()
        getattr(bwd, phase)()


def _gla(proj, lr, wup, bup, bsz, seq):
    t = proj.shape[0]
    lb = min(512, seq)
    nblk = seq // lb
    vw = GLA_HEADS * GLA_DV
    fwd = lambda b, i: b * nblk + i
    bwd = lambda b, i: b * nblk + nblk - 1 - i
    return pl.pallas_call(
        functools.partial(_gla_kernel, n_chunks=lb // GLA_CHUNK),
        grid=(bsz, nblk),
        in_specs=[pl.BlockSpec((lb, 512), lambda b, i: (fwd(b, i), COL_GQK // 4)),
                  pl.BlockSpec((lb, 512), lambda b, i: (fwd(b, i), COL_GV // 4)),
                  pl.BlockSpec((lb, LANES), lambda b, i: (fwd(b, i), 0)),
                  pl.BlockSpec((lb, 512), lambda b, i: (bwd(b, i), COL_GQK // 4)),
                  pl.BlockSpec((lb, 512), lambda b, i: (bwd(b, i), COL_GV // 4)),
                  pl.BlockSpec((lb, LANES), lambda b, i: (bwd(b, i), 0)),
                  pl.BlockSpec((2, GLA_RANK, GLA_HEADS * GLA_DK), lambda b, i: (0, 0, 0)),
                  pl.BlockSpec((2, 1, GLA_HEADS * GLA_DK), lambda b, i: (0, 0, 0))],
        out_specs=[pl.BlockSpec((lb, vw), lambda b, i: (fwd(b, i), 0)),
                   pl.BlockSpec((lb, vw), lambda b, i: (bwd(b, i), 0))],
        out_shape=[jax.ShapeDtypeStruct((t, vw), BF16), jax.ShapeDtypeStruct((t, vw), BF16)],
        scratch_shapes=[pltpu.VMEM((GLA_HEADS, GLA_DV, GLA_DK), F32),
                        pltpu.VMEM((GLA_HEADS, GLA_DV, GLA_DK), F32),
                        pltpu.VMEM((2, lb, vw), F32),
                        pltpu.VMEM((2, lb // GLA_CHUNK, GLA_HEADS, GLA_DV, GLA_DK), F32)],
        compiler_params=_cparams(("parallel", "arbitrary")),
        name="gla",
    )(proj, proj, lr, proj, proj, lr, wup, bup)


def _t5_bucket_np(rel):
    nb = REL_BUCKETS // 2
    max_exact = nb // 2
    ret = np.where(rel > 0, nb, 0)
    n = np.abs(rel)
    nf = np.maximum(n, 1).astype(np.float64)
    large = max_exact + (np.log(nf / max_exact) / math.log(REL_MAX_DIST / max_exact)
                         * (nb - max_exact)).astype(np.int64)
    large = np.minimum(large, nb - 1)
    return ret + np.where(n < max_exact, n, large)


def _attn_kernel(fast_ref, far_ref, q_ref, k_ref, vt_ref, tile_ref, feat_ref, lam_ref, sg_ref, o_ref,
                 m_ref, l_ref, acc_ref, *, tq, tk, nk, lam_init):
    qi = pl.program_id(2)
    head = pl.program_id(1)
    q = q_ref[...]
    n_sub = tk // vt_ref.shape[-1]
    ts = tk // n_sub
    tasks = [(u, c) for u in range(n_sub) for c in range(2)]
    near, lo, hi = 0, 1, 2

    def q_aug(variant):
        feat = jnp.broadcast_to(feat_ref[0, variant:variant + 1, :], (tq, DIFF_DH)).astype(BF16)
        return [jnp.concatenate([q[:, c * DIFF_DH:(c + 1) * DIFF_DH], feat], axis=1) for c in range(2)]

    def logits(kb, u, c, qa):
        r0 = pl.multiple_of(kb * tk + u * ts, ts)
        k = k_ref[pl.ds(r0, ts), c * LANES:(c + 1) * LANES]
        return lax.dot_general(k, qa[c], (((1,), (1,)), ((), ())), preferred_element_type=F32)

    def pipelined(blocks):
        work = [(kb, qa, bias, step, u, c) for kb, qa, bias, step in blocks for u, c in tasks]
        s_next = logits(work[0][0], work[0][4], work[0][5], work[0][1])
        for t, (kb, qa, bias, step, u, c) in enumerate(work):
            s = s_next
            if t + 1 < len(work):
                nxt = work[t + 1]
                s_next = logits(nxt[0], nxt[4], nxt[5], nxt[1])
            if bias is not None:
                s = s + bias[u * ts:(u + 1) * ts, :]
            step(s, c, vt_ref[kb * n_sub + u])

    def bounded_step(s, c, vt):
        p = jnp.exp2(s)
        l_ref[c] = l_ref[c] + jnp.sum(p, axis=0, keepdims=True)
        acc_ref[c] = acc_ref[c] + jnp.dot(vt, p.astype(BF16), preferred_element_type=F32)

    def online_step(cst):
        def step(s, c, vt):
            m_old = m_ref[c]
            m_new = jnp.maximum(m_old, jnp.max(s, axis=0, keepdims=True) + cst)
            p = jnp.exp2(s - (m_new - cst))
            alpha = jnp.exp2(m_old - m_new)
            l_ref[c] = alpha * l_ref[c] + jnp.sum(p, axis=0, keepdims=True)
            acc_ref[c] = alpha * acc_ref[c] + jnp.dot(vt, p.astype(BF16), preferred_element_type=F32)
            m_ref[c] = m_new
        return step

    def sweep(far_lo_step, far_hi_step, near_step, group):
        q_near, q_lo, q_hi = q_aug(near), q_aug(lo), q_aug(hi)

        def far(start, count, qa, step):
            def body(i, carry):
                pipelined([(start + group * i + g, qa, None, step) for g in range(group)])
                return carry

            lax.fori_loop(0, count // group, body, 0)
            for rem in range(1, group):
                @pl.when(count % group == rem)
                def _(rem=rem):
                    pipelined([(start + count - rem + g, qa, None, step) for g in range(rem)])

        far(0, jnp.maximum(qi - 1, 0), q_lo, far_lo_step)
        for has_left in (False, True):
            for has_right in (False, True):
                dlts = ([-1] if has_left else []) + [0] + ([1] if has_right else [])

                left_ok = (qi > 0) if has_left else (qi == 0)
                right_ok = (qi < nk - 1) if has_right else (qi >= nk - 1)

                @pl.when(jnp.logical_and(left_ok, right_ok))
                def _(dlts=dlts):
                    pipelined([(qi + dlt, q_near, tile_ref.at[0, dlt + 1], near_step) for dlt in dlts])
        hi_start = jnp.minimum(qi + 2, nk)
        far(hi_start, nk - hi_start, q_hi, far_hi_step)

    m_ref[...] = jnp.full_like(m_ref, -1e30)
    l_ref[...] = jnp.zeros_like(l_ref)
    acc_ref[...] = jnp.zeros_like(acc_ref)

    @pl.when(fast_ref[0] == 1)
    def _():
        sweep(bounded_step, bounded_step, bounded_step, 2)

    @pl.when(fast_ref[0] == 0)
    def _():
        sweep(online_step(far_ref[head, 0]), online_step(far_ref[head, 1]), online_step(0.0), 1)

    lv = lam_ref[...]
    lam = (jnp.exp(jnp.sum(lv[0:1] * lv[1:2], axis=-1, keepdims=True))
           - jnp.exp(jnp.sum(lv[2:3] * lv[3:4], axis=-1, keepdims=True)) + lam_init)
    out_t = acc_ref[0] * (1.0 / l_ref[0]) - acc_ref[1] * (lam / l_ref[1])
    out = out_t.T
    ms = jnp.mean(out * out, axis=-1, keepdims=True)
    out = out * lax.rsqrt(ms + RMS_EPS) * sg_ref[...] * (1.0 - lam_init)
    o_ref[...] = out.astype(o_ref.dtype)


def _attn_tile(seq):
    return min(512, seq)


def _attn(proj, vt, tiles, far, fast, feat, lam_vecs, subn_g, bsz, seq, lam_init):
    t = proj.shape[0]
    tq = tk = _attn_tile(seq)
    nq = seq // tq
    nk = seq // tk
    vb = vt.shape[-1]
    return pl.pallas_call(
        functools.partial(_attn_kernel, tq=tq, tk=tk, nk=nk, lam_init=lam_init),
        grid=(bsz, DIFF_HEADS, nq),
        in_specs=[pl.BlockSpec(memory_space=pltpu.SMEM),
                  pl.BlockSpec(memory_space=pltpu.SMEM),
                  pl.BlockSpec((tq, LANES), lambda b, h, i: (b * nq + i, COL_DQ + h)),
                  pl.BlockSpec((seq, 2 * LANES), lambda b, h, i: (b, COL_DKA // 2 + h)),
                  pl.BlockSpec((seq // vb, DIFF_DV, vb), lambda b, h, i: (b, h, 0)),
                  pl.BlockSpec((1, 3, tk, tq), lambda b, h, i: (h, 0, 0, 0)),
                  pl.BlockSpec((1, 3, DIFF_DH), lambda b, h, i: (h, 0, 0)),
                  pl.BlockSpec((4, DIFF_DH), lambda b, h, i: (0, 0)),
                  pl.BlockSpec((1, DIFF_DV), lambda b, h, i: (0, 0))],
        out_specs=pl.BlockSpec((tq, DIFF_DV), lambda b, h, i: (b * nq + i, h)),
        out_shape=jax.ShapeDtypeStruct((t, DIFF_HEADS * DIFF_DV), BF16),
        scratch_shapes=[pltpu.VMEM((2, 1, tq), F32), pltpu.VMEM((2, 1, tq), F32),
                        pltpu.VMEM((2, DIFF_DV, tq), F32)],
        compiler_params=_cparams(("parallel", "parallel", "arbitrary")),
        name="attn",
    )(fast, far, proj, proj, vt, tiles, feat, lam_vecs, subn_g)


def _bias_tables(rel_bias, tq, tk):
    n = tk + tq - 1
    log2e = math.log2(math.e)
    tiles = []
    for d in (-1, 0, 1):
        onehot = np.eye(REL_BUCKETS, dtype=np.float32)[_t5_bucket_np(d * tk + np.arange(n) - (tq - 1))]
        g = jnp.dot(jnp.asarray(onehot), rel_bias.astype(F32), precision=HIGHEST).T * log2e
        hank = jnp.tile(g, (1, tk + 1))[:, :tk * (n + 1)].reshape(-1, tk, n + 1)
        tiles.append(hank[:, :, :tq][:, :, ::-1])
    tiles = jnp.stack(tiles, axis=1)
    nb = REL_BUCKETS // 2
    far = jnp.stack([rel_bias[nb - 1], rel_bias[2 * nb - 1]], axis=-1).astype(F32) * log2e
    return tiles, far


def _softmax_features(far, rel_bias, qn_g, kn_g):
    log2e = math.log2(math.e)
    bound = (1.02 * DIFF_DH ** 0.5 * log2e * jnp.max(jnp.abs(qn_g)) * jnp.max(jnp.abs(kn_g))
             + log2e * jnp.max(jnp.abs(rel_bias)))
    fast = bound <= SOFTMAX_SAFE_RANGE
    neg_bound = -bound.astype(BF16).astype(F32)
    c_hi = far.astype(BF16).astype(F32)
    c_lo = (far - c_hi).astype(BF16).astype(F32)
    zeros = jnp.zeros_like(c_hi[:, 0])
    nbv = jnp.broadcast_to(neg_bound, zeros.shape)
    rows = jnp.stack([jnp.stack([nbv, zeros, zeros], -1),
                      jnp.stack([nbv, c_hi[:, 0], c_lo[:, 0]], -1),
                      jnp.stack([nbv, c_hi[:, 1], c_lo[:, 1]], -1)], axis=1)
    rows = jnp.where(fast, rows, 0.0)
    feat = jnp.pad(rows, ((0, 0), (0, 0), (0, DIFF_DH - N_FEAT)))
    return fast.astype(jnp.int32).reshape(1), feat


def _mix_kernel(x_ref, of_ref, ob_ref, gg_ref, ob2_ref, ga_ref, gb_ref, mod_ref, gn_ref, n2_ref,
                wa_ref, wb_ref, wo_ref, wr_ref, br_ref,
                x1_ref, h2_ref, info_ref, cnt_ref, base_ref, *, tm):
    @pl.when(pl.program_id(0) == 0)
    def _():
        base_ref[...] = jnp.zeros_like(base_ref)

    osum = of_ref[...].astype(F32) + ob_ref[...].astype(F32)
    parts = []
    for h in range(GLA_HEADS):
        sl = osum[:, h * GLA_DV:(h + 1) * GLA_DV]
        ms = jnp.mean(sl * sl, axis=-1, keepdims=True)
        parts.append(sl * lax.rsqrt(ms + RMS_EPS) * gn_ref[...])
    gg = gg_ref[...].astype(F32)
    o_a = jnp.concatenate(parts, axis=-1) * (gg * jax.nn.sigmoid(gg))
    y_a = jnp.dot(o_a.astype(BF16), wa_ref[...], preferred_element_type=F32)
    y_b = jnp.dot(ob2_ref[...], wb_ref[...], preferred_element_type=F32)
    merged = (jax.nn.sigmoid(ga_ref[...].astype(F32)) * y_a
              + jax.nn.sigmoid(gb_ref[...].astype(F32)) * y_b)
    y = jnp.dot(merged.astype(BF16), wo_ref[...], preferred_element_type=F32)
    x1 = x_ref[...] + mod_ref[0, 2:3, :] * y
    x1_ref[...] = x1

    ms = jnp.mean(x1 * x1, axis=-1, keepdims=True)
    h2 = x1 * lax.rsqrt(ms + RMS_EPS) * n2_ref[...]
    h2 = h2 * (1.0 + mod_ref[0, 4:5, :]) + mod_ref[0, 3:4, :]
    h2_ref[...] = h2

    logits = jnp.dot(h2, wr_ref[...], precision=HIGHEST, preferred_element_type=F32) + br_ref[...]
    lane_i = lax.broadcasted_iota(jnp.int32, logits.shape, 1)
    lane = lane_i.astype(F32)
    neg = jnp.float32(-3e38)
    big = jnp.float32(1 << 20)
    is_g = lane < N_GROUPS
    gl = jnp.where(is_g, logits, neg)
    gmax = jnp.max(gl, axis=-1, keepdims=True)
    grp = jnp.min(jnp.where(jnp.logical_and(is_g, gl == gmax), lane, big), axis=-1, keepdims=True)
    p_grp = 1.0 / jnp.sum(jnp.where(is_g, jnp.exp(gl - gmax), 0.0), axis=-1, keepdims=True)
    lo = N_GROUPS + grp * EXPERTS_PER_GROUP
    in_grp = jnp.logical_and(lane >= lo, lane < lo + EXPERTS_PER_GROUP)
    el = jnp.where(in_grp, logits, neg)
    e1 = jnp.max(el, axis=-1, keepdims=True)
    i1 = jnp.min(jnp.where(jnp.logical_and(in_grp, el == e1), lane, big), axis=-1, keepdims=True)
    rest = jnp.logical_and(in_grp, lane != i1)
    el2 = jnp.where(rest, logits, neg)
    e2 = jnp.max(el2, axis=-1, keepdims=True)
    i2 = jnp.min(jnp.where(jnp.logical_and(rest, el2 == e2), lane, big), axis=-1, keepdims=True)
    r = jnp.exp(e2 - e1)
    w1 = p_grp / (1.0 + r)
    w2 = w1 * r

    oh1 = (lane == i1).astype(F32)
    oh2 = (lane == i2).astype(F32)
    rr = lax.broadcasted_iota(jnp.int32, (tm, tm), 0)
    cc = lax.broadcasted_iota(jnp.int32, (tm, tm), 1)
    tril = (cc < rr).astype(BF16)
    before = jnp.dot(tril, (oh1 + oh2).astype(BF16), preferred_element_type=F32) + base_ref[...]
    rank1 = jnp.sum(oh1 * before, axis=-1, keepdims=True)
    rank2 = jnp.sum(oh2 * before, axis=-1, keepdims=True)
    base_new = base_ref[...] + jnp.sum(oh1 + oh2, axis=0, keepdims=True)
    base_ref[...] = base_new
    cnt_ref[...] = jnp.broadcast_to(base_new, cnt_ref.shape)

    ex1 = i1 - N_GROUPS
    ex2 = i2 - N_GROUPS
    info = jnp.zeros(logits.shape, F32)
    for idx, val in enumerate((ex1, ex2, rank1, rank2, w1, w2)):
        info = jnp.where(lane_i == idx, val, info)
    info_ref[...] = info


def _mix(x, of, ob, proj, ob2, mod, gn, n2, wa, wb, wo, wr, br, seq):
    t, d = x.shape
    tm = min(256, seq)
    tpb = seq // tm
    row = lambda i: (i, 0)
    const = lambda i: (0, 0)
    return pl.pallas_call(
        functools.partial(_mix_kernel, tm=tm),
        grid=(t // tm,),
        in_specs=[pl.BlockSpec((tm, d), row),
                  pl.BlockSpec((tm, 512), row),
                  pl.BlockSpec((tm, 512), row),
                  pl.BlockSpec((tm, 512), lambda i: (i, COL_GG // 4)),
                  pl.BlockSpec((tm, 512), row),
                  pl.BlockSpec((tm, 1024), lambda i: (i, COL_GA // 8)),
                  pl.BlockSpec((tm, 1024), lambda i: (i, COL_GB // 8)),
                  pl.BlockSpec((1, 6, d), lambda i: (i // tpb, 0, 0)),
                  pl.BlockSpec((1, GLA_DV), const),
                  pl.BlockSpec((1, d), const),
                  pl.BlockSpec((512, d), const),
                  pl.BlockSpec((512, d), const),
                  pl.BlockSpec((d, d), const),
                  pl.BlockSpec((d, LANES), const),
                  pl.BlockSpec((1, LANES), const)],
        out_specs=[pl.BlockSpec((tm, d), row),
                   pl.BlockSpec((tm, d), row),
                   pl.BlockSpec((tm, LANES), row),
                   pl.BlockSpec((8, LANES), const)],
        out_shape=[jax.ShapeDtypeStruct((t, d), F32),
                   jax.ShapeDtypeStruct((t, d), F32),
                   jax.ShapeDtypeStruct((t, LANES), F32),
                   jax.ShapeDtypeStruct((8, LANES), F32)],
        scratch_shapes=[pltpu.VMEM((1, LANES), F32)],
        compiler_params=_cparams(("arbitrary",)),
        name="mix",
    )(x, of, ob, proj, ob2, proj, proj, mod, gn, n2, wa, wb, wo, wr, br)


def _dispatch_kernel(pos_ref, h_ref, zeros_ref, xs_ref, sem_ref, *, tb):
    del zeros_ref

    def row_copy(r, k):
        return pltpu.make_async_copy(h_ref.at[pl.ds(r, 1), :],
                                     xs_ref.at[pl.ds(pos_ref[0, 0, 2 * r + k], 1), :], sem_ref)

    def start(r, carry):
        row_copy(r, 0).start(priority=0)
        row_copy(r, 1).start(priority=1)
        return carry

    def wait(r, carry):
        row_copy(r, 0).wait()
        row_copy(r, 1).wait()
        return carry

    lax.fori_loop(0, tb, start, 0, unroll=DMA_ISSUE_UNROLL)
    lax.fori_loop(0, tb, wait, 0, unroll=True)


def _dispatch(h2, pos3, n_rows, tb):
    t, d = h2.shape
    return pl.pallas_call(
        functools.partial(_dispatch_kernel, tb=tb),
        grid_spec=pltpu.PrefetchScalarGridSpec(
            num_scalar_prefetch=0,
            grid=(t // tb,),
            in_specs=[pl.BlockSpec((1, 1, 2 * tb), lambda i: (i, 0, 0), memory_space=pltpu.SMEM),
                      pl.BlockSpec((tb, d), lambda i: (i, 0)),
                      pl.BlockSpec(memory_space=pl.ANY)],
            out_specs=pl.BlockSpec(memory_space=pl.ANY),
            scratch_shapes=[pltpu.SemaphoreType.DMA(())]),
        out_shape=jax.ShapeDtypeStruct((n_rows, d), F32),
        input_output_aliases={2: 0},
        compiler_params=_cparams(("arbitrary",)),
        name="dispatch",
    )(pos3, h2, jnp.zeros((n_rows, d), F32))


def _expert_kernel(te_ref, nv_ref, xs_ref, w1_ref, w2_ref, ys_ref):
    @pl.when(pl.program_id(0) >= nv_ref[0])
    def _():
        ys_ref[...] = jnp.zeros_like(ys_ref)

    @pl.when(pl.program_id(0) < nv_ref[0])
    def _():
        xb = xs_ref[...].astype(BF16)
        hu = jnp.dot(xb, w1_ref[0], preferred_element_type=F32)
        hg = hu[:, :D_EXPERT]
        act = (hg * jax.nn.sigmoid(hg)) * hu[:, D_EXPERT:]
        ys_ref[...] = jnp.dot(act.astype(BF16), w2_ref[0], preferred_element_type=F32)


def _experts(xs, tile_expert, n_valid, w1, w2, n_tiles):
    d = xs.shape[1]
    blk = lambda i, te, nv: (jnp.maximum(jnp.minimum(i, nv[0] - 1), 0), 0)
    return pl.pallas_call(
        _expert_kernel,
        grid_spec=pltpu.PrefetchScalarGridSpec(
            num_scalar_prefetch=2,
            grid=(n_tiles,),
            in_specs=[pl.BlockSpec((EXPERT_TILE, d), blk),
                      pl.BlockSpec((1, d, 2 * D_EXPERT), lambda i, te, nv: (te[i], 0, 0)),
                      pl.BlockSpec((1, D_EXPERT, d), lambda i, te, nv: (te[i], 0, 0))],
            out_specs=pl.BlockSpec((EXPERT_TILE, d), lambda i, te, nv: (i, 0))),
        out_shape=jax.ShapeDtypeStruct((n_tiles * EXPERT_TILE, d), F32),
        compiler_params=_cparams(("arbitrary",)),
        name="experts",
    )(tile_expert, n_valid, xs, w1, w2)


def _combine_kernel(pos_ref, ys_ref, x1_ref, info_ref, mod_ref, o_ref, buf_ref, sem_ref, *, tb):
    def row_copy(r, k):
        return pltpu.make_async_copy(ys_ref.at[pl.ds(pos_ref[0, 0, 2 * r + k], 1), :],
                                     buf_ref.at[k, pl.ds(r, 1), :], sem_ref)

    def start(r, carry):
        row_copy(r, 0).start(priority=0)
        row_copy(r, 1).start(priority=1)
        return carry

    def wait(r, carry):
        row_copy(r, 0).wait()
        row_copy(r, 1).wait()
        return carry

    lax.fori_loop(0, tb, start, 0, unroll=DMA_ISSUE_UNROLL)
    lax.fori_loop(0, tb, wait, 0, unroll=True)
    info = info_ref[...]
    y = buf_ref[0] * info[:, 4:5] + buf_ref[1] * info[:, 5:6]
    o_ref[...] = x1_ref[...] + mod_ref[0, 5:6, :] * y


def _combine(ys, pos3, x1, info, mod, seq, tb):
    t, d = x1.shape
    tpb = seq // tb
    return pl.pallas_call(
        functools.partial(_combine_kernel, tb=tb),
        grid_spec=pltpu.PrefetchScalarGridSpec(
            num_scalar_prefetch=0,
            grid=(t // tb,),
            in_specs=[pl.BlockSpec((1, 1, 2 * tb), lambda i: (i, 0, 0), memory_space=pltpu.SMEM),
                      pl.BlockSpec(memory_space=pl.ANY),
                      pl.BlockSpec((tb, d), lambda i: (i, 0)),
                      pl.BlockSpec((tb, LANES), lambda i: (i, 0)),
                      pl.BlockSpec((1, 6, d), lambda i: (i // tpb, 0, 0))],
            out_specs=pl.BlockSpec((tb, d), lambda i: (i, 0)),
            scratch_shapes=[pltpu.VMEM((2, tb, d), F32), pltpu.SemaphoreType.DMA(())]),
        out_shape=jax.ShapeDtypeStruct((t, d), F32),
        compiler_params=_cparams(("arbitrary",)),
        name="combine",
    )(pos3, ys, x1, info, mod)


def _moe(h2, x1, info, counts, mod, w1, w2, seq):
    t, d = h2.shape
    tb = min(256, seq)
    n_tiles = (2 * t + N_EXPERTS * (EXPERT_TILE - 1)) // EXPERT_TILE
    cnt = counts[0, N_GROUPS:N_GROUPS + N_EXPERTS].astype(jnp.int32)
    padded = ((cnt + EXPERT_TILE - 1) // EXPERT_TILE) * EXPERT_TILE
    ends = jnp.cumsum(padded)
    starts = ends - padded
    n_valid = (ends[-1] // EXPERT_TILE).astype(jnp.int32).reshape(1)
    tile_start = jnp.arange(n_tiles, dtype=jnp.int32) * EXPERT_TILE
    tile_expert = jnp.sum(ends[None, :] <= tile_start[:, None], axis=1).astype(jnp.int32)
    tile_expert = jnp.minimum(tile_expert, tile_expert[jnp.maximum(n_valid[0] - 1, 0)])
    eid = info[:, 0:2].astype(jnp.int32)
    pos = starts[eid] + info[:, 2:4].astype(jnp.int32)
    pos3 = pos.reshape(t // tb, 1, 2 * tb)
    xs = _dispatch(h2, pos3, n_tiles * EXPERT_TILE, tb)
    ys = _experts(xs, tile_expert, n_valid, w1, w2, n_tiles)
    return _combine(ys, pos3, x1, info, mod, seq, tb)


def _qk_norm_rows(qn_g, kn_g):
    zeros = jnp.zeros((DIFF_DH,), F32)
    feat = jnp.asarray(np.arange(DIFF_DH) < N_FEAT, F32)
    q_gain = jnp.tile(qn_g, 2 * DIFF_HEADS) * (DIFF_DH ** -0.5 * math.log2(math.e))
    k_gain = jnp.tile(jnp.concatenate([kn_g, zeros]), DIFF_HEADS)
    k_add = jnp.tile(jnp.concatenate([zeros, feat]), DIFF_HEADS)
    return jnp.stack([jnp.stack([q_gain, jnp.zeros_like(q_gain)]),
                      jnp.stack([k_gain, k_add]), jnp.stack([k_gain, k_add])])


def _prep_w_in(w_in_l):
    w = w_in_l
    c = np.cumsum([0, 256, 256, 512, 512, 16, 16, 512, 512, 512, 1024, 1024])
    seg = lambda i: w[:, c[i]:c[i + 1]]
    gq, gk, gv, gg, lrf, lrb, dq, dk, dv, ga, gb = (seg(i) for i in range(11))
    dk_aug = jnp.pad(dk.reshape(-1, 2 * DIFF_HEADS, 1, DIFF_DH), ((0, 0), (0, 0), (0, 1), (0, 0)))
    dk_aug = dk_aug.reshape(-1, 4 * DIFF_HEADS * DIFF_DH)
    main = jnp.concatenate([gq, gk, gv, ga, gb, dk_aug, gg, dq, dv], axis=1).astype(BF16)
    lr = jnp.concatenate([lrf, lrb, jnp.zeros((w.shape[0], LANES - 2 * GLA_RANK), w.dtype)], axis=1)
    return main, lr.astype(BF16)


def kernel(x, c, w_ada, b_ada, norm1_g, norm2_g, w_in, gla_w_up, gla_b_up, gla_norm_g, diff_qnorm_g,
           diff_knorm_g, diff_lambda, diff_subnorm_g, rel_bias, w_branch_a, w_branch_b, w_out,
           w_router_group, b_router_group, w_router_expert, b_router_expert, w_expert_in, w_expert_out):
    bsz, seq, d = x.shape
    t = bsz * seq
    depth = w_ada.shape[0]
    mod_all = _ada(c, w_ada, b_ada).reshape(depth, bsz, 6, d)

    tq = _attn_tile(seq)
    tiles, far = _bias_tables(rel_bias, tq, tq)
    grp = np.arange(PROJ_TILE) // DIFF_DH
    gmat = jnp.asarray((grp[:, None] == grp[None, :]).astype(np.float32) / DIFF_DH, dtype=BF16)

    xf = x.reshape(t, d)
    for l in range(depth):
        lam_init = 0.8 - 0.6 * math.exp(-0.3 * l)
        mod = mod_all[l]
        w_main, w_lr = _prep_w_in(w_in[l])
        qkg = _qk_norm_rows(diff_qnorm_g[l], diff_knorm_g[l])
        proj, lr, vt = _inproj(xf, mod, norm1_g[l].reshape(1, d), w_main, w_lr, gmat, qkg, seq)
        o_f, o_b = _gla(proj, lr, gla_w_up[l], gla_b_up[l].reshape(2, 1, -1), bsz, seq)
        fast, feat = _softmax_features(far, rel_bias, diff_qnorm_g[l], diff_knorm_g[l])
        o_b2 = _attn(proj, vt, tiles, far, fast, feat, diff_lambda[l], diff_subnorm_g[l].reshape(1, -1),
                     bsz, seq, lam_init)
        w_r = jnp.concatenate([w_router_group[l], w_router_expert[l],
                               jnp.zeros((d, LANES - N_GROUPS - N_EXPERTS), F32)], axis=1)
        b_r = jnp.concatenate([b_router_group[l], b_router_expert[l],
                               jnp.zeros((LANES - N_GROUPS - N_EXPERTS,), F32)]).reshape(1, LANES)
        x1, h2, info, counts = _mix(xf, o_f, o_b, proj, o_b2, mod, gla_norm_g[l].reshape(1, -1),
                                    norm2_g[l].reshape(1, d), w_branch_a[l].astype(BF16),
                                    w_branch_b[l].astype(BF16), w_out[l].astype(BF16), w_r, b_r, seq)
        xf = _moe(h2, x1, info, counts, mod, w_expert_in[l].astype(BF16),
                  w_expert_out[l].astype(BF16), seq)
    return xf.reshape(bsz, seq, d)
```

```python
import functools
import math

import numpy as np
import jax
import jax.numpy as jnp
from jax import lax
from jax.experimental import pallas as pl
from jax.experimental.pallas import tpu as pltpu

F32 = jnp.float32
BF16 = jnp.bfloat16
HIGHEST = lax.Precision.HIGHEST

D_MODEL = 1024
DEPTH = 2
GLA_HEADS, GLA_DK, GLA_DV, GLA_RANK, GLA_TAU, GLA_CHUNK = 4, 64, 128, 16, 16.0, 64
DIFF_HEADS, DIFF_DH, DIFF_DV = 4, 64, 128
REL_BUCKETS, REL_MAX_DIST = 32, 128
N_GROUPS, EXPERTS_PER_GROUP = 4, 8
N_EXPERTS = N_GROUPS * EXPERTS_PER_GROUP
D_EXPERT = D_MODEL // 2
RMS_EPS = 1e-6
LANES = 128
VMEM_LIMIT = 52 * 1024 * 1024
EXPERT_TILE = 256
DMA_ISSUE_UNROLL = 8

COL_GQK, COL_GV, COL_GA, COL_GB, COL_DKA, COL_GG, COL_DQ, COL_DV = 0, 4, 8, 16, 24, 32, 36, 40
N_PROJ = 44 * LANES
PROJ_TILE = 512
N_FEAT = 3
SOFTMAX_SAFE_RANGE = 60.0


def _cparams(sem, vmem=VMEM_LIMIT):
    return pltpu.CompilerParams(dimension_semantics=sem, vmem_limit_bytes=vmem)


def _ada_kernel(c_ref, w_ref, b_ref, o_ref):
    c = c_ref[...]
    ca = c * jax.nn.sigmoid(c)
    o_ref[0] = jnp.dot(ca, w_ref[0], precision=HIGHEST, preferred_element_type=F32) + b_ref[0]


def _ada(c, w_ada, b_ada):
    depth, d, n = w_ada.shape
    bsz = c.shape[0]
    tn = 1536
    return pl.pallas_call(
        _ada_kernel,
        grid=(depth, n // tn),
        in_specs=[pl.BlockSpec((bsz, d), lambda l, j: (0, 0)),
                  pl.BlockSpec((1, d, tn), lambda l, j: (l, 0, j)),
                  pl.BlockSpec((1, 1, tn), lambda l, j: (l, 0, j))],
        out_specs=pl.BlockSpec((1, bsz, tn), lambda l, j: (l, 0, j)),
        out_shape=jax.ShapeDtypeStruct((depth, bsz, n), F32),
        compiler_params=_cparams(("parallel", "parallel")),
        name="ada",
    )(c, w_ada, b_ada.reshape(depth, 1, n))


def _inproj_kernel(x_ref, mod_ref, g_ref, w_ref, wlr_ref, gmat_ref, qkg_ref, proj_ref, lr_ref, vt_ref):
    x = x_ref[...]
    ms = jnp.mean(x * x, axis=-1, keepdims=True)
    h = x * lax.rsqrt(ms + RMS_EPS) * g_ref[...]
    h = h * (1.0 + mod_ref[0, 1:2, :]) + mod_ref[0, 0:1, :]
    hb = h.astype(BF16)
    w = PROJ_TILE
    qk_tiles = {COL_DQ * LANES // w: 0, COL_DKA * LANES // w: 1, COL_DKA * LANES // w + 1: 2}
    for j in range(N_PROJ // w):
        acc = jnp.dot(hb, w_ref[:, j * w:(j + 1) * w], preferred_element_type=F32)
        if j in qk_tiles:
            ms2 = jnp.dot((acc * acc).astype(BF16), gmat_ref[...], preferred_element_type=F32)
            acc = acc * lax.rsqrt(ms2 + RMS_EPS) * qkg_ref[qk_tiles[j], 0:1, :] + qkg_ref[qk_tiles[j], 1:2, :]
        proj_ref[:, j * w:(j + 1) * w] = acc.astype(BF16)
        if j == COL_DV * LANES // w:
            vt_ref[0] = acc.T.astype(BF16)
    lr_ref[...] = jnp.dot(hb, wlr_ref[...], preferred_element_type=F32)


def _inproj(x, mod, g, w, wlr, gmat, qkg, seq):
    t, d = x.shape
    tm = min(256, seq)
    tpb = seq // tm
    return pl.pallas_call(
        _inproj_kernel,
        grid=(t // tm,),
        in_specs=[pl.BlockSpec((tm, d), lambda i: (i, 0)),
                  pl.BlockSpec((1, 6, d), lambda i: (i // tpb, 0, 0)),
                  pl.BlockSpec((1, d), lambda i: (0, 0)),
                  pl.BlockSpec((d, N_PROJ), lambda i: (0, 0)),
                  pl.BlockSpec((d, LANES), lambda i: (0, 0)),
                  pl.BlockSpec((PROJ_TILE, PROJ_TILE), lambda i: (0, 0)),
                  pl.BlockSpec((3, 2, PROJ_TILE), lambda i: (0, 0, 0))],
        out_specs=[pl.BlockSpec((tm, N_PROJ), lambda i: (i, 0)),
                   pl.BlockSpec((tm, LANES), lambda i: (i, 0)),
                   pl.BlockSpec((1, DIFF_HEADS * DIFF_DV, tm), lambda i: (i, 0, 0))],
        out_shape=[jax.ShapeDtypeStruct((t, N_PROJ), BF16),
                   jax.ShapeDtypeStruct((t, LANES), F32),
                   jax.ShapeDtypeStruct((t // tm, DIFF_HEADS * DIFF_DV, tm), BF16)],
        compiler_params=_cparams(("parallel",)),
        name="inproj",
    )(x, mod, g, w, wlr, gmat, qkg)


class _GlaDirection:
    def __init__(self, qk_ref, v_ref, lr, wup, bup, o_ref, s_ref, oi_ref, kv_ref, n_chunks, reverse):
        self.qk_ref, self.v_ref, self.lr, self.wup, self.bup = qk_ref, v_ref, lr, wup, bup
        self.o_ref, self.s_ref, self.oi_ref, self.kv_ref = o_ref, s_ref, oi_ref, kv_ref
        self.n_chunks, self.reverse = n_chunks, reverse
        c = GLA_CHUNK
        row = lax.broadcasted_iota(jnp.int32, (c, c), 0)
        col = lax.broadcasted_iota(jnp.int32, (c, c), 1)
        if reverse:
            self.smat, self.mask, self.tot_row = (col >= row).astype(BF16), col > row, 0
        else:
            self.smat, self.mask, self.tot_row = (col <= row).astype(BF16), col <= row, c - 1

    def _rows(self, n):
        return slice(n * GLA_CHUNK, (n + 1) * GLA_CHUNK)

    def _head(self, h, width):
        return slice(h * width, (h + 1) * width)

    def cumulative_decay(self):
        z = jnp.dot(self.lr, self.wup, precision=HIGHEST, preferred_element_type=F32) + self.bup
        lg = (jnp.minimum(z, 0.0) - jnp.log(1.0 + jnp.exp(-jnp.abs(z)))) * (1.0 / GLA_TAU)
        lg_hi = lg.astype(BF16)
        lg_lo = (lg - lg_hi.astype(F32)).astype(BF16)
        self.cums = [jnp.dot(self.smat, lg_hi[self._rows(n)], preferred_element_type=F32)
                     + jnp.dot(self.smat, lg_lo[self._rows(n)], preferred_element_type=F32)
                     for n in range(self.n_chunks)]

    def scale_operands(self):
        hd = GLA_HEADS * GLA_DK
        self.decs, self.q_dec, self.k_inv, self.k_end = [], [], [], []
        for n, cum in enumerate(self.cums):
            dec = jnp.exp(cum[self.tot_row:self.tot_row + 1, :])
            qk = self.qk_ref[self._rows(n), :].astype(F32)
            k_inv = qk[:, hd:] * jnp.exp(-cum)
            self.decs.append(dec)
            self.q_dec.append((qk[:, :hd] * jnp.exp(cum) * (GLA_DK ** -0.5)).astype(BF16))
            self.k_end.append((k_inv * dec).astype(BF16))
            self.k_inv.append(k_inv.astype(BF16))

    def scores_and_outer_products(self):
        self.scores = []
        for n in range(self.n_chunks):
            v = self.v_ref[self._rows(n), :]
            for h in range(GLA_HEADS):
                sl = self._head(h, GLA_DK)
                self.scores.append(lax.dot_general(self.q_dec[n][:, sl], self.k_inv[n][:, sl],
                                                   (((1,), (1,)), ((), ())), preferred_element_type=F32))
                self.kv_ref[n, h] = lax.dot_general(v[:, self._head(h, GLA_DV)], self.k_end[n][:, sl],
                                                    (((0,), (0,)), ((), ())), preferred_element_type=F32)

    def intra_chunk(self):
        for n in range(self.n_chunks):
            v = self.v_ref[self._rows(n), :]
            for h in range(GLA_HEADS):
                a = jnp.where(self.mask, self.scores[n * GLA_HEADS + h], 0.0).astype(BF16)
                self.oi_ref[self._rows(n), self._head(h, GLA_DV)] = jnp.dot(
                    a, v[:, self._head(h, GLA_DV)], preferred_element_type=F32)

    def inter_chunk(self):
        states = [self.s_ref[h] for h in range(GLA_HEADS)]
        order = range(self.n_chunks - 1, -1, -1) if self.reverse else range(self.n_chunks)
        for n in order:
            for h in range(GLA_HEADS):
                sl = self._head(h, GLA_DK)
                o = self.oi_ref[self._rows(n), self._head(h, GLA_DV)] + lax.dot_general(
                    self.q_dec[n][:, sl], states[h].astype(BF16), (((1,), (1,)), ((), ())),
                    preferred_element_type=F32)
                self.o_ref[self._rows(n), self._head(h, GLA_DV)] = o.astype(self.o_ref.dtype)
                states[h] = states[h] * self.decs[n][:, sl] + self.kv_ref[n, h]
        for h in range(GLA_HEADS):
            self.s_ref[h] = states[h]


def _gla_kernel(qkf_ref, vf_ref, lrf_ref, qkb_ref, vb_ref, lrb_ref, wup_ref, bup_ref,
                of_ref, ob_ref, sf_ref, sb_ref, oi_ref, kv_ref, *, n_chunks):
    @pl.when(pl.program_id(1) == 0)
    def _():
        sf_ref[...] = jnp.zeros_like(sf_ref)
        sb_ref[...] = jnp.zeros_like(sb_ref)

    fwd = _GlaDirection(qkf_ref, vf_ref, lrf_ref[:, 0:GLA_RANK], wup_ref[0], bup_ref[0], of_ref, sf_ref,
                        oi_ref.at[0], kv_ref.at[0], n_chunks, False)
    bwd = _GlaDirection(qkb_ref, vb_ref, lrb_ref[:, GLA_RANK:2 * GLA_RANK], wup_ref[1], bup_ref[1], ob_ref,
                        sb_ref, oi_ref.at[1], kv_ref.at[1], n_chunks, True)
    for phase in ("cumulative_decay", "scale_operands", "scores_and_outer_products", "intra_chunk",
                  "inter_chunk"):
        getattr(fwd, phase)()
        getattr(bwd, phase)()


def _gla(proj, lr, wup, bup, bsz, seq):
    t = proj.shape[0]
    lb = min(512, seq)
    nblk = seq // lb
    vw = GLA_HEADS * GLA_DV
    fwd = lambda b, i: b * nblk + i
    bwd = lambda b, i: b * nblk + nblk - 1 - i
    return pl.pallas_call(
        functools.partial(_gla_kernel, n_chunks=lb // GLA_CHUNK),
        grid=(bsz, nblk),
        in_specs=[pl.BlockSpec((lb, 512), lambda b, i: (fwd(b, i), COL_GQK // 4)),
                  pl.BlockSpec((lb, 512), lambda b, i: (fwd(b, i), COL_GV // 4)),
                  pl.BlockSpec((lb, LANES), lambda b, i: (fwd(b, i), 0)),
                  pl.BlockSpec((lb, 512), lambda b, i: (bwd(b, i), COL_GQK // 4)),
                  pl.BlockSpec((lb, 512), lambda b, i: (bwd(b, i), COL_GV // 4)),
                  pl.BlockSpec((lb, LANES), lambda b, i: (bwd(b, i), 0)),
                  pl.BlockSpec((2, GLA_RANK, GLA_HEADS * GLA_DK), lambda b, i: (0, 0, 0)),
                  pl.BlockSpec((2, 1, GLA_HEADS * GLA_DK), lambda b, i: (0, 0, 0))],
        out_specs=[pl.BlockSpec((lb, vw), lambda b, i: (fwd(b, i), 0)),
                   pl.BlockSpec((lb, vw), lambda b, i: (bwd(b, i), 0))],
        out_shape=[jax.ShapeDtypeStruct((t, vw), BF16), jax.ShapeDtypeStruct((t, vw), BF16)],
        scratch_shapes=[pltpu.VMEM((GLA_HEADS, GLA_DV, GLA_DK), F32),
                        pltpu.VMEM((GLA_HEADS, GLA_DV, GLA_DK), F32),
                        pltpu.VMEM((2, lb, vw), F32),
                        pltpu.VMEM((2, lb // GLA_CHUNK, GLA_HEADS, GLA_DV, GLA_DK), F32)],
        compiler_params=_cparams(("parallel", "arbitrary")),
        name="gla",
    )(proj, proj, lr, proj, proj, lr, wup, bup)


def _t5_bucket_np(rel):
    nb = REL_BUCKETS // 2
    max_exact = nb // 2
    ret = np.where(rel > 0, nb, 0)
    n = np.abs(rel)
    nf = np.maximum(n, 1).astype(np.float64)
    large = max_exact + (np.log(nf / max_exact) / math.log(REL_MAX_DIST / max_exact)
                         * (nb - max_exact)).astype(np.int64)
    large = np.minimum(large, nb - 1)
    return ret + np.where(n < max_exact, n, large)


def _attn_kernel(fast_ref, far_ref, q_ref, k_ref, vt_ref, tile_ref, feat_ref, lam_ref, sg_ref, o_ref,
                 m_ref, l_ref, acc_ref, *, tq, tk, nk, lam_init):
    qi = pl.program_id(2)
    head = pl.program_id(1)
    q = q_ref[...]
    n_sub = tk // vt_ref.shape[-1]
    ts = tk // n_sub
    tasks = [(u, c) for u in range(n_sub) for c in range(2)]
    near, lo, hi = 0, 1, 2

    def q_aug(variant):
        feat = jnp.broadcast_to(feat_ref[0, variant:variant + 1, :], (tq, DIFF_DH))
        return [jnp.concatenate([q[:, c * DIFF_DH:(c + 1) * DIFF_DH].astype(F32), feat], axis=1).T.astype(BF16)
                for c in range(2)]

    def logits(kb, u, c, qa):
        r0 = pl.multiple_of(kb * tk + u * ts, ts)
        k = k_ref[pl.ds(r0, ts), c * LANES:(c + 1) * LANES]
        return jnp.dot(k, qa[c], preferred_element_type=F32)

    def pipelined(blocks):
        work = [(kb, qa, bias, step, u, c) for kb, qa, bias, step in blocks for u, c in tasks]
        s_next = logits(work[0][0], work[0][4], work[0][5], work[0][1])
        for t, (kb, qa, bias, step, u, c) in enumerate(work):
            s = s_next
            if t + 1 < len(work):
                nxt = work[t + 1]
                s_next = logits(nxt[0], nxt[4], nxt[5], nxt[1])
            if bias is not None:
                s = s + bias[u * ts:(u + 1) * ts, :]
            step(s, c, vt_ref[kb * n_sub + u])

    def bounded_step(s, c, vt):
        p = jnp.exp2(s)
        l_ref[c] = l_ref[c] + jnp.sum(p, axis=0, keepdims=True)
        acc_ref[c] = acc_ref[c] + jnp.dot(vt, p.astype(BF16), preferred_element_type=F32)

    def online_step(cst):
        def step(s, c, vt):
            m_old = m_ref[c]
            m_new = jnp.maximum(m_old, jnp.max(s, axis=0, keepdims=True) + cst)
            p = jnp.exp2(s - (m_new - cst))
            alpha = jnp.exp2(m_old - m_new)
            l_ref[c] = alpha * l_ref[c] + jnp.sum(p, axis=0, keepdims=True)
            acc_ref[c] = alpha * acc_ref[c] + jnp.dot(vt, p.astype(BF16), preferred_element_type=F32)
            m_ref[c] = m_new
        return step

    def sweep(far_lo_step, far_hi_step, near_step, group):
        q_near, q_lo, q_hi = q_aug(near), q_aug(lo), q_aug(hi)

        def far(start, count, qa, step):
            def body(i, carry):
                pipelined([(start + group * i + g, qa, None, step) for g in range(group)])
                return carry

            lax.fori_loop(0, count // group, body, 0)
            done = count // group * group
            part = group // 2
            while part >= 1:
                @pl.when((count - done) % (2 * part) >= part)
                def _(part=part, done=done):
                    pipelined([(start + done + g, qa, None, step) for g in range(part)])
                done = done + jnp.where((count - done) % (2 * part) >= part, part, 0)
                part //= 2

        far(0, jnp.maximum(qi - 1, 0), q_lo, far_lo_step)
        for has_left in (False, True):
            for has_right in (False, True):
                dlts = ([-1] if has_left else []) + [0] + ([1] if has_right else [])

                left_ok = (qi > 0) if has_left else (qi == 0)
                right_ok = (qi < nk - 1) if has_right else (qi >= nk - 1)

                @pl.when(jnp.logical_and(left_ok, right_ok))
                def _(dlts=dlts):
                    pipelined([(qi + dlt, q_near, tile_ref.at[0, dlt + 1], near_step) for dlt in dlts])
        hi_start = jnp.minimum(qi + 2, nk)
        far(hi_start, nk - hi_start, q_hi, far_hi_step)

    m_ref[...] = jnp.full_like(m_ref, -1e30)
    l_ref[...] = jnp.zeros_like(l_ref)
    acc_ref[...] = jnp.zeros_like(acc_ref)

    @pl.when(fast_ref[0] == 1)
    def _():
        sweep(bounded_step, bounded_step, bounded_step, 4)

    @pl.when(fast_ref[0] == 0)
    def _():
        sweep(online_step(far_ref[head, 0]), online_step(far_ref[head, 1]), online_step(0.0), 1)

    lv = lam_ref[...]
    lam = (jnp.exp(jnp.sum(lv[0:1] * lv[1:2], axis=-1, keepdims=True))
           - jnp.exp(jnp.sum(lv[2:3] * lv[3:4], axis=-1, keepdims=True)) + lam_init)
    out_t = acc_ref[0] * (1.0 / l_ref[0]) - acc_ref[1] * (lam / l_ref[1])
    out = out_t.T
    ms = jnp.mean(out * out, axis=-1, keepdims=True)
    out = out * lax.rsqrt(ms + RMS_EPS) * sg_ref[...] * (1.0 - lam_init)
    o_ref[...] = out.astype(o_ref.dtype)


def _attn_tile(seq):
    return min(512, seq)


def _attn(proj, vt, tiles, far, fast, feat, lam_vecs, subn_g, bsz, seq, lam_init):
    t = proj.shape[0]
    tq = tk = _attn_tile(seq)
    nq = seq // tq
    nk = seq // tk
    vb = vt.shape[-1]
    return pl.pallas_call(
        functools.partial(_attn_kernel, tq=tq, tk=tk, nk=nk, lam_init=lam_init),
        grid=(bsz, DIFF_HEADS, nq),
        in_specs=[pl.BlockSpec(memory_space=pltpu.SMEM),
                  pl.BlockSpec(memory_space=pltpu.SMEM),
                  pl.BlockSpec((tq, LANES), lambda b, h, i: (b * nq + i, COL_DQ + h)),
                  pl.BlockSpec((seq, 2 * LANES), lambda b, h, i: (b, COL_DKA // 2 + h)),
                  pl.BlockSpec((seq // vb, DIFF_DV, vb), lambda b, h, i: (b, h, 0)),
                  pl.BlockSpec((1, 3, tk, tq), lambda b, h, i: (h, 0, 0, 0)),
                  pl.BlockSpec((1, 3, DIFF_DH), lambda b, h, i: (h, 0, 0)),
                  pl.BlockSpec((4, DIFF_DH), lambda b, h, i: (0, 0)),
                  pl.BlockSpec((1, DIFF_DV), lambda b, h, i: (0, 0))],
        out_specs=pl.BlockSpec((tq, DIFF_DV), lambda b, h, i: (b * nq + i, h)),
        out_shape=jax.ShapeDtypeStruct((t, DIFF_HEADS * DIFF_DV), BF16),
        scratch_shapes=[pltpu.VMEM((2, 1, tq), F32), pltpu.VMEM((2, 1, tq), F32),
                        pltpu.VMEM((2, DIFF_DV, tq), F32)],
        compiler_params=_cparams(("parallel", "parallel", "arbitrary")),
        name="attn",
    )(fast, far, proj, proj, vt, tiles, feat, lam_vecs, subn_g)


def _bias_tables(rel_bias, tq, tk):
    n = tk + tq - 1
    log2e = math.log2(math.e)
    tiles = []
    for d in (-1, 0, 1):
        onehot = np.eye(REL_BUCKETS, dtype=np.float32)[_t5_bucket_np(d * tk + np.arange(n) - (tq - 1))]
        g = jnp.dot(jnp.asarray(onehot), rel_bias.astype(F32), precision=HIGHEST).T * log2e
        hank = jnp.tile(g, (1, tk + 1))[:, :tk * (n + 1)].reshape(-1, tk, n + 1)
        tiles.append(hank[:, :, :tq][:, :, ::-1])
    tiles = jnp.stack(tiles, axis=1)
    nb = REL_BUCKETS // 2
    far = jnp.stack([rel_bias[nb - 1], rel_bias[2 * nb - 1]], axis=-1).astype(F32) * log2e
    return tiles, far


def _softmax_features(far, rel_bias, qn_g, kn_g):
    log2e = math.log2(math.e)
    bound = (1.02 * DIFF_DH ** 0.5 * log2e * jnp.max(jnp.abs(qn_g)) * jnp.max(jnp.abs(kn_g))
             + log2e * jnp.max(jnp.abs(rel_bias)))
    fast = bound <= SOFTMAX_SAFE_RANGE
    neg_bound = -bound.astype(BF16).astype(F32)
    c_hi = far.astype(BF16).astype(F32)
    c_lo = (far - c_hi).astype(BF16).astype(F32)
    zeros = jnp.zeros_like(c_hi[:, 0])
    nbv = jnp.broadcast_to(neg_bound, zeros.shape)
    rows = jnp.stack([jnp.stack([nbv, zeros, zeros], -1),
                      jnp.stack([nbv, c_hi[:, 0], c_lo[:, 0]], -1),
                      jnp.stack([nbv, c_hi[:, 1], c_lo[:, 1]], -1)], axis=1)
    rows = jnp.where(fast, rows, 0.0)
    feat = jnp.pad(rows, ((0, 0), (0, 0), (0, DIFF_DH - N_FEAT)))
    return fast.astype(jnp.int32).reshape(1), feat


def _mix_kernel(x_ref, of_ref, ob_ref, gg_ref, ob2_ref, ga_ref, gb_ref, mod_ref, gn_ref, n2_ref,
                wa_ref, wb_ref, wo_ref, wr_ref, br_ref,
                x1_ref, h2_ref, info_ref, cnt_ref, base_ref, *, tm):
    @pl.when(pl.program_id(0) == 0)
    def _():
        base_ref[...] = jnp.zeros_like(base_ref)

    osum = of_ref[...].astype(F32) + ob_ref[...].astype(F32)
    parts = []
    for h in range(GLA_HEADS):
        sl = osum[:, h * GLA_DV:(h + 1) * GLA_DV]
        ms = jnp.mean(sl * sl, axis=-1, keepdims=True)
        parts.append(sl * lax.rsqrt(ms + RMS_EPS) * gn_ref[...])
    gg = gg_ref[...].astype(F32)
    o_a = jnp.concatenate(parts, axis=-1) * (gg * jax.nn.sigmoid(gg))
    y_a = jnp.dot(o_a.astype(BF16), wa_ref[...], preferred_element_type=F32)
    y_b = jnp.dot(ob2_ref[...], wb_ref[...], preferred_element_type=F32)
    merged = (jax.nn.sigmoid(ga_ref[...].astype(F32)) * y_a
              + jax.nn.sigmoid(gb_ref[...].astype(F32)) * y_b)
    y = jnp.dot(merged.astype(BF16), wo_ref[...], preferred_element_type=F32)
    x1 = x_ref[...] + mod_ref[0, 2:3, :] * y
    x1_ref[...] = x1

    ms = jnp.mean(x1 * x1, axis=-1, keepdims=True)
    h2 = x1 * lax.rsqrt(ms + RMS_EPS) * n2_ref[...]
    h2 = h2 * (1.0 + mod_ref[0, 4:5, :]) + mod_ref[0, 3:4, :]
    h2_ref[...] = h2

    h2_hi = h2.astype(BF16)
    h2_lo = (h2 - h2_hi.astype(F32)).astype(BF16)
    logits = (jnp.dot(h2_hi, wr_ref[0], preferred_element_type=F32)
              + jnp.dot(h2_lo, wr_ref[0], preferred_element_type=F32)
              + jnp.dot(h2_hi, wr_ref[1], preferred_element_type=F32)) + br_ref[...]
    lane_i = lax.broadcasted_iota(jnp.int32, logits.shape, 1)
    lane = lane_i.astype(F32)
    neg = jnp.float32(-3e38)
    big = jnp.float32(1 << 20)
    is_g = lane < N_GROUPS
    gl = jnp.where(is_g, logits, neg)
    gmax = jnp.max(gl, axis=-1, keepdims=True)
    grp = jnp.min(jnp.where(jnp.logical_and(is_g, gl == gmax), lane, big), axis=-1, keepdims=True)
    p_grp = 1.0 / jnp.sum(jnp.where(is_g, jnp.exp(gl - gmax), 0.0), axis=-1, keepdims=True)
    lo = N_GROUPS + grp * EXPERTS_PER_GROUP
    in_grp = jnp.logical_and(lane >= lo, lane < lo + EXPERTS_PER_GROUP)
    el = jnp.where(in_grp, logits, neg)
    e1 = jnp.max(el, axis=-1, keepdims=True)
    i1 = jnp.min(jnp.where(jnp.logical_and(in_grp, el == e1), lane, big), axis=-1, keepdims=True)
    rest = jnp.logical_and(in_grp, lane != i1)
    el2 = jnp.where(rest, logits, neg)
    e2 = jnp.max(el2, axis=-1, keepdims=True)
    i2 = jnp.min(jnp.where(jnp.logical_and(rest, el2 == e2), lane, big), axis=-1, keepdims=True)
    r = jnp.exp(e2 - e1)
    w1 = p_grp / (1.0 + r)
    w2 = w1 * r

    oh1 = (lane == i1).astype(F32)
    oh2 = (lane == i2).astype(F32)
    rr = lax.broadcasted_iota(jnp.int32, (tm, tm), 0)
    cc = lax.broadcasted_iota(jnp.int32, (tm, tm), 1)
    tril = (cc < rr).astype(BF16)
    before = jnp.dot(tril, (oh1 + oh2).astype(BF16), preferred_element_type=F32) + base_ref[...]
    rank1 = jnp.sum(oh1 * before, axis=-1, keepdims=True)
    rank2 = jnp.sum(oh2 * before, axis=-1, keepdims=True)
    base_new = base_ref[...] + jnp.sum(oh1 + oh2, axis=0, keepdims=True)
    base_ref[...] = base_new
    cnt_ref[...] = jnp.broadcast_to(base_new, cnt_ref.shape)

    ex1 = i1 - N_GROUPS
    ex2 = i2 - N_GROUPS
    info = jnp.zeros(logits.shape, F32)
    for idx, val in enumerate((ex1, ex2, rank1, rank2, w1, w2)):
        info = jnp.where(lane_i == idx, val, info)
    info_ref[...] = info


def _mix(x, of, ob, proj, ob2, mod, gn, n2, wa, wb, wo, wr, br, seq):
    t, d = x.shape
    tm = min(512, seq)
    tpb = seq // tm
    row = lambda i: (i, 0)
    const = lambda i: (0, 0)
    return pl.pallas_call(
        functools.partial(_mix_kernel, tm=tm),
        grid=(t // tm,),
        in_specs=[pl.BlockSpec((tm, d), row),
                  pl.BlockSpec((tm, 512), row),
                  pl.BlockSpec((tm, 512), row),
                  pl.BlockSpec((tm, 512), lambda i: (i, COL_GG // 4)),
                  pl.BlockSpec((tm, 512), row),
                  pl.BlockSpec((tm, 1024), lambda i: (i, COL_GA // 8)),
                  pl.BlockSpec((tm, 1024), lambda i: (i, COL_GB // 8)),
                  pl.BlockSpec((1, 6, d), lambda i: (i // tpb, 0, 0)),
                  pl.BlockSpec((1, GLA_DV), const),
                  pl.BlockSpec((1, d), const),
                  pl.BlockSpec((512, d), const),
                  pl.BlockSpec((512, d), const),
                  pl.BlockSpec((d, d), const),
                  pl.BlockSpec((2, d, LANES), lambda i: (0, 0, 0)),
                  pl.BlockSpec((1, LANES), const)],
        out_specs=[pl.BlockSpec((tm, d), row),
                   pl.BlockSpec((tm, d), row),
                   pl.BlockSpec((tm, LANES), row),
                   pl.BlockSpec((8, LANES), const)],
        out_shape=[jax.ShapeDtypeStruct((t, d), F32),
                   jax.ShapeDtypeStruct((t, d), F32),
                   jax.ShapeDtypeStruct((t, LANES), F32),
                   jax.ShapeDtypeStruct((8, LANES), F32)],
        scratch_shapes=[pltpu.VMEM((1, LANES), F32)],
        compiler_params=_cparams(("arbitrary",)),
        name="mix",
    )(x, of, ob, proj, ob2, proj, proj, mod, gn, n2, wa, wb, wo, wr, br)


def _dispatch_kernel(pos_ref, h_ref, zeros_ref, xs_ref, sem_ref, *, tb):
    del zeros_ref

    def row_copy(r, k):
        return pltpu.make_async_copy(h_ref.at[pl.ds(r, 1), :],
                                     xs_ref.at[pl.ds(pos_ref[0, 0, 2 * r + k], 1), :], sem_ref)

    def start(r, carry):
        row_copy(r, 0).start(priority=0)
        row_copy(r, 1).start(priority=1)
        return carry

    def wait(r, carry):
        row_copy(r, 0).wait()
        row_copy(r, 1).wait()
        return carry

    lax.fori_loop(0, tb, start, 0, unroll=DMA_ISSUE_UNROLL)
    lax.fori_loop(0, tb, wait, 0, unroll=True)


def _dispatch(h2, pos3, n_rows, tb):
    t, d = h2.shape
    return pl.pallas_call(
        functools.partial(_dispatch_kernel, tb=tb),
        grid_spec=pltpu.PrefetchScalarGridSpec(
            num_scalar_prefetch=0,
            grid=(t // tb,),
            in_specs=[pl.BlockSpec((1, 1, 2 * tb), lambda i: (i, 0, 0), memory_space=pltpu.SMEM),
                      pl.BlockSpec((tb, d), lambda i: (i, 0)),
                      pl.BlockSpec(memory_space=pl.ANY)],
            out_specs=pl.BlockSpec(memory_space=pl.ANY),
            scratch_shapes=[pltpu.SemaphoreType.DMA(())]),
        out_shape=jax.ShapeDtypeStruct((n_rows, d), F32),
        input_output_aliases={2: 0},
        compiler_params=_cparams(("arbitrary",)),
        name="dispatch",
    )(pos3, h2, jnp.zeros((n_rows, d), F32))


def _expert_kernel(te_ref, nv_ref, xs_ref, w1_ref, w2_ref, ys_ref, w1b_ref, w2b_ref):
    i = pl.program_id(0)

    @pl.when(i >= nv_ref[0])
    def _():
        ys_ref[...] = jnp.zeros_like(ys_ref)

    new_expert = jnp.logical_or(i == 0, te_ref[i] != te_ref[jnp.maximum(i - 1, 0)])

    @pl.when(jnp.logical_and(i < nv_ref[0], new_expert))
    def _():
        w1b_ref[...] = w1_ref[0].astype(BF16)
        w2b_ref[...] = w2_ref[0].astype(BF16)

    @pl.when(i < nv_ref[0])
    def _():
        xb = xs_ref[...].astype(BF16)
        hu = jnp.dot(xb, w1b_ref[...], preferred_element_type=F32)
        hg = hu[:, :D_EXPERT]
        act = (hg * jax.nn.sigmoid(hg)) * hu[:, D_EXPERT:]
        ys_ref[...] = jnp.dot(act.astype(BF16), w2b_ref[...], preferred_element_type=F32)


def _experts(xs, tile_expert, n_valid, w1, w2, n_tiles):
    d = xs.shape[1]
    blk = lambda i, te, nv: (jnp.maximum(jnp.minimum(i, nv[0] - 1), 0), 0)
    return pl.pallas_call(
        _expert_kernel,
        grid_spec=pltpu.PrefetchScalarGridSpec(
            num_scalar_prefetch=2,
            grid=(n_tiles,),
            in_specs=[pl.BlockSpec((EXPERT_TILE, d), blk),
                      pl.BlockSpec((1, d, 2 * D_EXPERT), lambda i, te, nv: (te[i], 0, 0)),
                      pl.BlockSpec((1, D_EXPERT, d), lambda i, te, nv: (te[i], 0, 0))],
            out_specs=pl.BlockSpec((EXPERT_TILE, d), lambda i, te, nv: (i, 0)),
            scratch_shapes=[pltpu.VMEM((d, 2 * D_EXPERT), BF16), pltpu.VMEM((D_EXPERT, d), BF16)]),
        out_shape=jax.ShapeDtypeStruct((n_tiles * EXPERT_TILE, d), F32),
        compiler_params=_cparams(("arbitrary",)),
        name="experts",
    )(tile_expert, n_valid, xs, w1, w2)


def _combine_kernel(pos_ref, ys_ref, x1_ref, info_ref, mod_ref, o_ref, buf_ref, sem_ref, *, tb):
    def row_copy(r, k):
        return pltpu.make_async_copy(ys_ref.at[pl.ds(pos_ref[0, 0, 2 * r + k], 1), :],
                                     buf_ref.at[k, pl.ds(r, 1), :], sem_ref)

    def start(r, carry):
        row_copy(r, 0).start(priority=0)
        row_copy(r, 1).start(priority=1)
        return carry

    def wait(r, carry):
        row_copy(r, 0).wait()
        row_copy(r, 1).wait()
        return carry

    lax.fori_loop(0, tb, start, 0, unroll=DMA_ISSUE_UNROLL)
    lax.fori_loop(0, tb, wait, 0, unroll=True)
    info = info_ref[...]
    y = buf_ref[0] * info[:, 4:5] + buf_ref[1] * info[:, 5:6]
    o_ref[...] = x1_ref[...] + mod_ref[0, 5:6, :] * y


def _combine(ys, pos3, x1, info, mod, seq, tb):
    t, d = x1.shape
    tpb = seq // tb
    return pl.pallas_call(
        functools.partial(_combine_kernel, tb=tb),
        grid_spec=pltpu.PrefetchScalarGridSpec(
            num_scalar_prefetch=0,
            grid=(t // tb,),
            in_specs=[pl.BlockSpec((1, 1, 2 * tb), lambda i: (i, 0, 0), memory_space=pltpu.SMEM),
                      pl.BlockSpec(memory_space=pl.ANY),
                      pl.BlockSpec((tb, d), lambda i: (i, 0)),
                      pl.BlockSpec((tb, LANES), lambda i: (i, 0)),
                      pl.BlockSpec((1, 6, d), lambda i: (i // tpb, 0, 0))],
            out_specs=pl.BlockSpec((tb, d), lambda i: (i, 0)),
            scratch_shapes=[pltpu.VMEM((2, tb, d), F32), pltpu.SemaphoreType.DMA(())]),
        out_shape=jax.ShapeDtypeStruct((t, d), F32),
        compiler_params=_cparams(("arbitrary",)),
        name="combine",
    )(pos3, ys, x1, info, mod)


def _moe(h2, x1, info, counts, mod, w1, w2, seq):
    t, d = h2.shape
    tb = min(256, seq)
    n_tiles = (2 * t + N_EXPERTS * (EXPERT_TILE - 1)) // EXPERT_TILE
    cnt = counts[0, N_GROUPS:N_GROUPS + N_EXPERTS].astype(jnp.int32)
    padded = ((cnt + EXPERT_TILE - 1) // EXPERT_TILE) * EXPERT_TILE
    ends = jnp.cumsum(padded)
    starts = ends - padded
    n_valid = (ends[-1] // EXPERT_TILE).astype(jnp.int32).reshape(1)
    tile_start = jnp.arange(n_tiles, dtype=jnp.int32) * EXPERT_TILE
    tile_expert = jnp.sum(ends[None, :] <= tile_start[:, None], axis=1).astype(jnp.int32)
    tile_expert = jnp.minimum(tile_expert, tile_expert[jnp.maximum(n_valid[0] - 1, 0)])
    eid = info[:, 0:2].astype(jnp.int32)
    pos = starts[eid] + info[:, 2:4].astype(jnp.int32)
    pos3 = pos.reshape(t // tb, 1, 2 * tb)
    xs = _dispatch(h2, pos3, n_tiles * EXPERT_TILE, tb)
    ys = _experts(xs, tile_expert, n_valid, w1, w2, n_tiles)
    return _combine(ys, pos3, x1, info, mod, seq, tb)


def _qk_norm_rows(qn_g, kn_g):
    zeros = jnp.zeros((DIFF_DH,), F32)
    feat = jnp.asarray(np.arange(DIFF_DH) < N_FEAT, F32)
    q_gain = jnp.tile(qn_g, 2 * DIFF_HEADS) * (DIFF_DH ** -0.5 * math.log2(math.e))
    k_gain = jnp.tile(jnp.concatenate([kn_g, zeros]), DIFF_HEADS)
    k_add = jnp.tile(jnp.concatenate([zeros, feat]), DIFF_HEADS)
    return jnp.stack([jnp.stack([q_gain, jnp.zeros_like(q_gain)]),
                      jnp.stack([k_gain, k_add]), jnp.stack([k_gain, k_add])])


def _prep_w_in(w_in_l):
    w = w_in_l
    c = np.cumsum([0, 256, 256, 512, 512, 16, 16, 512, 512, 512, 1024, 1024])
    seg = lambda i: w[:, c[i]:c[i + 1]]
    gq, gk, gv, gg, lrf, lrb, dq, dk, dv, ga, gb = (seg(i) for i in range(11))
    dk_aug = jnp.pad(dk.reshape(-1, 2 * DIFF_HEADS, 1, DIFF_DH), ((0, 0), (0, 0), (0, 1), (0, 0)))
    dk_aug = dk_aug.reshape(-1, 4 * DIFF_HEADS * DIFF_DH)
    main = jnp.concatenate([gq, gk, gv, ga, gb, dk_aug, gg, dq, dv], axis=1).astype(BF16)
    lr = jnp.concatenate([lrf, lrb, jnp.zeros((w.shape[0], LANES - 2 * GLA_RANK), w.dtype)], axis=1)
    return main, lr.astype(BF16)


def kernel(x, c, w_ada, b_ada, norm1_g, norm2_g, w_in, gla_w_up, gla_b_up, gla_norm_g, diff_qnorm_g,
           diff_knorm_g, diff_lambda, diff_subnorm_g, rel_bias, w_branch_a, w_branch_b, w_out,
           w_router_group, b_router_group, w_router_expert, b_router_expert, w_expert_in, w_expert_out):
    bsz, seq, d = x.shape
    t = bsz * seq
    depth = w_ada.shape[0]
    mod_all = _ada(c, w_ada, b_ada).reshape(depth, bsz, 6, d)

    tq = _attn_tile(seq)
    tiles, far = _bias_tables(rel_bias, tq, tq)
    grp = np.arange(PROJ_TILE) // DIFF_DH
    gmat = jnp.asarray((grp[:, None] == grp[None, :]).astype(np.float32) / DIFF_DH, dtype=BF16)

    xf = x.reshape(t, d)
    for l in range(depth):
        lam_init = 0.8 - 0.6 * math.exp(-0.3 * l)
        mod = mod_all[l]
        w_main, w_lr = _prep_w_in(w_in[l])
        qkg = _qk_norm_rows(diff_qnorm_g[l], diff_knorm_g[l])
        proj, lr, vt = _inproj(xf, mod, norm1_g[l].reshape(1, d), w_main, w_lr, gmat, qkg, seq)
        o_f, o_b = _gla(proj, lr, gla_w_up[l], gla_b_up[l].reshape(2, 1, -1), bsz, seq)
        fast, feat = _softmax_features(far, rel_bias, diff_qnorm_g[l], diff_knorm_g[l])
        o_b2 = _attn(proj, vt, tiles, far, fast, feat, diff_lambda[l], diff_subnorm_g[l].reshape(1, -1),
                     bsz, seq, lam_init)
        w_r = jnp.concatenate([w_router_group[l], w_router_expert[l],
                               jnp.zeros((d, LANES - N_GROUPS - N_EXPERTS), F32)], axis=1)
        w_r_hi = w_r.astype(BF16)
        w_r = jnp.stack([w_r_hi, (w_r - w_r_hi.astype(F32)).astype(BF16)])
        b_r = jnp.concatenate([b_router_group[l], b_router_expert[l],
                               jnp.zeros((LANES - N_GROUPS - N_EXPERTS,), F32)]).reshape(1, LANES)
        x1, h2, info, counts = _mix(xf, o_f, o_b, proj, o_b2, mod, gla_norm_g[l].reshape(1, -1),
                                    norm2_g[l].reshape(1, d), w_branch_a[l].astype(BF16),
                                    w_branch_b[l].astype(BF16), w_out[l].astype(BF16), w_r, b_r, seq)
        xf = _moe(h2, x1, info, counts, mod, w_expert_in[l], w_expert_out[l], seq)
    return xf.reshape(bsz, seq, d)
```

```python
import functools
import math

import numpy as np
import jax
import jax.numpy as jnp
from jax import lax
from jax.experimental import pallas as pl
from jax.experimental.pallas import tpu as pltpu

F32 = jnp.float32
BF16 = jnp.bfloat16
HIGHEST = lax.Precision.HIGHEST

D_MODEL = 1024
DEPTH = 2
GLA_HEADS, GLA_DK, GLA_DV, GLA_RANK, GLA_TAU, GLA_CHUNK = 4, 64, 128, 16, 16.0, 64
DIFF_HEADS, DIFF_DH, DIFF_DV = 4, 64, 128
REL_BUCKETS, REL_MAX_DIST = 32, 128
N_GROUPS, EXPERTS_PER_GROUP = 4, 8
N_EXPERTS = N_GROUPS * EXPERTS_PER_GROUP
D_EXPERT = D_MODEL // 2
RMS_EPS = 1e-6
LANES = 128
VMEM_LIMIT = 52 * 1024 * 1024
EXPERT_TILE = 256
DMA_ISSUE_UNROLL = 8

COL_GQK, COL_GV, COL_GA, COL_GB, COL_DKA, COL_GG, COL_DQ, COL_DV = 0, 4, 8, 16, 24, 32, 36, 40
N_PROJ = 44 * LANES
PROJ_TILE = 512
N_FEAT = 3
SOFTMAX_SAFE_RANGE = 60.0


def _cparams(sem, vmem=VMEM_LIMIT):
    return pltpu.CompilerParams(dimension_semantics=sem, vmem_limit_bytes=vmem)


def _ada_kernel(c_ref, w_ref, b_ref, o_ref):
    c = c_ref[...]
    ca = c * jax.nn.sigmoid(c)
    o_ref[0] = jnp.dot(ca, w_ref[0], precision=HIGHEST, preferred_element_type=F32) + b_ref[0]


def _ada(c, w_ada, b_ada):
    depth, d, n = w_ada.shape
    bsz = c.shape[0]
    tn = 1536
    return pl.pallas_call(
        _ada_kernel,
        grid=(depth, n // tn),
        in_specs=[pl.BlockSpec((bsz, d), lambda l, j: (0, 0)),
                  pl.BlockSpec((1, d, tn), lambda l, j: (l, 0, j)),
                  pl.BlockSpec((1, 1, tn), lambda l, j: (l, 0, j))],
        out_specs=pl.BlockSpec((1, bsz, tn), lambda l, j: (l, 0, j)),
        out_shape=jax.ShapeDtypeStruct((depth, bsz, n), F32),
        compiler_params=_cparams(("parallel", "parallel")),
        name="ada",
    )(c, w_ada, b_ada.reshape(depth, 1, n))


def _inproj_kernel(x_ref, mod_ref, g_ref, w_ref, wlr_ref, gmat_ref, qkg_ref, proj_ref, lr_ref, vt_ref):
    x = x_ref[...]
    ms = jnp.mean(x * x, axis=-1, keepdims=True)
    h = x * lax.rsqrt(ms + RMS_EPS) * g_ref[...]
    h = h * (1.0 + mod_ref[0, 1:2, :]) + mod_ref[0, 0:1, :]
    hb = h.astype(BF16)
    w = PROJ_TILE
    qk_tiles = {COL_DQ * LANES // w: 0, COL_DKA * LANES // w: 1, COL_DKA * LANES // w + 1: 2}
    for j in range(N_PROJ // w):
        acc = jnp.dot(hb, w_ref[:, j * w:(j + 1) * w], preferred_element_type=F32)
        if j in qk_tiles:
            ms2 = jnp.dot((acc * acc).astype(BF16), gmat_ref[...], preferred_element_type=F32)
            acc = acc * lax.rsqrt(ms2 + RMS_EPS) * qkg_ref[qk_tiles[j], 0:1, :] + qkg_ref[qk_tiles[j], 1:2, :]
        proj_ref[:, j * w:(j + 1) * w] = acc.astype(BF16)
        if j == COL_DV * LANES // w:
            vt_ref[0] = acc.T.astype(BF16)
    lr_ref[...] = jnp.dot(hb, wlr_ref[...], preferred_element_type=F32)


def _inproj(x, mod, g, w, wlr, gmat, qkg, seq):
    t, d = x.shape
    tm = min(256, seq)
    tpb = seq // tm
    return pl.pallas_call(
        _inproj_kernel,
        grid=(t // tm,),
        in_specs=[pl.BlockSpec((tm, d), lambda i: (i, 0)),
                  pl.BlockSpec((1, 6, d), lambda i: (i // tpb, 0, 0)),
                  pl.BlockSpec((1, d), lambda i: (0, 0)),
                  pl.BlockSpec((d, N_PROJ), lambda i: (0, 0)),
                  pl.BlockSpec((d, LANES), lambda i: (0, 0)),
                  pl.BlockSpec((PROJ_TILE, PROJ_TILE), lambda i: (0, 0)),
                  pl.BlockSpec((3, 2, PROJ_TILE), lambda i: (0, 0, 0))],
        out_specs=[pl.BlockSpec((tm, N_PROJ), lambda i: (i, 0)),
                   pl.BlockSpec((tm, LANES), lambda i: (i, 0)),
                   pl.BlockSpec((1, DIFF_HEADS * DIFF_DV, tm), lambda i: (i, 0, 0))],
        out_shape=[jax.ShapeDtypeStruct((t, N_PROJ), BF16),
                   jax.ShapeDtypeStruct((t, LANES), F32),
                   jax.ShapeDtypeStruct((t // tm, DIFF_HEADS * DIFF_DV, tm), BF16)],
        compiler_params=_cparams(("parallel",)),
        name="inproj",
    )(x, mod, g, w, wlr, gmat, qkg)


class _GlaDirection:
    def __init__(self, qk_ref, v_ref, lr, wup, bup, o_ref, s_ref, oi_ref, kv_ref, n_chunks, reverse):
        self.qk_ref, self.v_ref, self.lr, self.wup, self.bup = qk_ref, v_ref, lr, wup, bup
        self.o_ref, self.s_ref, self.oi_ref, self.kv_ref = o_ref, s_ref, oi_ref, kv_ref
        self.n_chunks, self.reverse = n_chunks, reverse
        c = GLA_CHUNK
        row = lax.broadcasted_iota(jnp.int32, (c, c), 0)
        col = lax.broadcasted_iota(jnp.int32, (c, c), 1)
        if reverse:
            self.smat, self.mask, self.tot_row = (col >= row).astype(BF16), col > row, 0
        else:
            self.smat, self.mask, self.tot_row = (col <= row).astype(BF16), col <= row, c - 1

    def _rows(self, n):
        return slice(n * GLA_CHUNK, (n + 1) * GLA_CHUNK)

    def _head(self, h, width):
        return slice(h * width, (h + 1) * width)

    def cumulative_decay(self):
        z = jnp.dot(self.lr, self.wup, precision=HIGHEST, preferred_element_type=F32) + self.bup
        lg = (jnp.minimum(z, 0.0) - jnp.log(1.0 + jnp.exp(-jnp.abs(z)))) * (1.0 / GLA_TAU)
        lg_hi = lg.astype(BF16)
        lg_lo = (lg - lg_hi.astype(F32)).astype(BF16)
        self.cums = [jnp.dot(self.smat, lg_hi[self._rows(n)], preferred_element_type=F32)
                     + jnp.dot(self.smat, lg_lo[self._rows(n)], preferred_element_type=F32)
                     for n in range(self.n_chunks)]

    def scale_operands(self):
        hd = GLA_HEADS * GLA_DK
        self.decs, self.q_dec, self.k_inv, self.k_end = [], [], [], []
        for n, cum in enumerate(self.cums):
            dec = jnp.exp(cum[self.tot_row:self.tot_row + 1, :])
            qk = self.qk_ref[self._rows(n), :].astype(F32)
            k_inv = qk[:, hd:] * jnp.exp(-cum)
            self.decs.append(dec)
            self.q_dec.append((qk[:, :hd] * jnp.exp(cum) * (GLA_DK ** -0.5)).astype(BF16))
            self.k_end.append((k_inv * dec).astype(BF16))
            self.k_inv.append(k_inv.astype(BF16))

    def scores_and_outer_products(self):
        self.scores = []
        for n in range(self.n_chunks):
            v = self.v_ref[self._rows(n), :]
            for h in range(GLA_HEADS):
                sl = self._head(h, GLA_DK)
                self.scores.append(lax.dot_general(self.q_dec[n][:, sl], self.k_inv[n][:, sl],
                                                   (((1,), (1,)), ((), ())), preferred_element_type=F32))
                self.kv_ref[n, h] = lax.dot_general(v[:, self._head(h, GLA_DV)], self.k_end[n][:, sl],
                                                    (((0,), (0,)), ((), ())), preferred_element_type=F32)

    def intra_chunk(self):
        for n in range(self.n_chunks):
            v = self.v_ref[self._rows(n), :]
            for h in range(GLA_HEADS):
                a = jnp.where(self.mask, self.scores[n * GLA_HEADS + h], 0.0).astype(BF16)
                self.oi_ref[self._rows(n), self._head(h, GLA_DV)] = jnp.dot(
                    a, v[:, self._head(h, GLA_DV)], preferred_element_type=F32)

    def inter_chunk(self):
        states = [self.s_ref[h] for h in range(GLA_HEADS)]
        order = range(self.n_chunks - 1, -1, -1) if self.reverse else range(self.n_chunks)
        for n in order:
            for h in range(GLA_HEADS):
                sl = self._head(h, GLA_DK)
                o = self.oi_ref[self._rows(n), self._head(h, GLA_DV)] + lax.dot_general(
                    self.q_dec[n][:, sl], states[h].astype(BF16), (((1,), (1,)), ((), ())),
                    preferred_element_type=F32)
                self.o_ref[self._rows(n), self._head(h, GLA_DV)] = o.astype(self.o_ref.dtype)
                states[h] = states[h] * self.decs[n][:, sl] + self.kv_ref[n, h]
        for h in range(GLA_HEADS):
            self.s_ref[h] = states[h]


def _gla_kernel(qkf_ref, vf_ref, lrf_ref, qkb_ref, vb_ref, lrb_ref, wup_ref, bup_ref,
                of_ref, ob_ref, sf_ref, sb_ref, oi_ref, kv_ref, *, n_chunks):
    @pl.when(pl.program_id(1) == 0)
    def _():
        sf_ref[...] = jnp.zeros_like(sf_ref)
        sb_ref[...] = jnp.zeros_like(sb_ref)

    fwd = _GlaDirection(qkf_ref, vf_ref, lrf_ref[:, 0:GLA_RANK], wup_ref[0], bup_ref[0], of_ref, sf_ref,
                        oi_ref.at[0], kv_ref.at[0], n_chunks, False)
    bwd = _GlaDirection(qkb_ref, vb_ref, lrb_ref[:, GLA_RANK:2 * GLA_RANK], wup_ref[1], bup_ref[1], ob_ref,
                        sb_ref, oi_ref.at[1], kv_ref.at[1], n_chunks, True)
    for phase in ("cumulative_decay", "scale_operands", "scores_and_outer_products", "intra_chunk",
                  "inter_chunk"):
        getattr(fwd, phase)()
        getattr(bwd, phase)()


def _gla(proj, lr, wup, bup, bsz, seq):
    t = proj.shape[0]
    lb = min(512, seq)
    nblk = seq // lb
    vw = GLA_HEADS * GLA_DV
    fwd = lambda b, i: b * nblk + i
    bwd = lambda b, i: b * nblk + nblk - 1 - i
    return pl.pallas_call(
        functools.partial(_gla_kernel, n_chunks=lb // GLA_CHUNK),
        grid=(bsz, nblk),
        in_specs=[pl.BlockSpec((lb, 512), lambda b, i: (fwd(b, i), COL_GQK // 4)),
                  pl.BlockSpec((lb, 512), lambda b, i: (fwd(b, i), COL_GV // 4)),
                  pl.BlockSpec((lb, LANES), lambda b, i: (fwd(b, i), 0)),
                  pl.BlockSpec((lb, 512), lambda b, i: (bwd(b, i), COL_GQK // 4)),
                  pl.BlockSpec((lb, 512), lambda b, i: (bwd(b, i), COL_GV // 4)),
                  pl.BlockSpec((lb, LANES), lambda b, i: (bwd(b, i), 0)),
                  pl.BlockSpec((2, GLA_RANK, GLA_HEADS * GLA_DK), lambda b, i: (0, 0, 0)),
                  pl.BlockSpec((2, 1, GLA_HEADS * GLA_DK), lambda b, i: (0, 0, 0))],
        out_specs=[pl.BlockSpec((lb, vw), lambda b, i: (fwd(b, i), 0)),
                   pl.BlockSpec((lb, vw), lambda b, i: (bwd(b, i), 0))],
        out_shape=[jax.ShapeDtypeStruct((t, vw), BF16), jax.ShapeDtypeStruct((t, vw), BF16)],
        scratch_shapes=[pltpu.VMEM((GLA_HEADS, GLA_DV, GLA_DK), F32),
                        pltpu.VMEM((GLA_HEADS, GLA_DV, GLA_DK), F32),
                        pltpu.VMEM((2, lb, vw), F32),
                        pltpu.VMEM((2, lb // GLA_CHUNK, GLA_HEADS, GLA_DV, GLA_DK), F32)],
        compiler_params=_cparams(("parallel", "arbitrary")),
        name="gla",
    )(proj, proj, lr, proj, proj, lr, wup, bup)


def _t5_bucket_np(rel):
    nb = REL_BUCKETS // 2
    max_exact = nb // 2
    ret = np.where(rel > 0, nb, 0)
    n = np.abs(rel)
    nf = np.maximum(n, 1).astype(np.float64)
    large = max_exact + (np.log(nf / max_exact) / math.log(REL_MAX_DIST / max_exact)
                         * (nb - max_exact)).astype(np.int64)
    large = np.minimum(large, nb - 1)
    return ret + np.where(n < max_exact, n, large)


def _attn_kernel(fast_ref, q_ref, k_ref, vt_ref, tile_ref, feat_ref, lam_ref, sg_ref, o_ref,
                 m_ref, l_ref, acc_ref, qa_ref, *, tq, tk, nk, lam_init):
    qi = pl.program_id(2)
    q = q_ref[...]
    n_sub = tk // vt_ref.shape[-1]
    ts = tk // n_sub
    tasks = [(u, c) for u in range(n_sub) for c in range(2)]
    far_tiles = (tile_ref.shape[1] - 1) // 2

    band, left, right = 0, 1, 2
    for var in (band, left, right):
        feat = jnp.broadcast_to(feat_ref[0, var:var + 1, :], (tq, DIFF_DH))
        for c in range(2):
            qa_ref[var, c] = jnp.concatenate([q[:, c * DIFF_DH:(c + 1) * DIFF_DH].astype(F32), feat],
                                             axis=1).T.astype(BF16)

    def logits(kb, var, u, c):
        r0 = pl.multiple_of(kb * tk + u * ts, ts)
        k = k_ref[pl.ds(r0, ts), c * LANES:(c + 1) * LANES]
        return jnp.dot(k, qa_ref[var, c], preferred_element_type=F32)

    def bias_tile(kb):
        return tile_ref.at[0, jnp.clip(kb - qi, -far_tiles, far_tiles) + far_tiles]

    def pipelined(blocks, step):
        work = [(kb, var, bias, u, c) for kb, var, bias in blocks for u, c in tasks]
        s_next = logits(work[0][0], work[0][1], work[0][3], work[0][4])
        for t, (kb, var, bias, u, c) in enumerate(work):
            s = s_next
            if t + 1 < len(work):
                nxt = work[t + 1]
                s_next = logits(nxt[0], nxt[1], nxt[3], nxt[4])
            if bias is not None:
                s = s + bias[u * ts:(u + 1) * ts, :]
            step(s, c, vt_ref[kb * n_sub + u])

    def bounded_step(s, c, vt):
        p = jnp.exp2(s)
        l_ref[c] = l_ref[c] + jnp.sum(p, axis=0, keepdims=True)
        acc_ref[c] = acc_ref[c] + jnp.dot(vt, p.astype(BF16), preferred_element_type=F32)

    def online_step(s, c, vt):
        m_old = m_ref[c]
        m_new = jnp.maximum(m_old, jnp.max(s, axis=0, keepdims=True))
        p = jnp.exp2(s - m_new)
        alpha = jnp.exp2(m_old - m_new)
        l_ref[c] = alpha * l_ref[c] + jnp.sum(p, axis=0, keepdims=True)
        acc_ref[c] = alpha * acc_ref[c] + jnp.dot(vt, p.astype(BF16), preferred_element_type=F32)
        m_ref[c] = m_new

    m_ref[...] = jnp.full_like(m_ref, -1e30)
    l_ref[...] = jnp.zeros_like(l_ref)
    acc_ref[...] = jnp.zeros_like(acc_ref)

    @pl.when(fast_ref[0] == 1)
    def _():
        offsets = list(range(nk)) if nk < 3 else [-1, 0, 1] + list(range(2, nk - 1))
        blocks = []
        for d in offsets:
            kb = qi + d
            kb = jnp.where(kb >= nk, kb - nk, jnp.where(kb < 0, kb + nk, kb))
            if 2 <= d <= nk - 2:
                blocks.append((kb, jnp.where(kb > qi, right, left), None))
            else:
                blocks.append((kb, band, bias_tile(kb)))
        pipelined(blocks, bounded_step)

    @pl.when(fast_ref[0] == 0)
    def _():
        group = 2 if nk % 2 == 0 else 1

        def body(i, carry):
            pipelined([(group * i + g, band, bias_tile(group * i + g)) for g in range(group)], online_step)
            return carry

        lax.fori_loop(0, nk // group, body, 0)

    lv = lam_ref[...]
    lam = (jnp.exp(jnp.sum(lv[0:1] * lv[1:2], axis=-1, keepdims=True))
           - jnp.exp(jnp.sum(lv[2:3] * lv[3:4], axis=-1, keepdims=True)) + lam_init)
    out_t = acc_ref[0] * (1.0 / l_ref[0]) - acc_ref[1] * (lam / l_ref[1])
    out = out_t.T
    ms = jnp.mean(out * out, axis=-1, keepdims=True)
    out = out * lax.rsqrt(ms + RMS_EPS) * sg_ref[...] * (1.0 - lam_init)
    o_ref[...] = out.astype(o_ref.dtype)


def _attn_tile(seq):
    return min(512, seq)


def _attn(proj, vt, tiles, fast, feat, lam_vecs, subn_g, bsz, seq, lam_init):
    t = proj.shape[0]
    tq = tk = _attn_tile(seq)
    nq = seq // tq
    nk = seq // tk
    vb = vt.shape[-1]
    return pl.pallas_call(
        functools.partial(_attn_kernel, tq=tq, tk=tk, nk=nk, lam_init=lam_init),
        grid=(bsz, DIFF_HEADS, nq),
        in_specs=[pl.BlockSpec(memory_space=pltpu.SMEM),
                  pl.BlockSpec((tq, LANES), lambda b, h, i: (b * nq + i, COL_DQ + h)),
                  pl.BlockSpec((seq, 2 * LANES), lambda b, h, i: (b, COL_DKA // 2 + h)),
                  pl.BlockSpec((seq // vb, DIFF_DV, vb), lambda b, h, i: (b, h, 0)),
                  pl.BlockSpec((1, tiles.shape[1], tk, tq), lambda b, h, i: (h, 0, 0, 0)),
                  pl.BlockSpec((1, 3, DIFF_DH), lambda b, h, i: (h, 0, 0)),
                  pl.BlockSpec((4, DIFF_DH), lambda b, h, i: (0, 0)),
                  pl.BlockSpec((1, DIFF_DV), lambda b, h, i: (0, 0))],
        out_specs=pl.BlockSpec((tq, DIFF_DV), lambda b, h, i: (b * nq + i, h)),
        out_shape=jax.ShapeDtypeStruct((t, DIFF_HEADS * DIFF_DV), BF16),
        scratch_shapes=[pltpu.VMEM((2, 1, tq), F32), pltpu.VMEM((2, 1, tq), F32),
                        pltpu.VMEM((2, DIFF_DV, tq), F32), pltpu.VMEM((3, 2, LANES, tq), BF16)],
        compiler_params=_cparams(("parallel", "parallel", "arbitrary")),
        name="attn",
    )(fast, proj, proj, vt, tiles, feat, lam_vecs, subn_g)


def _bias_tables(rel_bias, tq, tk):
    n = tk + tq - 1
    log2e = math.log2(math.e)
    tiles = []
    for d in (-2, -1, 0, 1, 2):
        onehot = np.eye(REL_BUCKETS, dtype=np.float32)[_t5_bucket_np(d * tk + np.arange(n) - (tq - 1))]
        g = jnp.dot(jnp.asarray(onehot), rel_bias.astype(F32), precision=HIGHEST).T * log2e
        hank = jnp.tile(g, (1, tk + 1))[:, :tk * (n + 1)].reshape(-1, tk, n + 1)
        tiles.append(hank[:, :, :tq][:, :, ::-1])
    return jnp.stack(tiles, axis=1)


def _softmax_features(rel_bias, qn_g, kn_g):
    log2e = math.log2(math.e)
    bound = (1.02 * DIFF_DH ** 0.5 * log2e * jnp.max(jnp.abs(qn_g)) * jnp.max(jnp.abs(kn_g))
             + log2e * jnp.max(jnp.abs(rel_bias)))
    fast = bound <= SOFTMAX_SAFE_RANGE
    nb = REL_BUCKETS // 2
    far = jnp.stack([rel_bias[nb - 1], rel_bias[2 * nb - 1]], axis=-1).astype(F32) * log2e
    c_hi = far.astype(BF16).astype(F32)
    c_lo = (far - c_hi).astype(BF16).astype(F32)
    zeros = jnp.zeros_like(c_hi[:, 0])
    nbv = jnp.broadcast_to(-bound.astype(BF16).astype(F32), zeros.shape)
    rows = jnp.stack([jnp.stack([nbv, zeros, zeros], -1),
                      jnp.stack([nbv, c_hi[:, 0], c_lo[:, 0]], -1),
                      jnp.stack([nbv, c_hi[:, 1], c_lo[:, 1]], -1)], axis=1)
    feat = jnp.pad(jnp.where(fast, rows, 0.0), ((0, 0), (0, 0), (0, DIFF_DH - N_FEAT)))
    return fast.astype(jnp.int32).reshape(1), feat


def _mix_kernel(x_ref, of_ref, ob_ref, gg_ref, ob2_ref, ga_ref, gb_ref, mod_ref, gn_ref, n2_ref,
                wa_ref, wb_ref, wo_ref, wr_ref, br_ref,
                x1_ref, h2_ref, info_ref, cnt_ref, base_ref, *, tm):
    @pl.when(pl.program_id(0) == 0)
    def _():
        base_ref[...] = jnp.zeros_like(base_ref)

    osum = of_ref[...].astype(F32) + ob_ref[...].astype(F32)
    parts = []
    for h in range(GLA_HEADS):
        sl = osum[:, h * GLA_DV:(h + 1) * GLA_DV]
        ms = jnp.mean(sl * sl, axis=-1, keepdims=True)
        parts.append(sl * lax.rsqrt(ms + RMS_EPS) * gn_ref[...])
    gg = gg_ref[...].astype(F32)
    o_a = jnp.concatenate(parts, axis=-1) * (gg * jax.nn.sigmoid(gg))
    y_a = jnp.dot(o_a.astype(BF16), wa_ref[...], preferred_element_type=F32)
    y_b = jnp.dot(ob2_ref[...], wb_ref[...], preferred_element_type=F32)
    merged = (jax.nn.sigmoid(ga_ref[...].astype(F32)) * y_a
              + jax.nn.sigmoid(gb_ref[...].astype(F32)) * y_b)
    y = jnp.dot(merged.astype(BF16), wo_ref[...], preferred_element_type=F32)
    x1 = x_ref[...] + mod_ref[0, 2:3, :] * y
    x1_ref[...] = x1

    ms = jnp.mean(x1 * x1, axis=-1, keepdims=True)
    h2 = x1 * lax.rsqrt(ms + RMS_EPS) * n2_ref[...]
    h2 = h2 * (1.0 + mod_ref[0, 4:5, :]) + mod_ref[0, 3:4, :]
    h2_ref[...] = h2

    h2_hi = h2.astype(BF16)
    h2_lo = (h2 - h2_hi.astype(F32)).astype(BF16)
    logits = (jnp.dot(h2_hi, wr_ref[0], preferred_element_type=F32)
              + jnp.dot(h2_lo, wr_ref[0], preferred_element_type=F32)
              + jnp.dot(h2_hi, wr_ref[1], preferred_element_type=F32)) + br_ref[...]
    lane_i = lax.broadcasted_iota(jnp.int32, logits.shape, 1)
    lane = lane_i.astype(F32)
    neg = jnp.float32(-3e38)
    big = jnp.float32(1 << 20)
    is_g = lane < N_GROUPS
    gl = jnp.where(is_g, logits, neg)
    gmax = jnp.max(gl, axis=-1, keepdims=True)
    grp = jnp.min(jnp.where(jnp.logical_and(is_g, gl == gmax), lane, big), axis=-1, keepdims=True)
    p_grp = 1.0 / jnp.sum(jnp.where(is_g, jnp.exp(gl - gmax), 0.0), axis=-1, keepdims=True)
    lo = N_GROUPS + grp * EXPERTS_PER_GROUP
    in_grp = jnp.logical_and(lane >= lo, lane < lo + EXPERTS_PER_GROUP)
    el = jnp.where(in_grp, logits, neg)
    e1 = jnp.max(el, axis=-1, keepdims=True)
    i1 = jnp.min(jnp.where(jnp.logical_and(in_grp, el == e1), lane, big), axis=-1, keepdims=True)
    rest = jnp.logical_and(in_grp, lane != i1)
    el2 = jnp.where(rest, logits, neg)
    e2 = jnp.max(el2, axis=-1, keepdims=True)
    i2 = jnp.min(jnp.where(jnp.logical_and(rest, el2 == e2), lane, big), axis=-1, keepdims=True)
    r = jnp.exp(e2 - e1)
    w1 = p_grp / (1.0 + r)
    w2 = w1 * r

    oh1 = (lane == i1).astype(F32)
    oh2 = (lane == i2).astype(F32)
    rr = lax.broadcasted_iota(jnp.int32, (tm, tm), 0)
    cc = lax.broadcasted_iota(jnp.int32, (tm, tm), 1)
    tril = (cc < rr).astype(BF16)
    before = jnp.dot(tril, (oh1 + oh2).astype(BF16), preferred_element_type=F32) + base_ref[...]
    rank1 = jnp.sum(oh1 * before, axis=-1, keepdims=True)
    rank2 = jnp.sum(oh2 * before, axis=-1, keepdims=True)
    base_new = base_ref[...] + jnp.sum(oh1 + oh2, axis=0, keepdims=True)
    base_ref[...] = base_new
    cnt_ref[...] = jnp.broadcast_to(base_new, cnt_ref.shape)

    ex1 = i1 - N_GROUPS
    ex2 = i2 - N_GROUPS
    info = jnp.zeros(logits.shape, F32)
    for idx, val in enumerate((ex1, ex2, rank1, rank2, w1, w2)):
        info = jnp.where(lane_i == idx, val, info)
    info_ref[...] = info


def _mix(x, of, ob, proj, ob2, mod, gn, n2, wa, wb, wo, wr, br, seq):
    t, d = x.shape
    tm = min(512, seq)
    tpb = seq // tm
    row = lambda i: (i, 0)
    const = lambda i: (0, 0)
    return pl.pallas_call(
        functools.partial(_mix_kernel, tm=tm),
        grid=(t // tm,),
        in_specs=[pl.BlockSpec((tm, d), row),
                  pl.BlockSpec((tm, 512), row),
                  pl.BlockSpec((tm, 512), row),
                  pl.BlockSpec((tm, 512), lambda i: (i, COL_GG // 4)),
                  pl.BlockSpec((tm, 512), row),
                  pl.BlockSpec((tm, 1024), lambda i: (i, COL_GA // 8)),
                  pl.BlockSpec((tm, 1024), lambda i: (i, COL_GB // 8)),
                  pl.BlockSpec((1, 6, d), lambda i: (i // tpb, 0, 0)),
                  pl.BlockSpec((1, GLA_DV), const),
                  pl.BlockSpec((1, d), const),
                  pl.BlockSpec((512, d), const),
                  pl.BlockSpec((512, d), const),
                  pl.BlockSpec((d, d), const),
                  pl.BlockSpec((2, d, LANES), lambda i: (0, 0, 0)),
                  pl.BlockSpec((1, LANES), const)],
        out_specs=[pl.BlockSpec((tm, d), row),
                   pl.BlockSpec((tm, d), row),
                   pl.BlockSpec((tm, LANES), row),
                   pl.BlockSpec((8, LANES), const)],
        out_shape=[jax.ShapeDtypeStruct((t, d), F32),
                   jax.ShapeDtypeStruct((t, d), F32),
                   jax.ShapeDtypeStruct((t, LANES), F32),
                   jax.ShapeDtypeStruct((8, LANES), F32)],
        scratch_shapes=[pltpu.VMEM((1, LANES), F32)],
        compiler_params=_cparams(("arbitrary",)),
        name="mix",
    )(x, of, ob, proj, ob2, proj, proj, mod, gn, n2, wa, wb, wo, wr, br)


def _dispatch_kernel(pos_ref, h_ref, zeros_ref, xs_ref, sem_ref, *, tb):
    del zeros_ref

    def row_copy(r, k):
        return pltpu.make_async_copy(h_ref.at[pl.ds(r, 1), :],
                                     xs_ref.at[pl.ds(pos_ref[0, 0, 2 * r + k], 1), :], sem_ref)

    def start(r, carry):
        row_copy(r, 0).start(priority=0)
        row_copy(r, 1).start(priority=1)
        return carry

    def wait(r, carry):
        row_copy(r, 0).wait()
        row_copy(r, 1).wait()
        return carry

    lax.fori_loop(0, tb, start, 0, unroll=DMA_ISSUE_UNROLL)
    lax.fori_loop(0, tb, wait, 0, unroll=True)


def _dispatch(h2, pos3, n_rows, tb):
    t, d = h2.shape
    return pl.pallas_call(
        functools.partial(_dispatch_kernel, tb=tb),
        grid_spec=pltpu.PrefetchScalarGridSpec(
            num_scalar_prefetch=0,
            grid=(t // tb,),
            in_specs=[pl.BlockSpec((1, 1, 2 * tb), lambda i: (i, 0, 0), memory_space=pltpu.SMEM),
                      pl.BlockSpec((tb, d), lambda i: (i, 0)),
                      pl.BlockSpec(memory_space=pl.ANY)],
            out_specs=pl.BlockSpec(memory_space=pl.ANY),
            scratch_shapes=[pltpu.SemaphoreType.DMA(())]),
        out_shape=jax.ShapeDtypeStruct((n_rows, d), F32),
        input_output_aliases={2: 0},
        compiler_params=_cparams(("arbitrary",)),
        name="dispatch",
    )(pos3, h2, jnp.zeros((n_rows, d), F32))


def _expert_kernel(te_ref, nv_ref, xs_ref, w1_ref, w2_ref, ys_ref, w1b_ref, w2b_ref):
    i = pl.program_id(0)

    @pl.when(i >= nv_ref[0])
    def _():
        ys_ref[...] = jnp.zeros_like(ys_ref)

    new_expert = jnp.logical_or(i == 0, te_ref[i] != te_ref[jnp.maximum(i - 1, 0)])

    @pl.when(jnp.logical_and(i < nv_ref[0], new_expert))
    def _():
        w1b_ref[...] = w1_ref[0].astype(BF16)
        w2b_ref[...] = w2_ref[0].astype(BF16)

    @pl.when(i < nv_ref[0])
    def _():
        xb = xs_ref[...].astype(BF16)
        hu = jnp.dot(xb, w1b_ref[...], preferred_element_type=F32)
        hg = hu[:, :D_EXPERT]
        act = (hg * jax.nn.sigmoid(hg)) * hu[:, D_EXPERT:]
        ys_ref[...] = jnp.dot(act.astype(BF16), w2b_ref[...], preferred_element_type=F32)


def _experts(xs, tile_expert, n_valid, w1, w2, n_tiles):
    d = xs.shape[1]
    blk = lambda i, te, nv: (jnp.maximum(jnp.minimum(i, nv[0] - 1), 0), 0)
    return pl.pallas_call(
        _expert_kernel,
        grid_spec=pltpu.PrefetchScalarGridSpec(
            num_scalar_prefetch=2,
            grid=(n_tiles,),
            in_specs=[pl.BlockSpec((EXPERT_TILE, d), blk),
                      pl.BlockSpec((1, d, 2 * D_EXPERT), lambda i, te, nv: (te[i], 0, 0)),
                      pl.BlockSpec((1, D_EXPERT, d), lambda i, te, nv: (te[i], 0, 0))],
            out_specs=pl.BlockSpec((EXPERT_TILE, d), lambda i, te, nv: (i, 0)),
            scratch_shapes=[pltpu.VMEM((d, 2 * D_EXPERT), BF16), pltpu.VMEM((D_EXPERT, d), BF16)]),
        out_shape=jax.ShapeDtypeStruct((n_tiles * EXPERT_TILE, d), F32),
        compiler_params=_cparams(("arbitrary",)),
        name="experts",
    )(tile_expert, n_valid, xs, w1, w2)


def _combine_kernel(pos_ref, ys_ref, x1_ref, info_ref, mod_ref, o_ref, buf_ref, sem_ref, *, tb):
    def row_copy(r, k):
        return pltpu.make_async_copy(ys_ref.at[pl.ds(pos_ref[0, 0, 2 * r + k], 1), :],
                                     buf_ref.at[k, pl.ds(r, 1), :], sem_ref)

    def start(r, carry):
        row_copy(r, 0).start(priority=0)
        row_copy(r, 1).start(priority=1)
        return carry

    def wait(r, carry):
        row_copy(r, 0).wait()
        row_copy(r, 1).wait()
        return carry

    lax.fori_loop(0, tb, start, 0, unroll=DMA_ISSUE_UNROLL)
    lax.fori_loop(0, tb, wait, 0, unroll=True)
    info = info_ref[...]
    y = buf_ref[0] * info[:, 4:5] + buf_ref[1] * info[:, 5:6]
    o_ref[...] = x1_ref[...] + mod_ref[0, 5:6, :] * y


def _combine(ys, pos3, x1, info, mod, seq, tb):
    t, d = x1.shape
    tpb = seq // tb
    return pl.pallas_call(
        functools.partial(_combine_kernel, tb=tb),
        grid_spec=pltpu.PrefetchScalarGridSpec(
            num_scalar_prefetch=0,
            grid=(t // tb,),
            in_specs=[pl.BlockSpec((1, 1, 2 * tb), lambda i: (i, 0, 0), memory_space=pltpu.SMEM),
                      pl.BlockSpec(memory_space=pl.ANY),
                      pl.BlockSpec((tb, d), lambda i: (i, 0)),
                      pl.BlockSpec((tb, LANES), lambda i: (i, 0)),
                      pl.BlockSpec((1, 6, d), lambda i: (i // tpb, 0, 0))],
            out_specs=pl.BlockSpec((tb, d), lambda i: (i, 0)),
            scratch_shapes=[pltpu.VMEM((2, tb, d), F32), pltpu.SemaphoreType.DMA(())]),
        out_shape=jax.ShapeDtypeStruct((t, d), F32),
        compiler_params=_cparams(("arbitrary",)),
        name="combine",
    )(pos3, ys, x1, info, mod)


def _moe(h2, x1, info, counts, mod, w1, w2, layer, seq):
    t, d = h2.shape
    tb = min(256, seq)
    n_tiles = (2 * t + N_EXPERTS * (EXPERT_TILE - 1)) // EXPERT_TILE
    cnt = counts[0, N_GROUPS:N_GROUPS + N_EXPERTS].astype(jnp.int32)
    padded = ((cnt + EXPERT_TILE - 1) // EXPERT_TILE) * EXPERT_TILE
    ends = jnp.cumsum(padded)
    starts = ends - padded
    n_valid = (ends[-1] // EXPERT_TILE).astype(jnp.int32).reshape(1)
    tile_start = jnp.arange(n_tiles, dtype=jnp.int32) * EXPERT_TILE
    tile_expert = jnp.sum(ends[None, :] <= tile_start[:, None], axis=1).astype(jnp.int32)
    tile_expert = jnp.minimum(tile_expert, tile_expert[jnp.maximum(n_valid[0] - 1, 0)])
    tile_expert = tile_expert + layer * N_EXPERTS
    eid = info[:, 0:2].astype(jnp.int32)
    pos = starts[eid] + info[:, 2:4].astype(jnp.int32)
    pos3 = pos.reshape(t // tb, 1, 2 * tb)
    xs = _dispatch(h2, pos3, n_tiles * EXPERT_TILE, tb)
    ys = _experts(xs, tile_expert, n_valid, w1, w2, n_tiles)
    return _combine(ys, pos3, x1, info, mod, seq, tb)


def _qk_norm_rows(qn_g, kn_g):
    zeros = jnp.zeros((DIFF_DH,), F32)
    feat = jnp.asarray(np.arange(DIFF_DH) < N_FEAT, F32)
    q_gain = jnp.tile(qn_g, 2 * DIFF_HEADS) * (DIFF_DH ** -0.5 * math.log2(math.e))
    k_gain = jnp.tile(jnp.concatenate([kn_g, zeros]), DIFF_HEADS)
    k_add = jnp.tile(jnp.concatenate([zeros, feat]), DIFF_HEADS)
    return jnp.stack([jnp.stack([q_gain, jnp.zeros_like(q_gain)]),
                      jnp.stack([k_gain, k_add]), jnp.stack([k_gain, k_add])])


def _prep_w_in(w_in_l):
    w = w_in_l
    c = np.cumsum([0, 256, 256, 512, 512, 16, 16, 512, 512, 512, 1024, 1024])
    seg = lambda i: w[:, c[i]:c[i + 1]]
    gq, gk, gv, gg, lrf, lrb, dq, dk, dv, ga, gb = (seg(i) for i in range(11))
    dk_aug = jnp.pad(dk.reshape(-1, 2 * DIFF_HEADS, 1, DIFF_DH), ((0, 0), (0, 0), (0, 1), (0, 0)))
    dk_aug = dk_aug.reshape(-1, 4 * DIFF_HEADS * DIFF_DH)
    main = jnp.concatenate([gq, gk, gv, ga, gb, dk_aug, gg, dq, dv], axis=1).astype(BF16)
    lr = jnp.concatenate([lrf, lrb, jnp.zeros((w.shape[0], LANES - 2 * GLA_RANK), w.dtype)], axis=1)
    return main, lr.astype(BF16)


def kernel(x, c, w_ada, b_ada, norm1_g, norm2_g, w_in, gla_w_up, gla_b_up, gla_norm_g, diff_qnorm_g,
           diff_knorm_g, diff_lambda, diff_subnorm_g, rel_bias, w_branch_a, w_branch_b, w_out,
           w_router_group, b_router_group, w_router_expert, b_router_expert, w_expert_in, w_expert_out):
    bsz, seq, d = x.shape
    t = bsz * seq
    depth = w_ada.shape[0]
    mod_all = _ada(c, w_ada, b_ada).reshape(depth, bsz, 6, d)

    tq = _attn_tile(seq)
    tiles = _bias_tables(rel_bias, tq, tq)
    grp = np.arange(PROJ_TILE) // DIFF_DH
    gmat = jnp.asarray((grp[:, None] == grp[None, :]).astype(np.float32) / DIFF_DH, dtype=BF16)

    xf = x.reshape(t, d)
    for l in range(depth):
        lam_init = 0.8 - 0.6 * math.exp(-0.3 * l)
        mod = mod_all[l]
        w_main, w_lr = _prep_w_in(w_in[l])
        qkg = _qk_norm_rows(diff_qnorm_g[l], diff_knorm_g[l])
        proj, lr, vt = _inproj(xf, mod, norm1_g[l].reshape(1, d), w_main, w_lr, gmat, qkg, seq)
        o_f, o_b = _gla(proj, lr, gla_w_up[l], gla_b_up[l].reshape(2, 1, -1), bsz, seq)
        fast, feat = _softmax_features(rel_bias, diff_qnorm_g[l], diff_knorm_g[l])
        o_b2 = _attn(proj, vt, tiles, fast, feat, diff_lambda[l], diff_subnorm_g[l].reshape(1, -1),
                     bsz, seq, lam_init)
        w_r = jnp.concatenate([w_router_group[l], w_router_expert[l],
                               jnp.zeros((d, LANES - N_GROUPS - N_EXPERTS), F32)], axis=1)
        w_r_hi = w_r.astype(BF16)
        w_r = jnp.stack([w_r_hi, (w_r - w_r_hi.astype(F32)).astype(BF16)])
        b_r = jnp.concatenate([b_router_group[l], b_router_expert[l],
                               jnp.zeros((LANES - N_GROUPS - N_EXPERTS,), F32)]).reshape(1, LANES)
        x1, h2, info, counts = _mix(xf, o_f, o_b, proj, o_b2, mod, gla_norm_g[l].reshape(1, -1),
                                    norm2_g[l].reshape(1, d), w_branch_a[l].astype(BF16),
                                    w_branch_b[l].astype(BF16), w_out[l].astype(BF16), w_r, b_r, seq)
        xf = _moe(h2, x1, info, counts, mod, w_expert_in.reshape(-1, d, 2 * D_EXPERT),
                  w_expert_out.reshape(-1, D_EXPERT, d), l, seq)
    return xf.reshape(bsz, seq, d)
```

```python
import functools
import math

import numpy as np
import jax
import jax.numpy as jnp
from jax import lax
from jax.experimental import pallas as pl
from jax.experimental.pallas import tpu as pltpu

F32 = jnp.float32
BF16 = jnp.bfloat16
HIGHEST = lax.Precision.HIGHEST

D_MODEL = 1024
DEPTH = 2
GLA_HEADS, GLA_DK, GLA_DV, GLA_RANK, GLA_TAU, GLA_CHUNK = 4, 64, 128, 16, 16.0, 64
DIFF_HEADS, DIFF_DH, DIFF_DV = 4, 64, 128
REL_BUCKETS, REL_MAX_DIST = 32, 128
N_GROUPS, EXPERTS_PER_GROUP = 4, 8
N_EXPERTS = N_GROUPS * EXPERTS_PER_GROUP
D_EXPERT = D_MODEL // 2
RMS_EPS = 1e-6
LANES = 128
VMEM_LIMIT = 52 * 1024 * 1024
EXPERT_TILE = 256
DMA_ISSUE_UNROLL = 8

COL_GQK, COL_GV, COL_GA, COL_GB, COL_DKA, COL_GG, COL_DQ, COL_DV = 0, 4, 8, 16, 24, 32, 36, 40
N_PROJ = 44 * LANES
PROJ_TILE = 512
N_FEAT = 3
SOFTMAX_SAFE_RANGE = 60.0
VALUE_MATMUL_TASKS = 4


def _cparams(sem, vmem=VMEM_LIMIT):
    return pltpu.CompilerParams(dimension_semantics=sem, vmem_limit_bytes=vmem)


def _ada_kernel(c_ref, w_ref, b_ref, o_ref):
    c = c_ref[...]
    ca = c * jax.nn.sigmoid(c)
    o_ref[0] = jnp.dot(ca, w_ref[0], precision=HIGHEST, preferred_element_type=F32) + b_ref[0]


def _ada(c, w_ada, b_ada):
    depth, d, n = w_ada.shape
    bsz = c.shape[0]
    tn = 1536
    return pl.pallas_call(
        _ada_kernel,
        grid=(depth, n // tn),
        in_specs=[pl.BlockSpec((bsz, d), lambda l, j: (0, 0)),
                  pl.BlockSpec((1, d, tn), lambda l, j: (l, 0, j)),
                  pl.BlockSpec((1, 1, tn), lambda l, j: (l, 0, j))],
        out_specs=pl.BlockSpec((1, bsz, tn), lambda l, j: (l, 0, j)),
        out_shape=jax.ShapeDtypeStruct((depth, bsz, n), F32),
        compiler_params=_cparams(("parallel", "parallel")),
        name="ada",
    )(c, w_ada, b_ada.reshape(depth, 1, n))


def _inproj_kernel(x_ref, mod_ref, g_ref, w_ref, wlr_ref, gmat_ref, qkg_ref, proj_ref, lr_ref, vt_ref):
    x = x_ref[...]
    ms = jnp.mean(x * x, axis=-1, keepdims=True)
    h = x * lax.rsqrt(ms + RMS_EPS) * g_ref[...]
    h = h * (1.0 + mod_ref[0, 1:2, :]) + mod_ref[0, 0:1, :]
    hb = h.astype(BF16)
    w = PROJ_TILE
    qk_tiles = {COL_DQ * LANES // w: 0, COL_DKA * LANES // w: 1, COL_DKA * LANES // w + 1: 2}
    for j in range(N_PROJ // w):
        acc = jnp.dot(hb, w_ref[:, j * w:(j + 1) * w], preferred_element_type=F32)
        if j in qk_tiles:
            ms2 = jnp.dot((acc * acc).astype(BF16), gmat_ref[...], preferred_element_type=F32)
            acc = acc * lax.rsqrt(ms2 + RMS_EPS) * qkg_ref[qk_tiles[j], 0:1, :] + qkg_ref[qk_tiles[j], 1:2, :]
        proj_ref[:, j * w:(j + 1) * w] = acc.astype(BF16)
        if j == COL_DV * LANES // w:
            vt_ref[0] = acc.T.astype(BF16)
    lr_ref[...] = jnp.dot(hb, wlr_ref[...], preferred_element_type=F32)


def _inproj(x, mod, g, w, wlr, gmat, qkg, seq):
    t, d = x.shape
    tm = min(256, seq)
    tpb = seq // tm
    return pl.pallas_call(
        _inproj_kernel,
        grid=(t // tm,),
        in_specs=[pl.BlockSpec((tm, d), lambda i: (i, 0)),
                  pl.BlockSpec((1, 6, d), lambda i: (i // tpb, 0, 0)),
                  pl.BlockSpec((1, d), lambda i: (0, 0)),
                  pl.BlockSpec((d, N_PROJ), lambda i: (0, 0)),
                  pl.BlockSpec((d, LANES), lambda i: (0, 0)),
                  pl.BlockSpec((PROJ_TILE, PROJ_TILE), lambda i: (0, 0)),
                  pl.BlockSpec((3, 2, PROJ_TILE), lambda i: (0, 0, 0))],
        out_specs=[pl.BlockSpec((tm, N_PROJ), lambda i: (i, 0)),
                   pl.BlockSpec((tm, LANES), lambda i: (i, 0)),
                   pl.BlockSpec((1, DIFF_HEADS * DIFF_DV, tm), lambda i: (i, 0, 0))],
        out_shape=[jax.ShapeDtypeStruct((t, N_PROJ), BF16),
                   jax.ShapeDtypeStruct((t, LANES), F32),
                   jax.ShapeDtypeStruct((t // tm, DIFF_HEADS * DIFF_DV, tm), BF16)],
        compiler_params=_cparams(("parallel",)),
        name="inproj",
    )(x, mod, g, w, wlr, gmat, qkg)


class _GlaDirection:
    def __init__(self, qk_ref, v_ref, lr, wup, bup, o_ref, s_ref, oi_ref, kv_ref, n_chunks, reverse):
        self.qk_ref, self.v_ref, self.lr, self.wup, self.bup = qk_ref, v_ref, lr, wup, bup
        self.o_ref, self.s_ref, self.oi_ref, self.kv_ref = o_ref, s_ref, oi_ref, kv_ref
        self.n_chunks, self.reverse = n_chunks, reverse
        c = GLA_CHUNK
        row = lax.broadcasted_iota(jnp.int32, (c, c), 0)
        col = lax.broadcasted_iota(jnp.int32, (c, c), 1)
        if reverse:
            self.smat, self.mask, self.tot_row = (col >= row).astype(BF16), col > row, 0
        else:
            self.smat, self.mask, self.tot_row = (col <= row).astype(BF16), col <= row, c - 1

    def _rows(self, n):
        return slice(n * GLA_CHUNK, (n + 1) * GLA_CHUNK)

    def _head(self, h, width):
        return slice(h * width, (h + 1) * width)

    def cumulative_decay(self):
        z = jnp.dot(self.lr, self.wup, precision=HIGHEST, preferred_element_type=F32) + self.bup
        lg = (jnp.minimum(z, 0.0) - jnp.log(1.0 + jnp.exp(-jnp.abs(z)))) * (1.0 / GLA_TAU)
        lg_hi = lg.astype(BF16)
        lg_lo = (lg - lg_hi.astype(F32)).astype(BF16)
        self.cums = [jnp.dot(self.smat, lg_hi[self._rows(n)], preferred_element_type=F32)
                     + jnp.dot(self.smat, lg_lo[self._rows(n)], preferred_element_type=F32)
                     for n in range(self.n_chunks)]

    def scale_operands(self):
        hd = GLA_HEADS * GLA_DK
        self.decs, self.q_dec, self.k_inv, self.k_end = [], [], [], []
        for n, cum in enumerate(self.cums):
            dec = jnp.exp(cum[self.tot_row:self.tot_row + 1, :])
            qk = self.qk_ref[self._rows(n), :].astype(F32)
            k_inv = qk[:, hd:] * jnp.exp(-cum)
            self.decs.append(dec)
            self.q_dec.append((qk[:, :hd] * jnp.exp(cum) * (GLA_DK ** -0.5)).astype(BF16))
            self.k_end.append((k_inv * dec).astype(BF16))
            self.k_inv.append(k_inv.astype(BF16))

    def scores_and_outer_products(self):
        self.scores = []
        for n in range(self.n_chunks):
            v = self.v_ref[self._rows(n), :]
            for h in range(GLA_HEADS):
                sl = self._head(h, GLA_DK)
                self.scores.append(lax.dot_general(self.q_dec[n][:, sl], self.k_inv[n][:, sl],
                                                   (((1,), (1,)), ((), ())), preferred_element_type=F32))
                self.kv_ref[n, h] = lax.dot_general(v[:, self._head(h, GLA_DV)], self.k_end[n][:, sl],
                                                    (((0,), (0,)), ((), ())), preferred_element_type=F32)

    def intra_chunk(self):
        for n in range(self.n_chunks):
            v = self.v_ref[self._rows(n), :]
            for h in range(GLA_HEADS):
                a = jnp.where(self.mask, self.scores[n * GLA_HEADS + h], 0.0).astype(BF16)
                self.oi_ref[self._rows(n), self._head(h, GLA_DV)] = jnp.dot(
                    a, v[:, self._head(h, GLA_DV)], preferred_element_type=F32)

    def inter_chunk(self):
        states = [self.s_ref[h] for h in range(GLA_HEADS)]
        order = range(self.n_chunks - 1, -1, -1) if self.reverse else range(self.n_chunks)
        for n in order:
            for h in range(GLA_HEADS):
                sl = self._head(h, GLA_DK)
                o = self.oi_ref[self._rows(n), self._head(h, GLA_DV)] + lax.dot_general(
                    self.q_dec[n][:, sl], states[h].astype(BF16), (((1,), (1,)), ((), ())),
                    preferred_element_type=F32)
                self.o_ref[self._rows(n), self._head(h, GLA_DV)] = o.astype(self.o_ref.dtype)
                states[h] = states[h] * self.decs[n][:, sl] + self.kv_ref[n, h]
        for h in range(GLA_HEADS):
            self.s_ref[h] = states[h]


def _gla_kernel(qkf_ref, vf_ref, lrf_ref, qkb_ref, vb_ref, lrb_ref, wup_ref, bup_ref,
                of_ref, ob_ref, sf_ref, sb_ref, oi_ref, kv_ref, *, n_chunks):
    @pl.when(pl.program_id(1) == 0)
    def _():
        sf_ref[...] = jnp.zeros_like(sf_ref)
        sb_ref[...] = jnp.zeros_like(sb_ref)

    fwd = _GlaDirection(qkf_ref, vf_ref, lrf_ref[:, 0:GLA_RANK], wup_ref[0], bup_ref[0], of_ref, sf_ref,
                        oi_ref.at[0], kv_ref.at[0], n_chunks, False)
    bwd = _GlaDirection(qkb_ref, vb_ref, lrb_ref[:, GLA_RANK:2 * GLA_RANK], wup_ref[1], bup_ref[1], ob_ref,
                        sb_ref, oi_ref.at[1], kv_ref.at[1], n_chunks, True)
    for phase in ("cumulative_decay", "scale_operands", "scores_and_outer_products", "intra_chunk",
                  "inter_chunk"):
        getattr(fwd, phase)()
        getattr(bwd, phase)()


def _gla(proj, lr, wup, bup, bsz, seq):
    t = proj.shape[0]
    lb = min(512, seq)
    nblk = seq // lb
    vw = GLA_HEADS * GLA_DV
    fwd = lambda b, i: b * nblk + i
    bwd = lambda b, i: b * nblk + nblk - 1 - i
    return pl.pallas_call(
        functools.partial(_gla_kernel, n_chunks=lb // GLA_CHUNK),
        grid=(bsz, nblk),
        in_specs=[pl.BlockSpec((lb, 512), lambda b, i: (fwd(b, i), COL_GQK // 4)),
                  pl.BlockSpec((lb, 512), lambda b, i: (fwd(b, i), COL_GV // 4)),
                  pl.BlockSpec((lb, LANES), lambda b, i: (fwd(b, i), 0)),
                  pl.BlockSpec((lb, 512), lambda b, i: (bwd(b, i), COL_GQK // 4)),
                  pl.BlockSpec((lb, 512), lambda b, i: (bwd(b, i), COL_GV // 4)),
                  pl.BlockSpec((lb, LANES), lambda b, i: (bwd(b, i), 0)),
                  pl.BlockSpec((2, GLA_RANK, GLA_HEADS * GLA_DK), lambda b, i: (0, 0, 0)),
                  pl.BlockSpec((2, 1, GLA_HEADS * GLA_DK), lambda b, i: (0, 0, 0))],
        out_specs=[pl.BlockSpec((lb, vw), lambda b, i: (fwd(b, i), 0)),
                   pl.BlockSpec((lb, vw), lambda b, i: (bwd(b, i), 0))],
        out_shape=[jax.ShapeDtypeStruct((t, vw), BF16), jax.ShapeDtypeStruct((t, vw), BF16)],
        scratch_shapes=[pltpu.VMEM((GLA_HEADS, GLA_DV, GLA_DK), F32),
                        pltpu.VMEM((GLA_HEADS, GLA_DV, GLA_DK), F32),
                        pltpu.VMEM((2, lb, vw), F32),
                        pltpu.VMEM((2, lb // GLA_CHUNK, GLA_HEADS, GLA_DV, GLA_DK), F32)],
        compiler_params=_cparams(("parallel", "arbitrary")),
        name="gla",
    )(proj, proj, lr, proj, proj, lr, wup, bup)


def _t5_bucket_np(rel):
    nb = REL_BUCKETS // 2
    max_exact = nb // 2
    ret = np.where(rel > 0, nb, 0)
    n = np.abs(rel)
    nf = np.maximum(n, 1).astype(np.float64)
    large = max_exact + (np.log(nf / max_exact) / math.log(REL_MAX_DIST / max_exact)
                         * (nb - max_exact)).astype(np.int64)
    large = np.minimum(large, nb - 1)
    return ret + np.where(n < max_exact, n, large)


def _attn_kernel(fast_ref, q_ref, k_ref, vt_ref, tile_ref, feat_ref, lam_ref, sg_ref, o_ref,
                 m_ref, l_ref, acc_ref, qa_ref, *, tq, tk, nk, lam_init):
    qi = pl.program_id(2)
    q = q_ref[...]
    n_sub = tk // vt_ref.shape[-1]
    ts = tk // n_sub
    tasks = [(u, c) for u in range(n_sub) for c in range(2)]
    far_tiles = (tile_ref.shape[1] - 1) // 2

    band, left, right = 0, 1, 2
    q_t = q.astype(F32).T
    frow = lax.broadcasted_iota(jnp.int32, (DIFF_DH, tq), 0)
    for var in (band, left, right):
        feat = jnp.zeros((DIFF_DH, tq), F32)
        for n in range(N_FEAT):
            feat = jnp.where(frow == n, feat_ref[pl.program_id(1), var * N_FEAT + n], feat)
        for c in range(2):
            qa_ref[var, c] = jnp.concatenate([q_t[c * DIFF_DH:(c + 1) * DIFF_DH], feat], axis=0).astype(BF16)

    def logits(kb, var, u, c):
        r0 = pl.multiple_of(kb * tk + u * ts, ts)
        k = k_ref[pl.ds(r0, ts), c * LANES:(c + 1) * LANES]
        return jnp.dot(k, qa_ref[var, c], preferred_element_type=F32)

    def bias_tile(kb):
        return tile_ref.at[0, jnp.clip(kb - qi, -far_tiles, far_tiles) + far_tiles]

    def pipelined(blocks, step):
        work = [(kb, var, bias, u, c) for kb, var, bias in blocks for u, c in tasks]
        s_next = logits(work[0][0], work[0][1], work[0][3], work[0][4])
        for t, (kb, var, bias, u, c) in enumerate(work):
            s = s_next
            if t + 1 < len(work):
                nxt = work[t + 1]
                s_next = logits(nxt[0], nxt[1], nxt[3], nxt[4])
            if bias is not None:
                s = s + bias[u * ts:(u + 1) * ts, :]
            step(s, c, vt_ref[kb * n_sub + u])

    pending = {0: [], 1: []}

    def flush_values(c):
        if pending[c]:
            parts, pending[c] = pending[c], []
            acc_ref[c] = acc_ref[c] + jnp.dot(jnp.concatenate([v for v, _ in parts], axis=1),
                                              jnp.concatenate([w for _, w in parts], axis=0),
                                              preferred_element_type=F32)

    def bounded_step(s, c, vt):
        p = jnp.exp2(s)
        l_ref[c] = l_ref[c] + jnp.sum(p, axis=0, keepdims=True)
        pending[c].append((vt, p.astype(BF16)))
        if len(pending[c]) == VALUE_MATMUL_TASKS:
            flush_values(c)

    def online_step(s, c, vt):
        m_old = m_ref[c]
        m_new = jnp.maximum(m_old, jnp.max(s, axis=0, keepdims=True))
        p = jnp.exp2(s - m_new)
        alpha = jnp.exp2(m_old - m_new)
        l_ref[c] = alpha * l_ref[c] + jnp.sum(p, axis=0, keepdims=True)
        acc_ref[c] = alpha * acc_ref[c] + jnp.dot(vt, p.astype(BF16), preferred_element_type=F32)
        m_ref[c] = m_new

    m_ref[...] = jnp.full_like(m_ref, -1e30)
    l_ref[...] = jnp.zeros_like(l_ref)
    acc_ref[...] = jnp.zeros_like(acc_ref)

    @pl.when(fast_ref[0] == 1)
    def _():
        offsets = list(range(nk)) if nk < 3 else [-1, 0, 1] + list(range(2, nk - 1))
        blocks = []
        for d in offsets:
            kb = qi + d
            kb = jnp.where(kb >= nk, kb - nk, jnp.where(kb < 0, kb + nk, kb))
            if 2 <= d <= nk - 2:
                blocks.append((kb, jnp.where(kb > qi, right, left), None))
            else:
                blocks.append((kb, band, bias_tile(kb)))
        pipelined(blocks, bounded_step)
        flush_values(0)
        flush_values(1)

    @pl.when(fast_ref[0] == 0)
    def _():
        group = 2 if nk % 2 == 0 else 1

        def body(i, carry):
            pipelined([(group * i + g, band, bias_tile(group * i + g)) for g in range(group)], online_step)
            return carry

        lax.fori_loop(0, nk // group, body, 0)

    lv = lam_ref[...]
    lam = (jnp.exp(jnp.sum(lv[0:1] * lv[1:2], axis=-1, keepdims=True))
           - jnp.exp(jnp.sum(lv[2:3] * lv[3:4], axis=-1, keepdims=True)) + lam_init)
    out_t = acc_ref[0] * (1.0 / l_ref[0]) - acc_ref[1] * (lam / l_ref[1])
    out = out_t.T
    ms = jnp.mean(out * out, axis=-1, keepdims=True)
    out = out * lax.rsqrt(ms + RMS_EPS) * sg_ref[...] * (1.0 - lam_init)
    o_ref[...] = out.astype(o_ref.dtype)


def _attn_tile(seq):
    return min(512, seq)


def _attn(proj, vt, tiles, fast, feat, lam_vecs, subn_g, bsz, seq, lam_init):
    t = proj.shape[0]
    tq = tk = _attn_tile(seq)
    nq = seq // tq
    nk = seq // tk
    vb = vt.shape[-1]
    return pl.pallas_call(
        functools.partial(_attn_kernel, tq=tq, tk=tk, nk=nk, lam_init=lam_init),
        grid=(bsz, DIFF_HEADS, nq),
        in_specs=[pl.BlockSpec(memory_space=pltpu.SMEM),
                  pl.BlockSpec((tq, LANES), lambda b, h, i: (b * nq + i, COL_DQ + h)),
                  pl.BlockSpec((seq, 2 * LANES), lambda b, h, i: (b, COL_DKA // 2 + h)),
                  pl.BlockSpec((seq // vb, DIFF_DV, vb), lambda b, h, i: (b, h, 0)),
                  pl.BlockSpec((1, tiles.shape[1], tk, tq), lambda b, h, i: (h, 0, 0, 0)),
                  pl.BlockSpec(memory_space=pltpu.SMEM),
                  pl.BlockSpec((4, DIFF_DH), lambda b, h, i: (0, 0)),
                  pl.BlockSpec((1, DIFF_DV), lambda b, h, i: (0, 0))],
        out_specs=pl.BlockSpec((tq, DIFF_DV), lambda b, h, i: (b * nq + i, h)),
        out_shape=jax.ShapeDtypeStruct((t, DIFF_HEADS * DIFF_DV), BF16),
        scratch_shapes=[pltpu.VMEM((2, 1, tq), F32), pltpu.VMEM((2, 1, tq), F32),
                        pltpu.VMEM((2, DIFF_DV, tq), F32), pltpu.VMEM((3, 2, LANES, tq), BF16)],
        compiler_params=_cparams(("parallel", "parallel", "arbitrary")),
        name="attn",
    )(fast, proj, proj, vt, tiles, feat, lam_vecs, subn_g)


def _bias_tables(rel_bias, tq, tk):
    n = tk + tq - 1
    log2e = math.log2(math.e)
    tiles = []
    for d in (-2, -1, 0, 1, 2):
        m = np.arange(n)
        rel = d * tk - np.where(m < tq, m, m - n)
        onehot = np.eye(REL_BUCKETS, dtype=np.float32)[_t5_bucket_np(rel)]
        y = jnp.dot(jnp.asarray(onehot), rel_bias.astype(F32), precision=HIGHEST).T * log2e
        tiles.append(jnp.tile(y, (1, tk))[:, :tk * (n - 1)].reshape(-1, tk, n - 1)[:, :, :tq])
    return jnp.stack(tiles, axis=1)


def _softmax_features(rel_bias, qn_g, kn_g):
    log2e = math.log2(math.e)
    bound = (1.02 * DIFF_DH ** 0.5 * log2e * jnp.max(jnp.abs(qn_g)) * jnp.max(jnp.abs(kn_g))
             + log2e * jnp.max(jnp.abs(rel_bias)))
    fast = bound <= SOFTMAX_SAFE_RANGE
    nb = REL_BUCKETS // 2
    far = jnp.stack([rel_bias[nb - 1], rel_bias[2 * nb - 1]], axis=-1).astype(F32) * log2e
    c_hi = far.astype(BF16).astype(F32)
    c_lo = (far - c_hi).astype(BF16).astype(F32)
    zeros = jnp.zeros_like(c_hi[:, 0])
    nbv = jnp.broadcast_to(-bound.astype(BF16).astype(F32), zeros.shape)
    rows = jnp.stack([jnp.stack([nbv, zeros, zeros], -1),
                      jnp.stack([nbv, c_hi[:, 0], c_lo[:, 0]], -1),
                      jnp.stack([nbv, c_hi[:, 1], c_lo[:, 1]], -1)], axis=1)
    feat = jnp.where(fast, rows, 0.0).reshape(DIFF_HEADS, 3 * N_FEAT)
    return fast.astype(jnp.int32).reshape(1), feat


def _mix_kernel(x_ref, of_ref, ob_ref, gg_ref, ob2_ref, ga_ref, gb_ref, mod_ref, gn_ref, n2_ref,
                wa_ref, wb_ref, wo_ref, wr_ref, br_ref,
                x1_ref, h2_ref, info_ref, cnt_ref, base_ref, *, tm):
    @pl.when(pl.program_id(0) == 0)
    def _():
        base_ref[...] = jnp.zeros_like(base_ref)

    osum = of_ref[...].astype(F32) + ob_ref[...].astype(F32)
    parts = []
    for h in range(GLA_HEADS):
        sl = osum[:, h * GLA_DV:(h + 1) * GLA_DV]
        ms = jnp.mean(sl * sl, axis=-1, keepdims=True)
        parts.append(sl * lax.rsqrt(ms + RMS_EPS) * gn_ref[...])
    gg = gg_ref[...].astype(F32)
    o_a = jnp.concatenate(parts, axis=-1) * (gg * jax.nn.sigmoid(gg))
    y_a = jnp.dot(o_a.astype(BF16), wa_ref[...], preferred_element_type=F32)
    y_b = jnp.dot(ob2_ref[...], wb_ref[...], preferred_element_type=F32)
    merged = (jax.nn.sigmoid(ga_ref[...].astype(F32)) * y_a
              + jax.nn.sigmoid(gb_ref[...].astype(F32)) * y_b)
    y = jnp.dot(merged.astype(BF16), wo_ref[...], preferred_element_type=F32)
    x1 = x_ref[...] + mod_ref[0, 2:3, :] * y
    x1_ref[...] = x1

    ms = jnp.mean(x1 * x1, axis=-1, keepdims=True)
    h2 = x1 * lax.rsqrt(ms + RMS_EPS) * n2_ref[...]
    h2 = h2 * (1.0 + mod_ref[0, 4:5, :]) + mod_ref[0, 3:4, :]
    h2_ref[...] = h2

    h2_hi = h2.astype(BF16)
    h2_lo = (h2 - h2_hi.astype(F32)).astype(BF16)
    logits = (jnp.dot(h2_hi, wr_ref[0], preferred_element_type=F32)
              + jnp.dot(h2_lo, wr_ref[0], preferred_element_type=F32)
              + jnp.dot(h2_hi, wr_ref[1], preferred_element_type=F32)) + br_ref[...]
    lane_i = lax.broadcasted_iota(jnp.int32, logits.shape, 1)
    lane = lane_i.astype(F32)
    neg = jnp.float32(-3e38)
    big = jnp.float32(1 << 20)
    is_g = lane < N_GROUPS
    gl = jnp.where(is_g, logits, neg)
    gmax = jnp.max(gl, axis=-1, keepdims=True)
    grp = jnp.min(jnp.where(jnp.logical_and(is_g, gl == gmax), lane, big), axis=-1, keepdims=True)
    p_grp = 1.0 / jnp.sum(jnp.where(is_g, jnp.exp(gl - gmax), 0.0), axis=-1, keepdims=True)
    lo = N_GROUPS + grp * EXPERTS_PER_GROUP
    in_grp = jnp.logical_and(lane >= lo, lane < lo + EXPERTS_PER_GROUP)
    el = jnp.where(in_grp, logits, neg)
    e1 = jnp.max(el, axis=-1, keepdims=True)
    i1 = jnp.min(jnp.where(jnp.logical_and(in_grp, el == e1), lane, big), axis=-1, keepdims=True)
    rest = jnp.logical_and(in_grp, lane != i1)
    el2 = jnp.where(rest, logits, neg)
    e2 = jnp.max(el2, axis=-1, keepdims=True)
    i2 = jnp.min(jnp.where(jnp.logical_and(rest, el2 == e2), lane, big), axis=-1, keepdims=True)
    r = jnp.exp(e2 - e1)
    w1 = p_grp / (1.0 + r)
    w2 = w1 * r

    oh1 = (lane == i1).astype(F32)
    oh2 = (lane == i2).astype(F32)
    rr = lax.broadcasted_iota(jnp.int32, (tm, tm), 0)
    cc = lax.broadcasted_iota(jnp.int32, (tm, tm), 1)
    tril = (cc < rr).astype(BF16)
    before = jnp.dot(tril, (oh1 + oh2).astype(BF16), preferred_element_type=F32) + base_ref[...]
    rank1 = jnp.sum(oh1 * before, axis=-1, keepdims=True)
    rank2 = jnp.sum(oh2 * before, axis=-1, keepdims=True)
    base_new = base_ref[...] + jnp.sum(oh1 + oh2, axis=0, keepdims=True)
    base_ref[...] = base_new
    cnt_ref[...] = jnp.broadcast_to(base_new, cnt_ref.shape)

    ex1 = i1 - N_GROUPS
    ex2 = i2 - N_GROUPS
    info = jnp.zeros(logits.shape, F32)
    for idx, val in enumerate((ex1, ex2, rank1, rank2, w1, w2)):
        info = jnp.where(lane_i == idx, val, info)
    info_ref[...] = info


def _mix(x, of, ob, proj, ob2, mod, gn, n2, wa, wb, wo, wr, br, seq):
    t, d = x.shape
    tm = min(512, seq)
    tpb = seq // tm
    row = lambda i: (i, 0)
    const = lambda i: (0, 0)
    return pl.pallas_call(
        functools.partial(_mix_kernel, tm=tm),
        grid=(t // tm,),
        in_specs=[pl.BlockSpec((tm, d), row),
                  pl.BlockSpec((tm, 512), row),
                  pl.BlockSpec((tm, 512), row),
                  pl.BlockSpec((tm, 512), lambda i: (i, COL_GG // 4)),
                  pl.BlockSpec((tm, 512), row),
                  pl.BlockSpec((tm, 1024), lambda i: (i, COL_GA // 8)),
                  pl.BlockSpec((tm, 1024), lambda i: (i, COL_GB // 8)),
                  pl.BlockSpec((1, 6, d), lambda i: (i // tpb, 0, 0)),
                  pl.BlockSpec((1, GLA_DV), const),
                  pl.BlockSpec((1, d), const),
                  pl.BlockSpec((512, d), const),
                  pl.BlockSpec((512, d), const),
                  pl.BlockSpec((d, d), const),
                  pl.BlockSpec((2, d, LANES), lambda i: (0, 0, 0)),
                  pl.BlockSpec((1, LANES), const)],
        out_specs=[pl.BlockSpec((tm, d), row),
                   pl.BlockSpec((tm, d), row),
                   pl.BlockSpec((tm, LANES), row),
                   pl.BlockSpec((8, LANES), const)],
        out_shape=[jax.ShapeDtypeStruct((t, d), F32),
                   jax.ShapeDtypeStruct((t, d), F32),
                   jax.ShapeDtypeStruct((t, LANES), F32),
                   jax.ShapeDtypeStruct((8, LANES), F32)],
        scratch_shapes=[pltpu.VMEM((1, LANES), F32)],
        compiler_params=_cparams(("arbitrary",)),
        name="mix",
    )(x, of, ob, proj, ob2, proj, proj, mod, gn, n2, wa, wb, wo, wr, br)


def _dispatch_kernel(pos_ref, h_ref, zeros_ref, xs_ref, sem_ref, *, tb):
    del zeros_ref

    def row_copy(r, k):
        return pltpu.make_async_copy(h_ref.at[pl.ds(r, 1), :],
                                     xs_ref.at[pl.ds(pos_ref[0, 0, 2 * r + k], 1), :], sem_ref)

    def start(r, carry):
        row_copy(r, 0).start(priority=0)
        row_copy(r, 1).start(priority=1)
        return carry

    def wait(r, carry):
        row_copy(r, 0).wait()
        row_copy(r, 1).wait()
        return carry

    lax.fori_loop(0, tb, start, 0, unroll=DMA_ISSUE_UNROLL)
    lax.fori_loop(0, tb, wait, 0, unroll=True)


def _dispatch(h2, pos3, n_rows, tb):
    t, d = h2.shape
    return pl.pallas_call(
        functools.partial(_dispatch_kernel, tb=tb),
        grid_spec=pltpu.PrefetchScalarGridSpec(
            num_scalar_prefetch=0,
            grid=(t // tb,),
            in_specs=[pl.BlockSpec((1, 1, 2 * tb), lambda i: (i, 0, 0), memory_space=pltpu.SMEM),
                      pl.BlockSpec((tb, d), lambda i: (i, 0)),
                      pl.BlockSpec(memory_space=pl.ANY)],
            out_specs=pl.BlockSpec(memory_space=pl.ANY),
            scratch_shapes=[pltpu.SemaphoreType.DMA(())]),
        out_shape=jax.ShapeDtypeStruct((n_rows, d), F32),
        input_output_aliases={2: 0},
        compiler_params=_cparams(("arbitrary",)),
        name="dispatch",
    )(pos3, h2, jnp.zeros((n_rows, d), F32))


def _expert_kernel(te_ref, nv_ref, xs_ref, w1_ref, w2_ref, ys_ref, w1b_ref, w2b_ref):
    i = pl.program_id(0)

    @pl.when(i >= nv_ref[0])
    def _():
        ys_ref[...] = jnp.zeros_like(ys_ref)

    new_expert = jnp.logical_or(i == 0, te_ref[i] != te_ref[jnp.maximum(i - 1, 0)])

    @pl.when(jnp.logical_and(i < nv_ref[0], new_expert))
    def _():
        w1b_ref[...] = w1_ref[0].astype(BF16)
        w2b_ref[...] = w2_ref[0].astype(BF16)

    @pl.when(i < nv_ref[0])
    def _():
        xb = xs_ref[...].astype(BF16)
        hu = jnp.dot(xb, w1b_ref[...], preferred_element_type=F32)
        hg = hu[:, :D_EXPERT]
        act = (hg * jax.nn.sigmoid(hg)) * hu[:, D_EXPERT:]
        ys_ref[...] = jnp.dot(act.astype(BF16), w2b_ref[...], preferred_element_type=F32)


def _experts(xs, tile_expert, n_valid, w1, w2, n_tiles):
    d = xs.shape[1]
    blk = lambda i, te, nv: (jnp.maximum(jnp.minimum(i, nv[0] - 1), 0), 0)
    return pl.pallas_call(
        _expert_kernel,
        grid_spec=pltpu.PrefetchScalarGridSpec(
            num_scalar_prefetch=2,
            grid=(n_tiles,),
            in_specs=[pl.BlockSpec((EXPERT_TILE, d), blk),
                      pl.BlockSpec((1, d, 2 * D_EXPERT), lambda i, te, nv: (te[i], 0, 0)),
                      pl.BlockSpec((1, D_EXPERT, d), lambda i, te, nv: (te[i], 0, 0))],
            out_specs=pl.BlockSpec((EXPERT_TILE, d), lambda i, te, nv: (i, 0)),
            scratch_shapes=[pltpu.VMEM((d, 2 * D_EXPERT), BF16), pltpu.VMEM((D_EXPERT, d), BF16)]),
        out_shape=jax.ShapeDtypeStruct((n_tiles * EXPERT_TILE, d), F32),
        compiler_params=_cparams(("arbitrary",)),
        name="experts",
    )(tile_expert, n_valid, xs, w1, w2)


def _combine_kernel(pos_ref, ys_ref, x1_ref, info_ref, mod_ref, o_ref, buf_ref, sem_ref, *, tb):
    def row_copy(r, k):
        return pltpu.make_async_copy(ys_ref.at[pl.ds(pos_ref[0, 0, 2 * r + k], 1), :],
                                     buf_ref.at[k, pl.ds(r, 1), :], sem_ref)

    def start(r, carry):
        row_copy(r, 0).start(priority=0)
        row_copy(r, 1).start(priority=1)
        return carry

    def wait(r, carry):
        row_copy(r, 0).wait()
        row_copy(r, 1).wait()
        return carry

    lax.fori_loop(0, tb, start, 0, unroll=DMA_ISSUE_UNROLL)
    lax.fori_loop(0, tb, wait, 0, unroll=True)
    info = info_ref[...]
    y = buf_ref[0] * info[:, 4:5] + buf_ref[1] * info[:, 5:6]
    o_ref[...] = x1_ref[...] + mod_ref[0, 5:6, :] * y


def _combine(ys, pos3, x1, info, mod, seq, tb):
    t, d = x1.shape
    tpb = seq // tb
    return pl.pallas_call(
        functools.partial(_combine_kernel, tb=tb),
        grid_spec=pltpu.PrefetchScalarGridSpec(
            num_scalar_prefetch=0,
            grid=(t // tb,),
            in_specs=[pl.BlockSpec((1, 1, 2 * tb), lambda i: (i, 0, 0), memory_space=pltpu.SMEM),
                      pl.BlockSpec(memory_space=pl.ANY),
                      pl.BlockSpec((tb, d), lambda i: (i, 0)),
                      pl.BlockSpec((tb, LANES), lambda i: (i, 0)),
                      pl.BlockSpec((1, 6, d), lambda i: (i // tpb, 0, 0))],
            out_specs=pl.BlockSpec((tb, d), lambda i: (i, 0)),
            scratch_shapes=[pltpu.VMEM((2, tb, d), F32), pltpu.SemaphoreType.DMA(())]),
        out_shape=jax.ShapeDtypeStruct((t, d), F32),
        compiler_params=_cparams(("arbitrary",)),
        name="combine",
    )(pos3, ys, x1, info, mod)


def _moe(h2, x1, info, counts, mod, w1, w2, layer, seq):
    t, d = h2.shape
    tb = min(256, seq)
    n_tiles = (2 * t + N_EXPERTS * (EXPERT_TILE - 1)) // EXPERT_TILE
    cnt = counts[0, N_GROUPS:N_GROUPS + N_EXPERTS].astype(jnp.int32)
    padded = ((cnt + EXPERT_TILE - 1) // EXPERT_TILE) * EXPERT_TILE
    ends = jnp.cumsum(padded)
    starts = ends - padded
    n_valid = (ends[-1] // EXPERT_TILE).astype(jnp.int32).reshape(1)
    tile_start = jnp.arange(n_tiles, dtype=jnp.int32) * EXPERT_TILE
    tile_expert = jnp.sum(ends[None, :] <= tile_start[:, None], axis=1).astype(jnp.int32)
    tile_expert = jnp.minimum(tile_expert, tile_expert[jnp.maximum(n_valid[0] - 1, 0)])
    tile_expert = tile_expert + layer * N_EXPERTS
    eid = info[:, 0:2].astype(jnp.int32)
    pos = starts[eid] + info[:, 2:4].astype(jnp.int32)
    pos3 = pos.reshape(t // tb, 1, 2 * tb)
    xs = _dispatch(h2, pos3, n_tiles * EXPERT_TILE, tb)
    ys = _experts(xs, tile_expert, n_valid, w1, w2, n_tiles)
    return _combine(ys, pos3, x1, info, mod, seq, tb)


def _qk_norm_rows(qn_g, kn_g):
    zeros = jnp.zeros((DIFF_DH,), F32)
    feat = jnp.asarray(np.arange(DIFF_DH) < N_FEAT, F32)
    q_gain = jnp.tile(qn_g, 2 * DIFF_HEADS) * (DIFF_DH ** -0.5 * math.log2(math.e))
    k_gain = jnp.tile(jnp.concatenate([kn_g, zeros]), DIFF_HEADS)
    k_add = jnp.tile(jnp.concatenate([zeros, feat]), DIFF_HEADS)
    return jnp.stack([jnp.stack([q_gain, jnp.zeros_like(q_gain)]),
                      jnp.stack([k_gain, k_add]), jnp.stack([k_gain, k_add])])


def _prep_w_in(w_in_l):
    w = w_in_l
    c = np.cumsum([0, 256, 256, 512, 512, 16, 16, 512, 512, 512, 1024, 1024])
    seg = lambda i: w[:, c[i]:c[i + 1]]
    gq, gk, gv, gg, lrf, lrb, dq, dk, dv, ga, gb = (seg(i) for i in range(11))
    dk_aug = jnp.pad(dk.reshape(-1, 2 * DIFF_HEADS, 1, DIFF_DH), ((0, 0), (0, 0), (0, 1), (0, 0)))
    dk_aug = dk_aug.reshape(-1, 4 * DIFF_HEADS * DIFF_DH)
    main = jnp.concatenate([gq, gk, gv, ga, gb, dk_aug, gg, dq, dv], axis=1).astype(BF16)
    lr = jnp.concatenate([lrf, lrb, jnp.zeros((w.shape[0], LANES - 2 * GLA_RANK), w.dtype)], axis=1)
    return main, lr.astype(BF16)


def kernel(x, c, w_ada, b_ada, norm1_g, norm2_g, w_in, gla_w_up, gla_b_up, gla_norm_g, diff_qnorm_g,
           diff_knorm_g, diff_lambda, diff_subnorm_g, rel_bias, w_branch_a, w_branch_b, w_out,
           w_router_group, b_router_group, w_router_expert, b_router_expert, w_expert_in, w_expert_out):
    bsz, seq, d = x.shape
    t = bsz * seq
    depth = w_ada.shape[0]
    mod_all = _ada(c, w_ada, b_ada).reshape(depth, bsz, 6, d)

    tq = _attn_tile(seq)
    tiles = _bias_tables(rel_bias, tq, tq)
    grp = np.arange(PROJ_TILE) // DIFF_DH
    gmat = jnp.asarray((grp[:, None] == grp[None, :]).astype(np.float32) / DIFF_DH, dtype=BF16)

    xf = x.reshape(t, d)
    for l in range(depth):
        lam_init = 0.8 - 0.6 * math.exp(-0.3 * l)
        mod = mod_all[l]
        w_main, w_lr = _prep_w_in(w_in[l])
        qkg = _qk_norm_rows(diff_qnorm_g[l], diff_knorm_g[l])
        proj, lr, vt = _inproj(xf, mod, norm1_g[l].reshape(1, d), w_main, w_lr, gmat, qkg, seq)
        o_f, o_b = _gla(proj, lr, gla_w_up[l], gla_b_up[l].reshape(2, 1, -1), bsz, seq)
        fast, feat = _softmax_features(rel_bias, diff_qnorm_g[l], diff_knorm_g[l])
        o_b2 = _attn(proj, vt, tiles, fast, feat, diff_lambda[l], diff_subnorm_g[l].reshape(1, -1),
                     bsz, seq, lam_init)
        w_r = jnp.concatenate([w_router_group[l], w_router_expert[l],
                               jnp.zeros((d, LANES - N_GROUPS - N_EXPERTS), F32)], axis=1)
        w_r_hi = w_r.astype(BF16)
        w_r = jnp.stack([w_r_hi, (w_r - w_r_hi.astype(F32)).astype(BF16)])
        b_r = jnp.concatenate([b_router_group[l], b_router_expert[l],
                               jnp.zeros((LANES - N_GROUPS - N_EXPERTS,), F32)]).reshape(1, LANES)
        x1, h2, info, counts = _mix(xf, o_f, o_b, proj, o_b2, mod, gla_norm_g[l].reshape(1, -1),
                                    norm2_g[l].reshape(1, d), w_branch_a[l].astype(BF16),
                                    w_branch_b[l].astype(BF16), w_out[l].astype(BF16), w_r, b_r, seq)
        xf = _moe(h2, x1, info, counts, mod, w_expert_in.reshape(-1, d, 2 * D_EXPERT),
                  w_expert_out.reshape(-1, D_EXPERT, d), l, seq)
    return xf.reshape(bsz, seq, d)
```

```python
import functools
import math

import numpy as np
import jax
import jax.numpy as jnp
from jax import lax
from jax.experimental import pallas as pl
from jax.experimental.pallas import tpu as pltpu

F32 = jnp.float32
BF16 = jnp.bfloat16
HIGHEST = lax.Precision.HIGHEST

D_MODEL = 1024
DEPTH = 2
GLA_HEADS, GLA_DK, GLA_DV, GLA_RANK, GLA_TAU, GLA_CHUNK = 4, 64, 128, 16, 16.0, 64
DIFF_HEADS, DIFF_DH, DIFF_DV = 4, 64, 128
REL_BUCKETS, REL_MAX_DIST = 32, 128
N_GROUPS, EXPERTS_PER_GROUP = 4, 8
N_EXPERTS = N_GROUPS * EXPERTS_PER_GROUP
D_EXPERT = D_MODEL // 2
RMS_EPS = 1e-6
LANES = 128
VMEM_LIMIT = 52 * 1024 * 1024
EXPERT_TILE = 256
DMA_ISSUE_UNROLL = 8

COL_GQK, COL_GV, COL_GA, COL_GB, COL_DKA, COL_GG, COL_DQ, COL_DV = 0, 4, 8, 16, 24, 32, 36, 40
N_PROJ = 44 * LANES
PROJ_TILE = 512
N_FEAT = 3
SOFTMAX_SAFE_RANGE = 60.0
VALUE_MATMUL_TASKS = 8


def _cparams(sem, vmem=VMEM_LIMIT):
    return pltpu.CompilerParams(dimension_semantics=sem, vmem_limit_bytes=vmem)


def _pack_rows(x):
    w = x.shape[1] // 2
    bits = lax.bitcast_convert_type(x.astype(BF16).astype(F32), jnp.uint32)
    return bits[:, :w] | (bits[:, w:] >> 16)


def _unpack_rows(p):
    hi = lax.bitcast_convert_type(p & jnp.uint32(0xFFFF0000), F32)
    lo = lax.bitcast_convert_type(p << 16, F32)
    return hi, lo


def _ada_kernel(c_ref, w_ref, b_ref, o_ref):
    c = c_ref[...]
    ca = c * jax.nn.sigmoid(c)
    o_ref[0] = jnp.dot(ca, w_ref[0], precision=HIGHEST, preferred_element_type=F32) + b_ref[0]


def _ada(c, w_ada, b_ada):
    depth, d, n = w_ada.shape
    bsz = c.shape[0]
    tn = 1536
    return pl.pallas_call(
        _ada_kernel,
        grid=(depth, n // tn),
        in_specs=[pl.BlockSpec((bsz, d), lambda l, j: (0, 0)),
                  pl.BlockSpec((1, d, tn), lambda l, j: (l, 0, j)),
                  pl.BlockSpec((1, 1, tn), lambda l, j: (l, 0, j))],
        out_specs=pl.BlockSpec((1, bsz, tn), lambda l, j: (l, 0, j)),
        out_shape=jax.ShapeDtypeStruct((depth, bsz, n), F32),
        compiler_params=_cparams(("parallel", "parallel")),
        name="ada",
    )(c, w_ada, b_ada.reshape(depth, 1, n))


def _inproj_kernel(x_ref, mod_ref, g_ref, w_ref, wlr_ref, gmat_ref, qkg_ref, proj_ref, lr_ref, vt_ref):
    x = x_ref[...]
    ms = jnp.mean(x * x, axis=-1, keepdims=True)
    h = x * lax.rsqrt(ms + RMS_EPS) * g_ref[...]
    h = h * (1.0 + mod_ref[0, 1:2, :]) + mod_ref[0, 0:1, :]
    hb = h.astype(BF16)
    w = PROJ_TILE
    qk_tiles = {COL_DQ * LANES // w: 0, COL_DKA * LANES // w: 1, COL_DKA * LANES // w + 1: 2}
    for j in range(N_PROJ // w):
        acc = jnp.dot(hb, w_ref[:, j * w:(j + 1) * w], preferred_element_type=F32)
        if j in qk_tiles:
            ms2 = jnp.dot((acc * acc).astype(BF16), gmat_ref[...], preferred_element_type=F32)
            acc = acc * lax.rsqrt(ms2 + RMS_EPS) * qkg_ref[qk_tiles[j], 0:1, :] + qkg_ref[qk_tiles[j], 1:2, :]
        proj_ref[:, j * w:(j + 1) * w] = acc.astype(BF16)
        if j == COL_DV * LANES // w:
            vt_ref[0] = acc.T.astype(BF16)
    lr_ref[...] = jnp.dot(hb, wlr_ref[...], preferred_element_type=F32)


def _inproj(x, mod, g, w, wlr, gmat, qkg, seq):
    t, d = x.shape
    tm = min(256, seq)
    tpb = seq // tm
    return pl.pallas_call(
        _inproj_kernel,
        grid=(t // tm,),
        in_specs=[pl.BlockSpec((tm, d), lambda i: (i, 0)),
                  pl.BlockSpec((1, 6, d), lambda i: (i // tpb, 0, 0)),
                  pl.BlockSpec((1, d), lambda i: (0, 0)),
                  pl.BlockSpec((d, N_PROJ), lambda i: (0, 0)),
                  pl.BlockSpec((d, LANES), lambda i: (0, 0)),
                  pl.BlockSpec((PROJ_TILE, PROJ_TILE), lambda i: (0, 0)),
                  pl.BlockSpec((3, 2, PROJ_TILE), lambda i: (0, 0, 0))],
        out_specs=[pl.BlockSpec((tm, N_PROJ), lambda i: (i, 0)),
                   pl.BlockSpec((tm, LANES), lambda i: (i, 0)),
                   pl.BlockSpec((1, DIFF_HEADS * DIFF_DV, tm), lambda i: (i, 0, 0))],
        out_shape=[jax.ShapeDtypeStruct((t, N_PROJ), BF16),
                   jax.ShapeDtypeStruct((t, LANES), F32),
                   jax.ShapeDtypeStruct((t // tm, DIFF_HEADS * DIFF_DV, tm), BF16)],
        compiler_params=_cparams(("parallel",)),
        name="inproj",
    )(x, mod, g, w, wlr, gmat, qkg)


class _GlaDirection:
    def __init__(self, qk_ref, v_ref, lr, wup, bup, o_ref, s_ref, oi_ref, kv_ref, n_chunks, reverse):
        self.qk_ref, self.v_ref, self.lr, self.wup, self.bup = qk_ref, v_ref, lr, wup, bup
        self.o_ref, self.s_ref, self.oi_ref, self.kv_ref = o_ref, s_ref, oi_ref, kv_ref
        self.n_chunks, self.reverse = n_chunks, reverse
        c = GLA_CHUNK
        row = lax.broadcasted_iota(jnp.int32, (c, c), 0)
        col = lax.broadcasted_iota(jnp.int32, (c, c), 1)
        if reverse:
            self.smat, self.mask, self.tot_row = (col >= row).astype(BF16), col > row, 0
        else:
            self.smat, self.mask, self.tot_row = (col <= row).astype(BF16), col <= row, c - 1

    def _rows(self, n):
        return slice(n * GLA_CHUNK, (n + 1) * GLA_CHUNK)

    def _head(self, h, width):
        return slice(h * width, (h + 1) * width)

    def cumulative_decay(self):
        z = jnp.dot(self.lr, self.wup, precision=HIGHEST, preferred_element_type=F32) + self.bup
        lg = (jnp.minimum(z, 0.0) - jnp.log(1.0 + jnp.exp(-jnp.abs(z)))) * (1.0 / GLA_TAU)
        lg_hi = lg.astype(BF16)
        lg_lo = (lg - lg_hi.astype(F32)).astype(BF16)
        self.cums = [jnp.dot(self.smat, lg_hi[self._rows(n)], preferred_element_type=F32)
                     + jnp.dot(self.smat, lg_lo[self._rows(n)], preferred_element_type=F32)
                     for n in range(self.n_chunks)]

    def scale_operands(self):
        hd = GLA_HEADS * GLA_DK
        self.decs, self.q_dec, self.k_inv, self.k_end = [], [], [], []
        for n, cum in enumerate(self.cums):
            dec = jnp.exp(cum[self.tot_row:self.tot_row + 1, :])
            qk = self.qk_ref[self._rows(n), :].astype(F32)
            k_inv = qk[:, hd:] * jnp.exp(-cum)
            self.decs.append(dec)
            self.q_dec.append((qk[:, :hd] * jnp.exp(cum) * (GLA_DK ** -0.5)).astype(BF16))
            self.k_end.append((k_inv * dec).astype(BF16))
            self.k_inv.append(k_inv.astype(BF16))

    def scores_and_outer_products(self):
        self.scores = []
        for n in range(self.n_chunks):
            v = self.v_ref[self._rows(n), :]
            for h in range(GLA_HEADS):
                sl = self._head(h, GLA_DK)
                self.scores.append(lax.dot_general(self.q_dec[n][:, sl], self.k_inv[n][:, sl],
                                                   (((1,), (1,)), ((), ())), preferred_element_type=F32))
                self.kv_ref[n, h] = lax.dot_general(v[:, self._head(h, GLA_DV)], self.k_end[n][:, sl],
                                                    (((0,), (0,)), ((), ())), preferred_element_type=F32)

    def intra_chunk(self):
        for n in range(self.n_chunks):
            v = self.v_ref[self._rows(n), :]
            for h in range(GLA_HEADS):
                a = jnp.where(self.mask, self.scores[n * GLA_HEADS + h], 0.0).astype(BF16)
                self.oi_ref[self._rows(n), self._head(h, GLA_DV)] = jnp.dot(
                    a, v[:, self._head(h, GLA_DV)], preferred_element_type=F32)

    def inter_chunk(self):
        states = [self.s_ref[h] for h in range(GLA_HEADS)]
        order = range(self.n_chunks - 1, -1, -1) if self.reverse else range(self.n_chunks)
        for n in order:
            for h in range(GLA_HEADS):
                sl = self._head(h, GLA_DK)
                o = self.oi_ref[self._rows(n), self._head(h, GLA_DV)] + lax.dot_general(
                    self.q_dec[n][:, sl], states[h].astype(BF16), (((1,), (1,)), ((), ())),
                    preferred_element_type=F32)
                self.o_ref[self._rows(n), self._head(h, GLA_DV)] = o.astype(self.o_ref.dtype)
                states[h] = states[h] * self.decs[n][:, sl] + self.kv_ref[n, h]
        for h in range(GLA_HEADS):
            self.s_ref[h] = states[h]


def _gla_kernel(qkf_ref, vf_ref, lrf_ref, qkb_ref, vb_ref, lrb_ref, wup_ref, bup_ref,
                of_ref, ob_ref, sf_ref, sb_ref, oi_ref, kv_ref, *, n_chunks):
    @pl.when(pl.program_id(1) == 0)
    def _():
        sf_ref[...] = jnp.zeros_like(sf_ref)
        sb_ref[...] = jnp.zeros_like(sb_ref)

    fwd = _GlaDirection(qkf_ref, vf_ref, lrf_ref[:, 0:GLA_RANK], wup_ref[0], bup_ref[0], of_ref, sf_ref,
                        oi_ref.at[0], kv_ref.at[0], n_chunks, False)
    bwd = _GlaDirection(qkb_ref, vb_ref, lrb_ref[:, GLA_RANK:2 * GLA_RANK], wup_ref[1], bup_ref[1], ob_ref,
                        sb_ref, oi_ref.at[1], kv_ref.at[1], n_chunks, True)
    for phase in ("cumulative_decay", "scale_operands", "scores_and_outer_products", "intra_chunk",
                  "inter_chunk"):
        getattr(fwd, phase)()
        getattr(bwd, phase)()


def _gla(proj, lr, wup, bup, bsz, seq):
    t = proj.shape[0]
    lb = min(512, seq)
    nblk = seq // lb
    vw = GLA_HEADS * GLA_DV
    fwd = lambda b, i: b * nblk + i
    bwd = lambda b, i: b * nblk + nblk - 1 - i
    return pl.pallas_call(
        functools.partial(_gla_kernel, n_chunks=lb // GLA_CHUNK),
        grid=(bsz, nblk),
        in_specs=[pl.BlockSpec((lb, 512), lambda b, i: (fwd(b, i), COL_GQK // 4)),
                  pl.BlockSpec((lb, 512), lambda b, i: (fwd(b, i), COL_GV // 4)),
                  pl.BlockSpec((lb, LANES), lambda b, i: (fwd(b, i), 0)),
                  pl.BlockSpec((lb, 512), lambda b, i: (bwd(b, i), COL_GQK // 4)),
                  pl.BlockSpec((lb, 512), lambda b, i: (bwd(b, i), COL_GV // 4)),
                  pl.BlockSpec((lb, LANES), lambda b, i: (bwd(b, i), 0)),
                  pl.BlockSpec((2, GLA_RANK, GLA_HEADS * GLA_DK), lambda b, i: (0, 0, 0)),
                  pl.BlockSpec((2, 1, GLA_HEADS * GLA_DK), lambda b, i: (0, 0, 0))],
        out_specs=[pl.BlockSpec((lb, vw), lambda b, i: (fwd(b, i), 0)),
                   pl.BlockSpec((lb, vw), lambda b, i: (bwd(b, i), 0))],
        out_shape=[jax.ShapeDtypeStruct((t, vw), BF16), jax.ShapeDtypeStruct((t, vw), BF16)],
        scratch_shapes=[pltpu.VMEM((GLA_HEADS, GLA_DV, GLA_DK), F32),
                        pltpu.VMEM((GLA_HEADS, GLA_DV, GLA_DK), F32),
                        pltpu.VMEM((2, lb, vw), F32),
                        pltpu.VMEM((2, lb // GLA_CHUNK, GLA_HEADS, GLA_DV, GLA_DK), F32)],
        compiler_params=_cparams(("parallel", "arbitrary")),
        name="gla",
    )(proj, proj, lr, proj, proj, lr, wup, bup)


def _t5_bucket_np(rel):
    nb = REL_BUCKETS // 2
    max_exact = nb // 2
    ret = np.where(rel > 0, nb, 0)
    n = np.abs(rel)
    nf = np.maximum(n, 1).astype(np.float64)
    large = max_exact + (np.log(nf / max_exact) / math.log(REL_MAX_DIST / max_exact)
                         * (nb - max_exact)).astype(np.int64)
    large = np.minimum(large, nb - 1)
    return ret + np.where(n < max_exact, n, large)


def _attn_kernel(fast_ref, q_ref, k_ref, vt_ref, tile_ref, feat_ref, lam_ref, sg_ref, o_ref,
                 m_ref, l_ref, acc_ref, qa_ref, *, tq, tk, nk, lam_init):
    qi = pl.program_id(2)
    q = q_ref[...]
    n_sub = tk // vt_ref.shape[-1]
    ts = tk // n_sub
    tasks = [(u, c) for u in range(n_sub) for c in range(2)]
    far_tiles = (tile_ref.shape[1] - 1) // 2

    band, left, right = 0, 1, 2
    q_t = q.astype(F32).T
    frow = lax.broadcasted_iota(jnp.int32, (DIFF_DH, tq), 0)
    for var in (band, left, right):
        feat = jnp.zeros((DIFF_DH, tq), F32)
        for n in range(N_FEAT):
            feat = jnp.where(frow == n, feat_ref[pl.program_id(1), var * N_FEAT + n], feat)
        for c in range(2):
            qa_ref[var, c] = jnp.concatenate([q_t[c * DIFF_DH:(c + 1) * DIFF_DH], feat], axis=0).astype(BF16)

    def logits(kb, var, u, c):
        r0 = pl.multiple_of(kb * tk + u * ts, ts)
        k = k_ref[pl.ds(r0, ts), c * LANES:(c + 1) * LANES]
        return jnp.dot(k, qa_ref[var, c], preferred_element_type=F32)

    def bias_tile(kb):
        return tile_ref.at[0, jnp.clip(kb - qi, -far_tiles, far_tiles) + far_tiles]

    def pipelined(blocks, step):
        work = [(kb, var, bias, u, c) for kb, var, bias in blocks for u, c in tasks]
        s_next = logits(work[0][0], work[0][1], work[0][3], work[0][4])
        for t, (kb, var, bias, u, c) in enumerate(work):
            s = s_next
            if t + 1 < len(work):
                nxt = work[t + 1]
                s_next = logits(nxt[0], nxt[1], nxt[3], nxt[4])
            if bias is not None:
                s = s + bias[u * ts:(u + 1) * ts, :]
            step(s, c, vt_ref[kb * n_sub + u])

    pending = {0: [], 1: []}

    def flush_values(c):
        if pending[c]:
            parts, pending[c] = pending[c], []
            acc_ref[c] = acc_ref[c] + jnp.dot(jnp.concatenate([v for v, _ in parts], axis=1),
                                              jnp.concatenate([w for _, w in parts], axis=0),
                                              preferred_element_type=F32)

    def bounded_step(s, c, vt):
        p = jnp.exp2(s)
        l_ref[c] = l_ref[c] + jnp.sum(p, axis=0, keepdims=True)
        pending[c].append((vt, p.astype(BF16)))
        if len(pending[c]) == VALUE_MATMUL_TASKS:
            flush_values(c)

    def online_step(s, c, vt):
        m_old = m_ref[c]
        m_new = jnp.maximum(m_old, jnp.max(s, axis=0, keepdims=True))
        p = jnp.exp2(s - m_new)
        alpha = jnp.exp2(m_old - m_new)
        l_ref[c] = alpha * l_ref[c] + jnp.sum(p, axis=0, keepdims=True)
        acc_ref[c] = alpha * acc_ref[c] + jnp.dot(vt, p.astype(BF16), preferred_element_type=F32)
        m_ref[c] = m_new

    m_ref[...] = jnp.full_like(m_ref, -1e30)
    l_ref[...] = jnp.zeros_like(l_ref)
    acc_ref[...] = jnp.zeros_like(acc_ref)

    @pl.when(fast_ref[0] == 1)
    def _():
        offsets = list(range(nk)) if nk < 3 else [-1, 0, 1] + list(range(2, nk - 1))
        blocks = []
        for d in offsets:
            kb = qi + d
            kb = jnp.where(kb >= nk, kb - nk, jnp.where(kb < 0, kb + nk, kb))
            if 2 <= d <= nk - 2:
                blocks.append((kb, jnp.where(kb > qi, right, left), None))
            else:
                blocks.append((kb, band, bias_tile(kb)))
        pipelined(blocks, bounded_step)
        flush_values(0)
        flush_values(1)

    @pl.when(fast_ref[0] == 0)
    def _():
        group = 2 if nk % 2 == 0 else 1

        def body(i, carry):
            pipelined([(group * i + g, band, bias_tile(group * i + g)) for g in range(group)], online_step)
            return carry

        lax.fori_loop(0, nk // group, body, 0)

    lv = lam_ref[...]
    lam = (jnp.exp(jnp.sum(lv[0:1] * lv[1:2], axis=-1, keepdims=True))
           - jnp.exp(jnp.sum(lv[2:3] * lv[3:4], axis=-1, keepdims=True)) + lam_init)
    out_t = acc_ref[0] * (1.0 / l_ref[0]) - acc_ref[1] * (lam / l_ref[1])
    out = out_t.T
    ms = jnp.mean(out * out, axis=-1, keepdims=True)
    out = out * lax.rsqrt(ms + RMS_EPS) * sg_ref[...] * (1.0 - lam_init)
    o_ref[...] = out.astype(o_ref.dtype)


def _attn_tile(seq):
    return min(512, seq)


def _attn(proj, vt, tiles, fast, feat, lam_vecs, subn_g, bsz, seq, lam_init):
    t = proj.shape[0]
    tq = tk = _attn_tile(seq)
    nq = seq // tq
    nk = seq // tk
    vb = vt.shape[-1]
    return pl.pallas_call(
        functools.partial(_attn_kernel, tq=tq, tk=tk, nk=nk, lam_init=lam_init),
        grid=(bsz, DIFF_HEADS, nq),
        in_specs=[pl.BlockSpec(memory_space=pltpu.SMEM),
                  pl.BlockSpec((tq, LANES), lambda b, h, i: (b * nq + i, COL_DQ + h)),
                  pl.BlockSpec((seq, 2 * LANES), lambda b, h, i: (b, COL_DKA // 2 + h)),
                  pl.BlockSpec((seq // vb, DIFF_DV, vb), lambda b, h, i: (b, h, 0)),
                  pl.BlockSpec((1, tiles.shape[1], tk, tq), lambda b, h, i: (h, 0, 0, 0)),
                  pl.BlockSpec(memory_space=pltpu.SMEM),
                  pl.BlockSpec((4, DIFF_DH), lambda b, h, i: (0, 0)),
                  pl.BlockSpec((1, DIFF_DV), lambda b, h, i: (0, 0))],
        out_specs=pl.BlockSpec((tq, DIFF_DV), lambda b, h, i: (b * nq + i, h)),
        out_shape=jax.ShapeDtypeStruct((t, DIFF_HEADS * DIFF_DV), BF16),
        scratch_shapes=[pltpu.VMEM((2, 1, tq), F32), pltpu.VMEM((2, 1, tq), F32),
                        pltpu.VMEM((2, DIFF_DV, tq), F32), pltpu.VMEM((3, 2, LANES, tq), BF16)],
        compiler_params=_cparams(("parallel", "parallel", "arbitrary")),
        name="attn",
    )(fast, proj, proj, vt, tiles, feat, lam_vecs, subn_g)


def _bias_tables(rel_bias, tq, tk):
    n = tk + tq - 1
    log2e = math.log2(math.e)
    tiles = []
    for d in (-2, -1, 0, 1, 2):
        m = np.arange(n)
        rel = d * tk - np.where(m < tq, m, m - n)
        onehot = np.eye(REL_BUCKETS, dtype=np.float32)[_t5_bucket_np(rel)]
        y = jnp.dot(jnp.asarray(onehot), rel_bias.astype(F32), precision=HIGHEST).T * log2e
        tiles.append(jnp.tile(y, (1, tk))[:, :tk * (n - 1)].reshape(-1, tk, n - 1)[:, :, :tq])
    return jnp.stack(tiles, axis=1)


def _softmax_features(rel_bias, qn_g, kn_g):
    log2e = math.log2(math.e)
    bound = (1.02 * DIFF_DH ** 0.5 * log2e * jnp.max(jnp.abs(qn_g)) * jnp.max(jnp.abs(kn_g))
             + log2e * jnp.max(jnp.abs(rel_bias)))
    fast = bound <= SOFTMAX_SAFE_RANGE
    nb = REL_BUCKETS // 2
    far = jnp.stack([rel_bias[nb - 1], rel_bias[2 * nb - 1]], axis=-1).astype(F32) * log2e
    c_hi = far.astype(BF16).astype(F32)
    c_lo = (far - c_hi).astype(BF16).astype(F32)
    zeros = jnp.zeros_like(c_hi[:, 0])
    nbv = jnp.broadcast_to(-bound.astype(BF16).astype(F32), zeros.shape)
    rows = jnp.stack([jnp.stack([nbv, zeros, zeros], -1),
                      jnp.stack([nbv, c_hi[:, 0], c_lo[:, 0]], -1),
                      jnp.stack([nbv, c_hi[:, 1], c_lo[:, 1]], -1)], axis=1)
    feat = jnp.where(fast, rows, 0.0).reshape(DIFF_HEADS, 3 * N_FEAT)
    return fast.astype(jnp.int32).reshape(1), feat


def _mix_kernel(x_ref, of_ref, ob_ref, gg_ref, ob2_ref, ga_ref, gb_ref, mod_ref, gn_ref, n2_ref,
                wa_ref, wb_ref, wo_ref, wr_ref, br_ref,
                x1_ref, h2_ref, info_ref, cnt_ref, base_ref, *, tm):
    @pl.when(pl.program_id(0) == 0)
    def _():
        base_ref[...] = jnp.zeros_like(base_ref)

    osum = of_ref[...].astype(F32) + ob_ref[...].astype(F32)
    parts = []
    for h in range(GLA_HEADS):
        sl = osum[:, h * GLA_DV:(h + 1) * GLA_DV]
        ms = jnp.mean(sl * sl, axis=-1, keepdims=True)
        parts.append(sl * lax.rsqrt(ms + RMS_EPS) * gn_ref[...])
    gg = gg_ref[...].astype(F32)
    o_a = jnp.concatenate(parts, axis=-1) * (gg * jax.nn.sigmoid(gg))
    y_a = jnp.dot(o_a.astype(BF16), wa_ref[...], preferred_element_type=F32)
    y_b = jnp.dot(ob2_ref[...], wb_ref[...], preferred_element_type=F32)
    merged = (jax.nn.sigmoid(ga_ref[...].astype(F32)) * y_a
              + jax.nn.sigmoid(gb_ref[...].astype(F32)) * y_b)
    y = jnp.dot(merged.astype(BF16), wo_ref[...], preferred_element_type=F32)
    x1 = x_ref[...] + mod_ref[0, 2:3, :] * y
    x1_ref[...] = x1

    ms = jnp.mean(x1 * x1, axis=-1, keepdims=True)
    h2 = x1 * lax.rsqrt(ms + RMS_EPS) * n2_ref[...]
    h2 = h2 * (1.0 + mod_ref[0, 4:5, :]) + mod_ref[0, 3:4, :]
    h2_ref[...] = _pack_rows(h2)

    h2_hi = h2.astype(BF16)
    h2_lo = (h2 - h2_hi.astype(F32)).astype(BF16)
    logits = (jnp.dot(h2_hi, wr_ref[0], preferred_element_type=F32)
              + jnp.dot(h2_lo, wr_ref[0], preferred_element_type=F32)
              + jnp.dot(h2_hi, wr_ref[1], preferred_element_type=F32)) + br_ref[...]
    lane_i = lax.broadcasted_iota(jnp.int32, logits.shape, 1)
    lane = lane_i.astype(F32)
    neg = jnp.float32(-3e38)
    big = jnp.float32(1 << 20)
    is_g = lane < N_GROUPS
    gl = jnp.where(is_g, logits, neg)
    gmax = jnp.max(gl, axis=-1, keepdims=True)
    grp = jnp.min(jnp.where(jnp.logical_and(is_g, gl == gmax), lane, big), axis=-1, keepdims=True)
    p_grp = 1.0 / jnp.sum(jnp.where(is_g, jnp.exp(gl - gmax), 0.0), axis=-1, keepdims=True)
    lo = N_GROUPS + grp * EXPERTS_PER_GROUP
    in_grp = jnp.logical_and(lane >= lo, lane < lo + EXPERTS_PER_GROUP)
    el = jnp.where(in_grp, logits, neg)
    e1 = jnp.max(el, axis=-1, keepdims=True)
    i1 = jnp.min(jnp.where(jnp.logical_and(in_grp, el == e1), lane, big), axis=-1, keepdims=True)
    rest = jnp.logical_and(in_grp, lane != i1)
    el2 = jnp.where(rest, logits, neg)
    e2 = jnp.max(el2, axis=-1, keepdims=True)
    i2 = jnp.min(jnp.where(jnp.logical_and(rest, el2 == e2), lane, big), axis=-1, keepdims=True)
    r = jnp.exp(e2 - e1)
    w1 = p_grp / (1.0 + r)
    w2 = w1 * r

    oh1 = (lane == i1).astype(F32)
    oh2 = (lane == i2).astype(F32)
    rr = lax.broadcasted_iota(jnp.int32, (tm, tm), 0)
    cc = lax.broadcasted_iota(jnp.int32, (tm, tm), 1)
    tril = (cc < rr).astype(BF16)
    before = jnp.dot(tril, (oh1 + oh2).astype(BF16), preferred_element_type=F32) + base_ref[...]
    rank1 = jnp.sum(oh1 * before, axis=-1, keepdims=True)
    rank2 = jnp.sum(oh2 * before, axis=-1, keepdims=True)
    base_new = base_ref[...] + jnp.sum(oh1 + oh2, axis=0, keepdims=True)
    base_ref[...] = base_new
    cnt_ref[...] = jnp.broadcast_to(base_new, cnt_ref.shape)

    ex1 = i1 - N_GROUPS
    ex2 = i2 - N_GROUPS
    info = jnp.zeros(logits.shape, F32)
    for idx, val in enumerate((ex1, ex2, rank1, rank2, w1, w2)):
        info = jnp.where(lane_i == idx, val, info)
    info_ref[...] = info


def _mix(x, of, ob, proj, ob2, mod, gn, n2, wa, wb, wo, wr, br, seq):
    t, d = x.shape
    tm = min(512, seq)
    tpb = seq // tm
    row = lambda i: (i, 0)
    const = lambda i: (0, 0)
    return pl.pallas_call(
        functools.partial(_mix_kernel, tm=tm),
        grid=(t // tm,),
        in_specs=[pl.BlockSpec((tm, d), row),
                  pl.BlockSpec((tm, 512), row),
                  pl.BlockSpec((tm, 512), row),
                  pl.BlockSpec((tm, 512), lambda i: (i, COL_GG // 4)),
                  pl.BlockSpec((tm, 512), row),
                  pl.BlockSpec((tm, 1024), lambda i: (i, COL_GA // 8)),
                  pl.BlockSpec((tm, 1024), lambda i: (i, COL_GB // 8)),
                  pl.BlockSpec((1, 6, d), lambda i: (i // tpb, 0, 0)),
                  pl.BlockSpec((1, GLA_DV), const),
                  pl.BlockSpec((1, d), const),
                  pl.BlockSpec((512, d), const),
                  pl.BlockSpec((512, d), const),
                  pl.BlockSpec((d, d), const),
                  pl.BlockSpec((2, d, LANES), lambda i: (0, 0, 0)),
                  pl.BlockSpec((1, LANES), const)],
        out_specs=[pl.BlockSpec((tm, d), row),
                   pl.BlockSpec((tm, d // 2), row),
                   pl.BlockSpec((tm, LANES), row),
                   pl.BlockSpec((8, LANES), const)],
        out_shape=[jax.ShapeDtypeStruct((t, d), F32),
                   jax.ShapeDtypeStruct((t, d // 2), jnp.uint32),
                   jax.ShapeDtypeStruct((t, LANES), F32),
                   jax.ShapeDtypeStruct((8, LANES), F32)],
        scratch_shapes=[pltpu.VMEM((1, LANES), F32)],
        compiler_params=_cparams(("arbitrary",)),
        name="mix",
    )(x, of, ob, proj, ob2, proj, proj, mod, gn, n2, wa, wb, wo, wr, br)


def _dispatch_kernel(pos_ref, h_ref, zeros_ref, xs_ref, sem_ref, *, tb):
    del zeros_ref

    def row_copy(r, k):
        return pltpu.make_async_copy(h_ref.at[pl.ds(r, 1), :],
                                     xs_ref.at[pl.ds(pos_ref[0, 0, 2 * r + k], 1), :], sem_ref)

    def start(r, carry):
        row_copy(r, 0).start(priority=0)
        row_copy(r, 1).start(priority=1)
        return carry

    def wait(r, carry):
        row_copy(r, 0).wait()
        row_copy(r, 1).wait()
        return carry

    lax.fori_loop(0, tb, start, 0, unroll=DMA_ISSUE_UNROLL)
    lax.fori_loop(0, tb, wait, 0, unroll=True)


def _dispatch(h2, pos3, n_rows, tb):
    t, d = h2.shape
    return pl.pallas_call(
        functools.partial(_dispatch_kernel, tb=tb),
        grid_spec=pltpu.PrefetchScalarGridSpec(
            num_scalar_prefetch=0,
            grid=(t // tb,),
            in_specs=[pl.BlockSpec((1, 1, 2 * tb), lambda i: (i, 0, 0), memory_space=pltpu.SMEM),
                      pl.BlockSpec((tb, d), lambda i: (i, 0)),
                      pl.BlockSpec(memory_space=pl.ANY)],
            out_specs=pl.BlockSpec(memory_space=pl.ANY),
            scratch_shapes=[pltpu.SemaphoreType.DMA(())]),
        out_shape=jax.ShapeDtypeStruct((n_rows, d), h2.dtype),
        input_output_aliases={2: 0},
        compiler_params=_cparams(("arbitrary",)),
        name="dispatch",
    )(pos3, h2, jnp.zeros((n_rows, d), h2.dtype))


def _expert_kernel(te_ref, nv_ref, xs_ref, w1_ref, w2_ref, ys_ref, w1b_ref, w2b_ref):
    i = pl.program_id(0)

    @pl.when(i >= nv_ref[0])
    def _():
        ys_ref[...] = jnp.zeros_like(ys_ref)

    new_expert = jnp.logical_or(i == 0, te_ref[i] != te_ref[jnp.maximum(i - 1, 0)])

    @pl.when(jnp.logical_and(i < nv_ref[0], new_expert))
    def _():
        w1b_ref[...] = w1_ref[0].astype(BF16)
        w2b_ref[...] = w2_ref[0].astype(BF16)

    @pl.when(i < nv_ref[0])
    def _():
        x_hi, x_lo = _unpack_rows(xs_ref[...])
        half = w1b_ref.shape[0] // 2
        hu = (jnp.dot(x_hi.astype(BF16), w1b_ref[:half, :], preferred_element_type=F32)
              + jnp.dot(x_lo.astype(BF16), w1b_ref[half:, :], preferred_element_type=F32))
        hg = hu[:, :D_EXPERT]
        act = (hg * jax.nn.sigmoid(hg)) * hu[:, D_EXPERT:]
        ys_ref[...] = _pack_rows(jnp.dot(act.astype(BF16), w2b_ref[...], preferred_element_type=F32))


def _experts(xs, tile_expert, n_valid, w1, w2, n_tiles):
    d = w1.shape[1]
    dp = xs.shape[1]
    blk = lambda i, te, nv: (jnp.maximum(jnp.minimum(i, nv[0] - 1), 0), 0)
    return pl.pallas_call(
        _expert_kernel,
        grid_spec=pltpu.PrefetchScalarGridSpec(
            num_scalar_prefetch=2,
            grid=(n_tiles,),
            in_specs=[pl.BlockSpec((EXPERT_TILE, dp), blk),
                      pl.BlockSpec((1, d, 2 * D_EXPERT), lambda i, te, nv: (te[i], 0, 0)),
                      pl.BlockSpec((1, D_EXPERT, d), lambda i, te, nv: (te[i], 0, 0))],
            out_specs=pl.BlockSpec((EXPERT_TILE, dp), lambda i, te, nv: (i, 0)),
            scratch_shapes=[pltpu.VMEM((d, 2 * D_EXPERT), BF16), pltpu.VMEM((D_EXPERT, d), BF16)]),
        out_shape=jax.ShapeDtypeStruct((n_tiles * EXPERT_TILE, dp), xs.dtype),
        compiler_params=_cparams(("arbitrary",)),
        name="experts",
    )(tile_expert, n_valid, xs, w1, w2)


def _combine_kernel(pos_ref, ys_ref, x1_ref, info_ref, mod_ref, o_ref, buf_ref, sem_ref, *, tb):
    def row_copy(r, k):
        return pltpu.make_async_copy(ys_ref.at[pl.ds(pos_ref[0, 0, 2 * r + k], 1), :],
                                     buf_ref.at[k, pl.ds(r, 1), :], sem_ref)

    def start(r, carry):
        row_copy(r, 0).start(priority=0)
        row_copy(r, 1).start(priority=1)
        return carry

    def wait(r, carry):
        row_copy(r, 0).wait()
        row_copy(r, 1).wait()
        return carry

    lax.fori_loop(0, tb, start, 0, unroll=DMA_ISSUE_UNROLL)
    lax.fori_loop(0, tb, wait, 0, unroll=True)
    info = info_ref[...]
    hi0, lo0 = _unpack_rows(buf_ref[0])
    hi1, lo1 = _unpack_rows(buf_ref[1])
    w0, w1 = info[:, 4:5], info[:, 5:6]
    y = jnp.concatenate([hi0 * w0 + hi1 * w1, lo0 * w0 + lo1 * w1], axis=1)
    o_ref[...] = x1_ref[...] + mod_ref[0, 5:6, :] * y


def _combine(ys, pos3, x1, info, mod, seq, tb):
    t, d = x1.shape
    tpb = seq // tb
    return pl.pallas_call(
        functools.partial(_combine_kernel, tb=tb),
        grid_spec=pltpu.PrefetchScalarGridSpec(
            num_scalar_prefetch=0,
            grid=(t // tb,),
            in_specs=[pl.BlockSpec((1, 1, 2 * tb), lambda i: (i, 0, 0), memory_space=pltpu.SMEM),
                      pl.BlockSpec(memory_space=pl.ANY),
                      pl.BlockSpec((tb, d), lambda i: (i, 0)),
                      pl.BlockSpec((tb, LANES), lambda i: (i, 0)),
                      pl.BlockSpec((1, 6, d), lambda i: (i // tpb, 0, 0))],
            out_specs=pl.BlockSpec((tb, d), lambda i: (i, 0)),
            scratch_shapes=[pltpu.VMEM((2, tb, ys.shape[1]), ys.dtype), pltpu.SemaphoreType.DMA(())]),
        out_shape=jax.ShapeDtypeStruct((t, d), F32),
        compiler_params=_cparams(("arbitrary",)),
        name="combine",
    )(pos3, ys, x1, info, mod)


def _moe(h2, x1, info, counts, mod, w1, w2, layer, seq):
    t, d = h2.shape
    tb = min(256, seq)
    n_tiles = (2 * t + N_EXPERTS * (EXPERT_TILE - 1)) // EXPERT_TILE
    cnt = counts[0, N_GROUPS:N_GROUPS + N_EXPERTS].astype(jnp.int32)
    padded = ((cnt + EXPERT_TILE - 1) // EXPERT_TILE) * EXPERT_TILE
    ends = jnp.cumsum(padded)
    starts = ends - padded
    n_valid = (ends[-1] // EXPERT_TILE).astype(jnp.int32).reshape(1)
    tile_start = jnp.arange(n_tiles, dtype=jnp.int32) * EXPERT_TILE
    tile_expert = jnp.sum(ends[None, :] <= tile_start[:, None], axis=1).astype(jnp.int32)
    tile_expert = jnp.minimum(tile_expert, tile_expert[jnp.maximum(n_valid[0] - 1, 0)])
    tile_expert = tile_expert + layer * N_EXPERTS
    eid = info[:, 0:2].astype(jnp.int32)
    pos = starts[eid] + info[:, 2:4].astype(jnp.int32)
    pos3 = pos.reshape(t // tb, 1, 2 * tb)
    xs = _dispatch(h2, pos3, n_tiles * EXPERT_TILE, tb)
    ys = _experts(xs, tile_expert, n_valid, w1, w2, n_tiles)
    return _combine(ys, pos3, x1, info, mod, seq, tb)


def _qk_norm_rows(qn_g, kn_g):
    zeros = jnp.zeros((DIFF_DH,), F32)
    feat = jnp.asarray(np.arange(DIFF_DH) < N_FEAT, F32)
    q_gain = jnp.tile(qn_g, 2 * DIFF_HEADS) * (DIFF_DH ** -0.5 * math.log2(math.e))
    k_gain = jnp.tile(jnp.concatenate([kn_g, zeros]), DIFF_HEADS)
    k_add = jnp.tile(jnp.concatenate([zeros, feat]), DIFF_HEADS)
    return jnp.stack([jnp.stack([q_gain, jnp.zeros_like(q_gain)]),
                      jnp.stack([k_gain, k_add]), jnp.stack([k_gain, k_add])])


def _prep_w_in(w_in_l):
    w = w_in_l
    c = np.cumsum([0, 256, 256, 512, 512, 16, 16, 512, 512, 512, 1024, 1024])
    seg = lambda i: w[:, c[i]:c[i + 1]]
    gq, gk, gv, gg, lrf, lrb, dq, dk, dv, ga, gb = (seg(i) for i in range(11))
    dk_aug = jnp.pad(dk.reshape(-1, 2 * DIFF_HEADS, 1, DIFF_DH), ((0, 0), (0, 0), (0, 1), (0, 0)))
    dk_aug = dk_aug.reshape(-1, 4 * DIFF_HEADS * DIFF_DH)
    main = jnp.concatenate([gq, gk, gv, ga, gb, dk_aug, gg, dq, dv], axis=1).astype(BF16)
    lr = jnp.concatenate([lrf, lrb, jnp.zeros((w.shape[0], LANES - 2 * GLA_RANK), w.dtype)], axis=1)
    return main, lr.astype(BF16)


def kernel(x, c, w_ada, b_ada, norm1_g, norm2_g, w_in, gla_w_up, gla_b_up, gla_norm_g, diff_qnorm_g,
           diff_knorm_g, diff_lambda, diff_subnorm_g, rel_bias, w_branch_a, w_branch_b, w_out,
           w_router_group, b_router_group, w_router_expert, b_router_expert, w_expert_in, w_expert_out):
    bsz, seq, d = x.shape
    t = bsz * seq
    depth = w_ada.shape[0]
    mod_all = _ada(c, w_ada, b_ada).reshape(depth, bsz, 6, d)

    tq = _attn_tile(seq)
    tiles = _bias_tables(rel_bias, tq, tq)
    grp = np.arange(PROJ_TILE) // DIFF_DH
    gmat = jnp.asarray((grp[:, None] == grp[None, :]).astype(np.float32) / DIFF_DH, dtype=BF16)

    xf = x.reshape(t, d)
    for l in range(depth):
        lam_init = 0.8 - 0.6 * math.exp(-0.3 * l)
        mod = mod_all[l]
        w_main, w_lr = _prep_w_in(w_in[l])
        qkg = _qk_norm_rows(diff_qnorm_g[l], diff_knorm_g[l])
        proj, lr, vt = _inproj(xf, mod, norm1_g[l].reshape(1, d), w_main, w_lr, gmat, qkg, seq)
        o_f, o_b = _gla(proj, lr, gla_w_up[l], gla_b_up[l].reshape(2, 1, -1), bsz, seq)
        fast, feat = _softmax_features(rel_bias, diff_qnorm_g[l], diff_knorm_g[l])
        o_b2 = _attn(proj, vt, tiles, fast, feat, diff_lambda[l], diff_subnorm_g[l].reshape(1, -1),
                     bsz, seq, lam_init)
        w_r = jnp.concatenate([w_router_group[l], w_router_expert[l],
                               jnp.zeros((d, LANES - N_GROUPS - N_EXPERTS), F32)], axis=1)
        w_r_hi = w_r.astype(BF16)
        w_r = jnp.stack([w_r_hi, (w_r - w_r_hi.astype(F32)).astype(BF16)])
        b_r = jnp.concatenate([b_router_group[l], b_router_expert[l],
                               jnp.zeros((LANES - N_GROUPS - N_EXPERTS,), F32)]).reshape(1, LANES)
        x1, h2, info, counts = _mix(xf, o_f, o_b, proj, o_b2, mod, gla_norm_g[l].reshape(1, -1),
                                    norm2_g[l].reshape(1, d), w_branch_a[l].astype(BF16),
                                    w_branch_b[l].astype(BF16), w_out[l].astype(BF16), w_r, b_r, seq)
        xf = _moe(h2, x1, info, counts, mod, w_expert_in.reshape(-1, d, 2 * D_EXPERT),
                  w_expert_out.reshape(-1, D_EXPERT, d), l, seq)
    return xf.reshape(bsz, seq, d)
```

```python
import functools
import math

import numpy as np
import jax
import jax.numpy as jnp
from jax import lax
from jax.experimental import pallas as pl
from jax.experimental.pallas import tpu as pltpu

F32 = jnp.float32
BF16 = jnp.bfloat16
HIGHEST = lax.Precision.HIGHEST

D_MODEL = 1024
DEPTH = 2
GLA_HEADS, GLA_DK, GLA_DV, GLA_RANK, GLA_TAU, GLA_CHUNK = 4, 64, 128, 16, 16.0, 64
DIFF_HEADS, DIFF_DH, DIFF_DV = 4, 64, 128
REL_BUCKETS, REL_MAX_DIST = 32, 128
N_GROUPS, EXPERTS_PER_GROUP = 4, 8
N_EXPERTS = N_GROUPS * EXPERTS_PER_GROUP
D_EXPERT = D_MODEL // 2
RMS_EPS = 1e-6
LANES = 128
VMEM_LIMIT = 52 * 1024 * 1024
EXPERT_TILE = 256
DMA_ISSUE_UNROLL = 8

COL_GQK, COL_GV, COL_GA, COL_GB, COL_DKA, COL_GG, COL_DQ, COL_DV = 0, 4, 8, 16, 24, 32, 36, 40
N_PROJ = 44 * LANES
PROJ_TILE = 512
N_FEAT = 3
SOFTMAX_SAFE_RANGE = 60.0
VALUE_MATMUL_TASKS = 8


def _cparams(sem, vmem=VMEM_LIMIT):
    return pltpu.CompilerParams(dimension_semantics=sem, vmem_limit_bytes=vmem)


def _pack_rows(x):
    w = x.shape[1] // 2
    bits = lax.bitcast_convert_type(x.astype(BF16).astype(F32), jnp.uint32)
    return bits[:, :w] | (bits[:, w:] >> 16)


def _unpack_rows(p):
    hi = lax.bitcast_convert_type(p & jnp.uint32(0xFFFF0000), F32)
    lo = lax.bitcast_convert_type(p << 16, F32)
    return hi, lo


def _ada_kernel(c_ref, w_ref, b_ref, o_ref):
    c = c_ref[...]
    ca = c * jax.nn.sigmoid(c)
    o_ref[0] = jnp.dot(ca, w_ref[0], precision=HIGHEST, preferred_element_type=F32) + b_ref[0]


def _ada(c, w_ada, b_ada):
    depth, d, n = w_ada.shape
    bsz = c.shape[0]
    tn = 1536
    return pl.pallas_call(
        _ada_kernel,
        grid=(depth, n // tn),
        in_specs=[pl.BlockSpec((bsz, d), lambda l, j: (0, 0)),
                  pl.BlockSpec((1, d, tn), lambda l, j: (l, 0, j)),
                  pl.BlockSpec((1, 1, tn), lambda l, j: (l, 0, j))],
        out_specs=pl.BlockSpec((1, bsz, tn), lambda l, j: (l, 0, j)),
        out_shape=jax.ShapeDtypeStruct((depth, bsz, n), F32),
        compiler_params=_cparams(("parallel", "parallel")),
        name="ada",
    )(c, w_ada, b_ada.reshape(depth, 1, n))


def _inproj_kernel(x_ref, mod_ref, g_ref, w_ref, wlr_ref, gmat_ref, qkg_ref, proj_ref, lr_ref, vt_ref):
    x = x_ref[...]
    ms = jnp.mean(x * x, axis=-1, keepdims=True)
    h = x * lax.rsqrt(ms + RMS_EPS) * g_ref[...]
    h = h * (1.0 + mod_ref[0, 1:2, :]) + mod_ref[0, 0:1, :]
    hb = h.astype(BF16)
    w = PROJ_TILE
    qk_tiles = {COL_DQ * LANES // w: 0, COL_DKA * LANES // w: 1, COL_DKA * LANES // w + 1: 2}
    for j in range(N_PROJ // w):
        acc = jnp.dot(hb, w_ref[:, j * w:(j + 1) * w], preferred_element_type=F32)
        if j in qk_tiles:
            ms2 = jnp.dot((acc * acc).astype(BF16), gmat_ref[...], preferred_element_type=F32)
            acc = acc * lax.rsqrt(ms2 + RMS_EPS) * qkg_ref[qk_tiles[j], 0:1, :] + qkg_ref[qk_tiles[j], 1:2, :]
        proj_ref[:, j * w:(j + 1) * w] = acc.astype(BF16)
        if j == COL_DV * LANES // w:
            vt_ref[0] = acc.T.astype(BF16)
    lr_ref[...] = jnp.dot(hb, wlr_ref[...], preferred_element_type=F32)


def _inproj(x, mod, g, w, wlr, gmat, qkg, seq):
    t, d = x.shape
    tm = min(256, seq)
    tpb = seq // tm
    return pl.pallas_call(
        _inproj_kernel,
        grid=(t // tm,),
        in_specs=[pl.BlockSpec((tm, d), lambda i: (i, 0)),
                  pl.BlockSpec((1, 6, d), lambda i: (i // tpb, 0, 0)),
                  pl.BlockSpec((1, d), lambda i: (0, 0)),
                  pl.BlockSpec((d, N_PROJ), lambda i: (0, 0)),
                  pl.BlockSpec((d, LANES), lambda i: (0, 0)),
                  pl.BlockSpec((PROJ_TILE, PROJ_TILE), lambda i: (0, 0)),
                  pl.BlockSpec((3, 2, PROJ_TILE), lambda i: (0, 0, 0))],
        out_specs=[pl.BlockSpec((tm, N_PROJ), lambda i: (i, 0)),
                   pl.BlockSpec((tm, LANES), lambda i: (i, 0)),
                   pl.BlockSpec((1, DIFF_HEADS * DIFF_DV, tm), lambda i: (i, 0, 0))],
        out_shape=[jax.ShapeDtypeStruct((t, N_PROJ), BF16),
                   jax.ShapeDtypeStruct((t, LANES), F32),
                   jax.ShapeDtypeStruct((t // tm, DIFF_HEADS * DIFF_DV, tm), BF16)],
        compiler_params=_cparams(("parallel",)),
        name="inproj",
    )(x, mod, g, w, wlr, gmat, qkg)


class _GlaDirection:
    def __init__(self, qk_ref, v_ref, lr, wup, bup, o_ref, s_ref, oi_ref, kv_ref, n_chunks, reverse):
        self.qk_ref, self.v_ref, self.lr, self.wup, self.bup = qk_ref, v_ref, lr, wup, bup
        self.o_ref, self.s_ref, self.oi_ref, self.kv_ref = o_ref, s_ref, oi_ref, kv_ref
        self.n_chunks, self.reverse = n_chunks, reverse
        c = GLA_CHUNK
        row = lax.broadcasted_iota(jnp.int32, (c, c), 0)
        col = lax.broadcasted_iota(jnp.int32, (c, c), 1)
        if reverse:
            self.smat, self.mask, self.tot_row = (col >= row).astype(BF16), col > row, 0
        else:
            self.smat, self.mask, self.tot_row = (col <= row).astype(BF16), col <= row, c - 1

    def _rows(self, n):
        return slice(n * GLA_CHUNK, (n + 1) * GLA_CHUNK)

    def _head(self, h, width):
        return slice(h * width, (h + 1) * width)

    def cumulative_decay(self):
        z = jnp.dot(self.lr, self.wup, precision=HIGHEST, preferred_element_type=F32) + self.bup
        lg = (jnp.minimum(z, 0.0) - jnp.log(1.0 + jnp.exp(-jnp.abs(z)))) * (1.0 / GLA_TAU)
        lg_hi = lg.astype(BF16)
        lg_lo = (lg - lg_hi.astype(F32)).astype(BF16)
        self.cums = [jnp.dot(self.smat, lg_hi[self._rows(n)], preferred_element_type=F32)
                     + jnp.dot(self.smat, lg_lo[self._rows(n)], preferred_element_type=F32)
                     for n in range(self.n_chunks)]

    def scale_operands(self):
        hd = GLA_HEADS * GLA_DK
        self.decs, self.q_dec, self.k_inv, self.k_end = [], [], [], []
        for n, cum in enumerate(self.cums):
            dec = jnp.exp(cum[self.tot_row:self.tot_row + 1, :])
            qk = self.qk_ref[self._rows(n), :].astype(F32)
            k_inv = qk[:, hd:] * jnp.exp(-cum)
            self.decs.append(dec)
            self.q_dec.append((qk[:, :hd] * jnp.exp(cum) * (GLA_DK ** -0.5)).astype(BF16))
            self.k_end.append((k_inv * dec).astype(BF16))
            self.k_inv.append(k_inv.astype(BF16))

    def scores_and_outer_products(self):
        self.scores = []
        for n in range(self.n_chunks):
            v = self.v_ref[self._rows(n), :]
            for h in range(GLA_HEADS):
                sl = self._head(h, GLA_DK)
                self.scores.append(lax.dot_general(self.q_dec[n][:, sl], self.k_inv[n][:, sl],
                                                   (((1,), (1,)), ((), ())), preferred_element_type=F32))
                self.kv_ref[n, h] = lax.dot_general(v[:, self._head(h, GLA_DV)], self.k_end[n][:, sl],
                                                    (((0,), (0,)), ((), ())), preferred_element_type=F32)

    def intra_chunk(self):
        for n in range(self.n_chunks):
            v = self.v_ref[self._rows(n), :]
            for h in range(GLA_HEADS):
                a = jnp.where(self.mask, self.scores[n * GLA_HEADS + h], 0.0).astype(BF16)
                self.oi_ref[self._rows(n), self._head(h, GLA_DV)] = jnp.dot(
                    a, v[:, self._head(h, GLA_DV)], preferred_element_type=F32)

    def inter_chunk(self):
        states = [self.s_ref[h] for h in range(GLA_HEADS)]
        order = range(self.n_chunks - 1, -1, -1) if self.reverse else range(self.n_chunks)
        for n in order:
            for h in range(GLA_HEADS):
                sl = self._head(h, GLA_DK)
                o = self.oi_ref[self._rows(n), self._head(h, GLA_DV)] + lax.dot_general(
                    self.q_dec[n][:, sl], states[h].astype(BF16), (((1,), (1,)), ((), ())),
                    preferred_element_type=F32)
                self.o_ref[self._rows(n), self._head(h, GLA_DV)] = o.astype(self.o_ref.dtype)
                states[h] = states[h] * self.decs[n][:, sl] + self.kv_ref[n, h]
        for h in range(GLA_HEADS):
            self.s_ref[h] = states[h]


def _gla_kernel(qkf_ref, vf_ref, lrf_ref, qkb_ref, vb_ref, lrb_ref, wup_ref, bup_ref,
                of_ref, ob_ref, sf_ref, sb_ref, oi_ref, kv_ref, *, n_chunks):
    @pl.when(pl.program_id(1) == 0)
    def _():
        sf_ref[...] = jnp.zeros_like(sf_ref)
        sb_ref[...] = jnp.zeros_like(sb_ref)

    fwd = _GlaDirection(qkf_ref, vf_ref, lrf_ref[:, 0:GLA_RANK], wup_ref[0], bup_ref[0], of_ref, sf_ref,
                        oi_ref.at[0], kv_ref.at[0], n_chunks, False)
    bwd = _GlaDirection(qkb_ref, vb_ref, lrb_ref[:, GLA_RANK:2 * GLA_RANK], wup_ref[1], bup_ref[1], ob_ref,
                        sb_ref, oi_ref.at[1], kv_ref.at[1], n_chunks, True)
    for phase in ("cumulative_decay", "scale_operands", "scores_and_outer_products", "intra_chunk",
                  "inter_chunk"):
        getattr(fwd, phase)()
        getattr(bwd, phase)()


def _gla(proj, lr, wup, bup, bsz, seq):
    t = proj.shape[0]
    lb = min(512, seq)
    nblk = seq // lb
    vw = GLA_HEADS * GLA_DV
    fwd = lambda b, i: b * nblk + i
    bwd = lambda b, i: b * nblk + nblk - 1 - i
    return pl.pallas_call(
        functools.partial(_gla_kernel, n_chunks=lb // GLA_CHUNK),
        grid=(bsz, nblk),
        in_specs=[pl.BlockSpec((lb, 512), lambda b, i: (fwd(b, i), COL_GQK // 4)),
                  pl.BlockSpec((lb, 512), lambda b, i: (fwd(b, i), COL_GV // 4)),
                  pl.BlockSpec((lb, LANES), lambda b, i: (fwd(b, i), 0)),
                  pl.BlockSpec((lb, 512), lambda b, i: (bwd(b, i), COL_GQK // 4)),
                  pl.BlockSpec((lb, 512), lambda b, i: (bwd(b, i), COL_GV // 4)),
                  pl.BlockSpec((lb, LANES), lambda b, i: (bwd(b, i), 0)),
                  pl.BlockSpec((2, GLA_RANK, GLA_HEADS * GLA_DK), lambda b, i: (0, 0, 0)),
                  pl.BlockSpec((2, 1, GLA_HEADS * GLA_DK), lambda b, i: (0, 0, 0))],
        out_specs=[pl.BlockSpec((lb, vw), lambda b, i: (fwd(b, i), 0)),
                   pl.BlockSpec((lb, vw), lambda b, i: (bwd(b, i), 0))],
        out_shape=[jax.ShapeDtypeStruct((t, vw), BF16), jax.ShapeDtypeStruct((t, vw), BF16)],
        scratch_shapes=[pltpu.VMEM((GLA_HEADS, GLA_DV, GLA_DK), F32),
                        pltpu.VMEM((GLA_HEADS, GLA_DV, GLA_DK), F32),
                        pltpu.VMEM((2, lb, vw), F32),
                        pltpu.VMEM((2, lb // GLA_CHUNK, GLA_HEADS, GLA_DV, GLA_DK), F32)],
        compiler_params=_cparams(("parallel", "arbitrary")),
        name="gla",
    )(proj, proj, lr, proj, proj, lr, wup, bup)


def _t5_bucket_np(rel):
    nb = REL_BUCKETS // 2
    max_exact = nb // 2
    ret = np.where(rel > 0, nb, 0)
    n = np.abs(rel)
    nf = np.maximum(n, 1).astype(np.float64)
    large = max_exact + (np.log(nf / max_exact) / math.log(REL_MAX_DIST / max_exact)
                         * (nb - max_exact)).astype(np.int64)
    large = np.minimum(large, nb - 1)
    return ret + np.where(n < max_exact, n, large)


def _attn_kernel(fast_ref, q_ref, k_ref, vt_ref, tile_ref, feat_ref, lam_ref, sg_ref, o_ref,
                 m_ref, l_ref, acc_ref, qa_ref, *, tq, tk, nk, lam_init):
    qi = pl.program_id(2)
    q = q_ref[...]
    n_sub = tk // vt_ref.shape[-1]
    ts = tk // n_sub
    tasks = [(u, c) for u in range(n_sub) for c in range(2)]
    far_tiles = (tile_ref.shape[1] - 1) // 2

    band, left, right = 0, 1, 2
    q_t = q.astype(F32).T
    frow = lax.broadcasted_iota(jnp.int32, (DIFF_DH, tq), 0)
    for var in (band, left, right):
        feat = jnp.zeros((DIFF_DH, tq), F32)
        for n in range(N_FEAT):
            feat = jnp.where(frow == n, feat_ref[pl.program_id(1), var * N_FEAT + n], feat)
        for c in range(2):
            qa_ref[var, c] = jnp.concatenate([q_t[c * DIFF_DH:(c + 1) * DIFF_DH], feat], axis=0).astype(BF16)

    def logits(kb, var, u, c):
        r0 = pl.multiple_of(kb * tk + u * ts, ts)
        k = k_ref[pl.ds(r0, ts), c * LANES:(c + 1) * LANES]
        return jnp.dot(k, qa_ref[var, c], preferred_element_type=F32)

    def bias_tile(kb):
        return tile_ref.at[0, jnp.clip(kb - qi, -far_tiles, far_tiles) + far_tiles]

    def pipelined(blocks, step):
        work = [(kb, var, bias, u, c) for kb, var, bias in blocks for u, c in tasks]
        s_next = logits(work[0][0], work[0][1], work[0][3], work[0][4])
        for t, (kb, var, bias, u, c) in enumerate(work):
            s = s_next
            if t + 1 < len(work):
                nxt = work[t + 1]
                s_next = logits(nxt[0], nxt[1], nxt[3], nxt[4])
            if bias is not None:
                s = s + bias[u * ts:(u + 1) * ts, :]
            step(s, c, vt_ref[kb * n_sub + u])

    pending = {0: [], 1: []}

    def flush_values(c):
        if pending[c]:
            parts, pending[c] = pending[c], []
            acc_ref[c] = acc_ref[c] + jnp.dot(jnp.concatenate([v for v, _ in parts], axis=1),
                                              jnp.concatenate([w for _, w in parts], axis=0),
                                              preferred_element_type=F32)

    def bounded_step(s, c, vt):
        p = jnp.exp2(s)
        l_ref[c] = l_ref[c] + jnp.sum(p, axis=0, keepdims=True)
        pending[c].append((vt, p.astype(BF16)))
        if len(pending[c]) == VALUE_MATMUL_TASKS:
            flush_values(c)

    def online_step(s, c, vt):
        m_old = m_ref[c]
        m_new = jnp.maximum(m_old, jnp.max(s, axis=0, keepdims=True))
        p = jnp.exp2(s - m_new)
        alpha = jnp.exp2(m_old - m_new)
        l_ref[c] = alpha * l_ref[c] + jnp.sum(p, axis=0, keepdims=True)
        acc_ref[c] = alpha * acc_ref[c] + jnp.dot(vt, p.astype(BF16), preferred_element_type=F32)
        m_ref[c] = m_new

    m_ref[...] = jnp.full_like(m_ref, -1e30)
    l_ref[...] = jnp.zeros_like(l_ref)
    acc_ref[...] = jnp.zeros_like(acc_ref)

    @pl.when(fast_ref[0] == 1)
    def _():
        offsets = list(range(nk)) if nk < 3 else [-1, 0, 1] + list(range(2, nk - 1))
        blocks = []
        for d in offsets:
            kb = qi + d
            kb = jnp.where(kb >= nk, kb - nk, jnp.where(kb < 0, kb + nk, kb))
            if 2 <= d <= nk - 2:
                blocks.append((kb, jnp.where(kb > qi, right, left), None))
            else:
                blocks.append((kb, band, bias_tile(kb)))
        pipelined(blocks, bounded_step)
        flush_values(0)
        flush_values(1)

    @pl.when(fast_ref[0] == 0)
    def _():
        group = 2 if nk % 2 == 0 else 1

        def body(i, carry):
            pipelined([(group * i + g, band, bias_tile(group * i + g)) for g in range(group)], online_step)
            return carry

        lax.fori_loop(0, nk // group, body, 0)

    lv = lam_ref[...]
    lam = (jnp.exp(jnp.sum(lv[0:1] * lv[1:2], axis=-1, keepdims=True))
           - jnp.exp(jnp.sum(lv[2:3] * lv[3:4], axis=-1, keepdims=True)) + lam_init)
    out_t = acc_ref[0] * (1.0 / l_ref[0]) - acc_ref[1] * (lam / l_ref[1])
    out = out_t.T
    ms = jnp.mean(out * out, axis=-1, keepdims=True)
    out = out * lax.rsqrt(ms + RMS_EPS) * sg_ref[...] * (1.0 - lam_init)
    o_ref[...] = out.astype(o_ref.dtype)


def _attn_tile(seq):
    return min(512, seq)


def _attn(proj, vt, tiles, fast, feat, lam_vecs, subn_g, bsz, seq, lam_init):
    t = proj.shape[0]
    tq = tk = _attn_tile(seq)
    nq = seq // tq
    nk = seq // tk
    vb = vt.shape[-1]
    return pl.pallas_call(
        functools.partial(_attn_kernel, tq=tq, tk=tk, nk=nk, lam_init=lam_init),
        grid=(bsz, DIFF_HEADS, nq),
        in_specs=[pl.BlockSpec(memory_space=pltpu.SMEM),
                  pl.BlockSpec((tq, LANES), lambda b, h, i: (b * nq + i, COL_DQ + h)),
                  pl.BlockSpec((seq, 2 * LANES), lambda b, h, i: (b, COL_DKA // 2 + h)),
                  pl.BlockSpec((seq // vb, DIFF_DV, vb), lambda b, h, i: (b, h, 0)),
                  pl.BlockSpec((1, tiles.shape[1], tk, tq), lambda b, h, i: (h, 0, 0, 0)),
                  pl.BlockSpec(memory_space=pltpu.SMEM),
                  pl.BlockSpec((4, DIFF_DH), lambda b, h, i: (0, 0)),
                  pl.BlockSpec((1, DIFF_DV), lambda b, h, i: (0, 0))],
        out_specs=pl.BlockSpec((tq, DIFF_DV), lambda b, h, i: (b * nq + i, h)),
        out_shape=jax.ShapeDtypeStruct((t, DIFF_HEADS * DIFF_DV), BF16),
        scratch_shapes=[pltpu.VMEM((2, 1, tq), F32), pltpu.VMEM((2, 1, tq), F32),
                        pltpu.VMEM((2, DIFF_DV, tq), F32), pltpu.VMEM((3, 2, LANES, tq), BF16)],
        compiler_params=_cparams(("parallel", "parallel", "arbitrary")),
        name="attn",
    )(fast, proj, proj, vt, tiles, feat, lam_vecs, subn_g)


def _bias_tables(rel_bias, tq, tk):
    period = 2 * REL_MAX_DIST
    assert tq % period == 0 and tk % period == 0
    log2e = math.log2(math.e)
    nb = REL_BUCKETS // 2
    m = np.arange(period)
    rel_rep = np.where(m <= period // 2, -m, period - m)
    onehot = np.eye(REL_BUCKETS, dtype=np.float32)[_t5_bucket_np(rel_rep)]
    y = jnp.dot(jnp.asarray(onehot), rel_bias.astype(F32), precision=HIGHEST).T * log2e
    stride = 2 * period - 1
    block = jnp.tile(y, (1, 2 * period))[:, :period * stride].reshape(-1, period, stride)[:, :, :period]
    band_tile = jnp.tile(block, (1, tk // period, tq // period))
    c_left = (rel_bias[nb - 1].astype(F32) * log2e)[:, None, None]
    c_right = (rel_bias[2 * nb - 1].astype(F32) * log2e)[:, None, None]
    tiles = []
    for d in (-2, -1, 0, 1, 2):
        rel = (d * tk + lax.broadcasted_iota(jnp.int32, (1, tk, tq), 1)
               - lax.broadcasted_iota(jnp.int32, (1, tk, tq), 2))
        tiles.append(jnp.where(jnp.abs(rel) < REL_MAX_DIST, band_tile, jnp.where(rel < 0, c_left, c_right)))
    return jnp.stack(tiles, axis=1)


def _softmax_features(rel_bias, qn_g, kn_g):
    log2e = math.log2(math.e)
    bound = (1.02 * DIFF_DH ** 0.5 * log2e * jnp.max(jnp.abs(qn_g)) * jnp.max(jnp.abs(kn_g))
             + log2e * jnp.max(jnp.abs(rel_bias)))
    fast = bound <= SOFTMAX_SAFE_RANGE
    nb = REL_BUCKETS // 2
    far = jnp.stack([rel_bias[nb - 1], rel_bias[2 * nb - 1]], axis=-1).astype(F32) * log2e
    c_hi = far.astype(BF16).astype(F32)
    c_lo = (far - c_hi).astype(BF16).astype(F32)
    zeros = jnp.zeros_like(c_hi[:, 0])
    nbv = jnp.broadcast_to(-bound.astype(BF16).astype(F32), zeros.shape)
    rows = jnp.stack([jnp.stack([nbv, zeros, zeros], -1),
                      jnp.stack([nbv, c_hi[:, 0], c_lo[:, 0]], -1),
                      jnp.stack([nbv, c_hi[:, 1], c_lo[:, 1]], -1)], axis=1)
    feat = jnp.where(fast, rows, 0.0).reshape(DIFF_HEADS, 3 * N_FEAT)
    return fast.astype(jnp.int32).reshape(1), feat


def _mix_kernel(x_ref, of_ref, ob_ref, gg_ref, ob2_ref, ga_ref, gb_ref, mod_ref, gn_ref, n2_ref,
                wa_ref, wb_ref, wo_ref, wr_ref, br_ref,
                x1_ref, h2_ref, info_ref, info_t_ref, cnt_ref, base_ref, *, tm):
    @pl.when(pl.program_id(0) == 0)
    def _():
        base_ref[...] = jnp.zeros_like(base_ref)

    osum = of_ref[...].astype(F32) + ob_ref[...].astype(F32)
    parts = []
    for h in range(GLA_HEADS):
        sl = osum[:, h * GLA_DV:(h + 1) * GLA_DV]
        ms = jnp.mean(sl * sl, axis=-1, keepdims=True)
        parts.append(sl * lax.rsqrt(ms + RMS_EPS) * gn_ref[...])
    gg = gg_ref[...].astype(F32)
    o_a = jnp.concatenate(parts, axis=-1) * (gg * jax.nn.sigmoid(gg))
    y_a = jnp.dot(o_a.astype(BF16), wa_ref[...], preferred_element_type=F32)
    y_b = jnp.dot(ob2_ref[...], wb_ref[...], preferred_element_type=F32)
    merged = (jax.nn.sigmoid(ga_ref[...].astype(F32)) * y_a
              + jax.nn.sigmoid(gb_ref[...].astype(F32)) * y_b)
    y = jnp.dot(merged.astype(BF16), wo_ref[...], preferred_element_type=F32)
    x1 = x_ref[...] + mod_ref[0, 2:3, :] * y
    x1_ref[...] = x1

    ms = jnp.mean(x1 * x1, axis=-1, keepdims=True)
    h2 = x1 * lax.rsqrt(ms + RMS_EPS) * n2_ref[...]
    h2 = h2 * (1.0 + mod_ref[0, 4:5, :]) + mod_ref[0, 3:4, :]
    h2_ref[...] = _pack_rows(h2)

    h2_hi = h2.astype(BF16)
    h2_lo = (h2 - h2_hi.astype(F32)).astype(BF16)
    logits = (jnp.dot(h2_hi, wr_ref[0], preferred_element_type=F32)
              + jnp.dot(h2_lo, wr_ref[0], preferred_element_type=F32)
              + jnp.dot(h2_hi, wr_ref[1], preferred_element_type=F32)) + br_ref[...]
    lane_i = lax.broadcasted_iota(jnp.int32, logits.shape, 1)
    lane = lane_i.astype(F32)
    neg = jnp.float32(-3e38)
    big = jnp.float32(1 << 20)
    is_g = lane < N_GROUPS
    gl = jnp.where(is_g, logits, neg)
    gmax = jnp.max(gl, axis=-1, keepdims=True)
    grp = jnp.min(jnp.where(jnp.logical_and(is_g, gl == gmax), lane, big), axis=-1, keepdims=True)
    p_grp = 1.0 / jnp.sum(jnp.where(is_g, jnp.exp(gl - gmax), 0.0), axis=-1, keepdims=True)
    lo = N_GROUPS + grp * EXPERTS_PER_GROUP
    in_grp = jnp.logical_and(lane >= lo, lane < lo + EXPERTS_PER_GROUP)
    el = jnp.where(in_grp, logits, neg)
    e1 = jnp.max(el, axis=-1, keepdims=True)
    i1 = jnp.min(jnp.where(jnp.logical_and(in_grp, el == e1), lane, big), axis=-1, keepdims=True)
    rest = jnp.logical_and(in_grp, lane != i1)
    el2 = jnp.where(rest, logits, neg)
    e2 = jnp.max(el2, axis=-1, keepdims=True)
    i2 = jnp.min(jnp.where(jnp.logical_and(rest, el2 == e2), lane, big), axis=-1, keepdims=True)
    r = jnp.exp(e2 - e1)
    w1 = p_grp / (1.0 + r)
    w2 = w1 * r

    oh1 = (lane == i1).astype(F32)
    oh2 = (lane == i2).astype(F32)
    rr = lax.broadcasted_iota(jnp.int32, (tm, tm), 0)
    cc = lax.broadcasted_iota(jnp.int32, (tm, tm), 1)
    tril = (cc < rr).astype(BF16)
    before = jnp.dot(tril, (oh1 + oh2).astype(BF16), preferred_element_type=F32) + base_ref[...]
    rank1 = jnp.sum(oh1 * before, axis=-1, keepdims=True)
    rank2 = jnp.sum(oh2 * before, axis=-1, keepdims=True)
    base_new = base_ref[...] + jnp.sum(oh1 + oh2, axis=0, keepdims=True)
    base_ref[...] = base_new
    cnt_ref[...] = jnp.broadcast_to(base_new, cnt_ref.shape)

    ex1 = i1 - N_GROUPS
    ex2 = i2 - N_GROUPS
    info = jnp.zeros(logits.shape, F32)
    for idx, val in enumerate((ex1, ex2, rank1, rank2, w1, w2)):
        info = jnp.where(lane_i == idx, val, info)
    info_ref[...] = info
    info_t_ref[...] = info.T[0:info_t_ref.shape[0], :]


def _mix(x, of, ob, proj, ob2, mod, gn, n2, wa, wb, wo, wr, br, seq):
    t, d = x.shape
    tm = min(512, seq)
    tpb = seq // tm
    row = lambda i: (i, 0)
    const = lambda i: (0, 0)
    return pl.pallas_call(
        functools.partial(_mix_kernel, tm=tm),
        grid=(t // tm,),
        in_specs=[pl.BlockSpec((tm, d), row),
                  pl.BlockSpec((tm, 512), row),
                  pl.BlockSpec((tm, 512), row),
                  pl.BlockSpec((tm, 512), lambda i: (i, COL_GG // 4)),
                  pl.BlockSpec((tm, 512), row),
                  pl.BlockSpec((tm, 1024), lambda i: (i, COL_GA // 8)),
                  pl.BlockSpec((tm, 1024), lambda i: (i, COL_GB // 8)),
                  pl.BlockSpec((1, 6, d), lambda i: (i // tpb, 0, 0)),
                  pl.BlockSpec((1, GLA_DV), const),
                  pl.BlockSpec((1, d), const),
                  pl.BlockSpec((512, d), const),
                  pl.BlockSpec((512, d), const),
                  pl.BlockSpec((d, d), const),
                  pl.BlockSpec((2, d, LANES), lambda i: (0, 0, 0)),
                  pl.BlockSpec((1, LANES), const)],
        out_specs=[pl.BlockSpec((tm, d), row),
                   pl.BlockSpec((tm, d // 2), row),
                   pl.BlockSpec((tm, LANES), row),
                   pl.BlockSpec((8, tm), lambda i: (0, i)),
                   pl.BlockSpec((8, LANES), const)],
        out_shape=[jax.ShapeDtypeStruct((t, d), F32),
                   jax.ShapeDtypeStruct((t, d // 2), jnp.uint32),
                   jax.ShapeDtypeStruct((t, LANES), F32),
                   jax.ShapeDtypeStruct((8, t), F32),
                   jax.ShapeDtypeStruct((8, LANES), F32)],
        scratch_shapes=[pltpu.VMEM((1, LANES), F32)],
        compiler_params=_cparams(("arbitrary",)),
        name="mix",
    )(x, of, ob, proj, ob2, proj, proj, mod, gn, n2, wa, wb, wo, wr, br)


def _dispatch_kernel(pos_ref, h_ref, zeros_ref, xs_ref, sem_ref, *, tb):
    del zeros_ref

    def row_copy(r, k):
        return pltpu.make_async_copy(h_ref.at[pl.ds(r, 1), :],
                                     xs_ref.at[pl.ds(pos_ref[0, 0, k * tb + r], 1), :], sem_ref)

    def start(r, carry):
        row_copy(r, 0).start(priority=0)
        row_copy(r, 1).start(priority=1)
        return carry

    def wait(r, carry):
        row_copy(r, 0).wait()
        row_copy(r, 1).wait()
        return carry

    lax.fori_loop(0, tb, start, 0, unroll=DMA_ISSUE_UNROLL)
    lax.fori_loop(0, tb, wait, 0, unroll=True)


def _dispatch(h2, pos3, n_rows, tb):
    t, d = h2.shape
    return pl.pallas_call(
        functools.partial(_dispatch_kernel, tb=tb),
        grid_spec=pltpu.PrefetchScalarGridSpec(
            num_scalar_prefetch=0,
            grid=(t // tb,),
            in_specs=[pl.BlockSpec((1, 1, 2 * tb), lambda i: (i, 0, 0), memory_space=pltpu.SMEM),
                      pl.BlockSpec((tb, d), lambda i: (i, 0)),
                      pl.BlockSpec(memory_space=pl.ANY)],
            out_specs=pl.BlockSpec(memory_space=pl.ANY),
            scratch_shapes=[pltpu.SemaphoreType.DMA(())]),
        out_shape=jax.ShapeDtypeStruct((n_rows, d), h2.dtype),
        input_output_aliases={2: 0},
        compiler_params=_cparams(("arbitrary",)),
        name="dispatch",
    )(pos3, h2, jnp.zeros((n_rows, d), h2.dtype))


def _expert_kernel(te_ref, nv_ref, xs_ref, w1_ref, w2_ref, ys_ref, w1b_ref, w2b_ref):
    i = pl.program_id(0)

    @pl.when(i >= nv_ref[0])
    def _():
        ys_ref[...] = jnp.zeros_like(ys_ref)

    new_expert = jnp.logical_or(i == 0, te_ref[i] != te_ref[jnp.maximum(i - 1, 0)])

    @pl.when(jnp.logical_and(i < nv_ref[0], new_expert))
    def _():
        w1b_ref[...] = w1_ref[0].astype(BF16)
        w2b_ref[...] = w2_ref[0].astype(BF16)

    @pl.when(i < nv_ref[0])
    def _():
        x_hi, x_lo = _unpack_rows(xs_ref[...])
        half = w1b_ref.shape[0] // 2
        hu = (jnp.dot(x_hi.astype(BF16), w1b_ref[:half, :], preferred_element_type=F32)
              + jnp.dot(x_lo.astype(BF16), w1b_ref[half:, :], preferred_element_type=F32))
        hg = hu[:, :D_EXPERT]
        act = (hg * jax.nn.sigmoid(hg)) * hu[:, D_EXPERT:]
        ys_ref[...] = _pack_rows(jnp.dot(act.astype(BF16), w2b_ref[...], preferred_element_type=F32))


def _experts(xs, tile_expert, n_valid, w1, w2, n_tiles):
    d = w1.shape[1]
    dp = xs.shape[1]
    blk = lambda i, te, nv: (jnp.maximum(jnp.minimum(i, nv[0] - 1), 0), 0)
    return pl.pallas_call(
        _expert_kernel,
        grid_spec=pltpu.PrefetchScalarGridSpec(
            num_scalar_prefetch=2,
            grid=(n_tiles,),
            in_specs=[pl.BlockSpec((EXPERT_TILE, dp), blk),
                      pl.BlockSpec((1, d, 2 * D_EXPERT), lambda i, te, nv: (te[i], 0, 0)),
                      pl.BlockSpec((1, D_EXPERT, d), lambda i, te, nv: (te[i], 0, 0))],
            out_specs=pl.BlockSpec((EXPERT_TILE, dp), lambda i, te, nv: (i, 0)),
            scratch_shapes=[pltpu.VMEM((d, 2 * D_EXPERT), BF16), pltpu.VMEM((D_EXPERT, d), BF16)]),
        out_shape=jax.ShapeDtypeStruct((n_tiles * EXPERT_TILE, dp), xs.dtype),
        compiler_params=_cparams(("arbitrary",)),
        name="experts",
    )(tile_expert, n_valid, xs, w1, w2)


def _combine_kernel(pos_ref, ys_ref, x1_ref, info_ref, mod_ref, o_ref, buf_ref, sem_ref, *, tb):
    def row_copy(r, k):
        return pltpu.make_async_copy(ys_ref.at[pl.ds(pos_ref[0, 0, k * tb + r], 1), :],
                                     buf_ref.at[k, pl.ds(r, 1), :], sem_ref)

    def start(r, carry):
        row_copy(r, 0).start(priority=0)
        row_copy(r, 1).start(priority=1)
        return carry

    def wait(r, carry):
        row_copy(r, 0).wait()
        row_copy(r, 1).wait()
        return carry

    lax.fori_loop(0, tb, start, 0, unroll=DMA_ISSUE_UNROLL)
    lax.fori_loop(0, tb, wait, 0, unroll=True)
    info = info_ref[...]
    hi0, lo0 = _unpack_rows(buf_ref[0])
    hi1, lo1 = _unpack_rows(buf_ref[1])
    w0, w1 = info[:, 4:5], info[:, 5:6]
    y = jnp.concatenate([hi0 * w0 + hi1 * w1, lo0 * w0 + lo1 * w1], axis=1)
    o_ref[...] = x1_ref[...] + mod_ref[0, 5:6, :] * y


def _combine(ys, pos3, x1, info, mod, seq, tb):
    t, d = x1.shape
    tpb = seq // tb
    return pl.pallas_call(
        functools.partial(_combine_kernel, tb=tb),
        grid_spec=pltpu.PrefetchScalarGridSpec(
            num_scalar_prefetch=0,
            grid=(t // tb,),
            in_specs=[pl.BlockSpec((1, 1, 2 * tb), lambda i: (i, 0, 0), memory_space=pltpu.SMEM),
                      pl.BlockSpec(memory_space=pl.ANY),
                      pl.BlockSpec((tb, d), lambda i: (i, 0)),
                      pl.BlockSpec((tb, LANES), lambda i: (i, 0)),
                      pl.BlockSpec((1, 6, d), lambda i: (i // tpb, 0, 0))],
            out_specs=pl.BlockSpec((tb, d), lambda i: (i, 0)),
            scratch_shapes=[pltpu.VMEM((2, tb, ys.shape[1]), ys.dtype), pltpu.SemaphoreType.DMA(())]),
        out_shape=jax.ShapeDtypeStruct((t, d), F32),
        compiler_params=_cparams(("arbitrary",)),
        name="combine",
    )(pos3, ys, x1, info, mod)


def _moe(h2, x1, info, info_t, counts, mod, w1, w2, layer, seq):
    t, d = h2.shape
    tb = min(256, seq)
    n_tiles = (2 * t + N_EXPERTS * (EXPERT_TILE - 1)) // EXPERT_TILE
    cnt = counts[0, N_GROUPS:N_GROUPS + N_EXPERTS].astype(jnp.int32)
    padded = ((cnt + EXPERT_TILE - 1) // EXPERT_TILE) * EXPERT_TILE
    ends = jnp.cumsum(padded)
    starts = ends - padded
    n_valid = (ends[-1] // EXPERT_TILE).astype(jnp.int32).reshape(1)
    tile_start = jnp.arange(n_tiles, dtype=jnp.int32) * EXPERT_TILE
    tile_expert = jnp.sum(ends[None, :] <= tile_start[:, None], axis=1).astype(jnp.int32)
    tile_expert = jnp.minimum(tile_expert, tile_expert[jnp.maximum(n_valid[0] - 1, 0)])
    tile_expert = tile_expert + layer * N_EXPERTS
    pos = starts[info_t[0:2].astype(jnp.int32)] + info_t[2:4].astype(jnp.int32)
    pos3 = pos.reshape(2, t // tb, tb).transpose(1, 0, 2).reshape(t // tb, 1, 2 * tb)
    xs = _dispatch(h2, pos3, n_tiles * EXPERT_TILE, tb)
    ys = _experts(xs, tile_expert, n_valid, w1, w2, n_tiles)
    return _combine(ys, pos3, x1, info, mod, seq, tb)


def _qk_norm_rows(qn_g, kn_g):
    zeros = jnp.zeros((DIFF_DH,), F32)
    feat = jnp.asarray(np.arange(DIFF_DH) < N_FEAT, F32)
    q_gain = jnp.tile(qn_g, 2 * DIFF_HEADS) * (DIFF_DH ** -0.5 * math.log2(math.e))
    k_gain = jnp.tile(jnp.concatenate([kn_g, zeros]), DIFF_HEADS)
    k_add = jnp.tile(jnp.concatenate([zeros, feat]), DIFF_HEADS)
    return jnp.stack([jnp.stack([q_gain, jnp.zeros_like(q_gain)]),
                      jnp.stack([k_gain, k_add]), jnp.stack([k_gain, k_add])])


def _prep_w_in(w_in_l):
    w = w_in_l
    c = np.cumsum([0, 256, 256, 512, 512, 16, 16, 512, 512, 512, 1024, 1024])
    seg = lambda i: w[:, c[i]:c[i + 1]]
    gq, gk, gv, gg, lrf, lrb, dq, dk, dv, ga, gb = (seg(i) for i in range(11))
    dk_aug = jnp.pad(dk.reshape(-1, 2 * DIFF_HEADS, 1, DIFF_DH), ((0, 0), (0, 0), (0, 1), (0, 0)))
    dk_aug = dk_aug.reshape(-1, 4 * DIFF_HEADS * DIFF_DH)
    main = jnp.concatenate([gq, gk, gv, ga, gb, dk_aug, gg, dq, dv], axis=1).astype(BF16)
    lr = jnp.concatenate([lrf, lrb, jnp.zeros((w.shape[0], LANES - 2 * GLA_RANK), w.dtype)], axis=1)
    return main, lr.astype(BF16)


def kernel(x, c, w_ada, b_ada, norm1_g, norm2_g, w_in, gla_w_up, gla_b_up, gla_norm_g, diff_qnorm_g,
           diff_knorm_g, diff_lambda, diff_subnorm_g, rel_bias, w_branch_a, w_branch_b, w_out,
           w_router_group, b_router_group, w_router_expert, b_router_expert, w_expert_in, w_expert_out):
    bsz, seq, d = x.shape
    t = bsz * seq
    depth = w_ada.shape[0]
    mod_all = _ada(c, w_ada, b_ada).reshape(depth, bsz, 6, d)

    tq = _attn_tile(seq)
    tiles = _bias_tables(rel_bias, tq, tq)
    grp = np.arange(PROJ_TILE) // DIFF_DH
    gmat = jnp.asarray((grp[:, None] == grp[None, :]).astype(np.float32) / DIFF_DH, dtype=BF16)

    xf = x.reshape(t, d)
    for l in range(depth):
        lam_init = 0.8 - 0.6 * math.exp(-0.3 * l)
        mod = mod_all[l]
        w_main, w_lr = _prep_w_in(w_in[l])
        qkg = _qk_norm_rows(diff_qnorm_g[l], diff_knorm_g[l])
        proj, lr, vt = _inproj(xf, mod, norm1_g[l].reshape(1, d), w_main, w_lr, gmat, qkg, seq)
        o_f, o_b = _gla(proj, lr, gla_w_up[l], gla_b_up[l].reshape(2, 1, -1), bsz, seq)
        fast, feat = _softmax_features(rel_bias, diff_qnorm_g[l], diff_knorm_g[l])
        o_b2 = _attn(proj, vt, tiles, fast, feat, diff_lambda[l], diff_subnorm_g[l].reshape(1, -1),
                     bsz, seq, lam_init)
        w_r = jnp.concatenate([w_router_group[l], w_router_expert[l],
                               jnp.zeros((d, LANES - N_GROUPS - N_EXPERTS), F32)], axis=1)
        w_r_hi = w_r.astype(BF16)
        w_r = jnp.stack([w_r_hi, (w_r - w_r_hi.astype(F32)).astype(BF16)])
        b_r = jnp.concatenate([b_router_group[l], b_router_expert[l],
                               jnp.zeros((LANES - N_GROUPS - N_EXPERTS,), F32)]).reshape(1, LANES)
        x1, h2, info, info_t, counts = _mix(xf, o_f, o_b, proj, o_b2, mod, gla_norm_g[l].reshape(1, -1),
                                    norm2_g[l].reshape(1, d), w_branch_a[l].astype(BF16),
                                    w_branch_b[l].astype(BF16), w_out[l].astype(BF16), w_r, b_r, seq)
        xf = _moe(h2, x1, info, info_t, counts, mod, w_expert_in.reshape(-1, d, 2 * D_EXPERT),
                  w_expert_out.reshape(-1, D_EXPERT, d), l, seq)
    return xf.reshape(bsz, seq, d)
```

```python
import functools
import math

import numpy as np
import jax
import jax.numpy as jnp
from jax import lax
from jax.experimental import pallas as pl
from jax.experimental.pallas import tpu as pltpu

F32 = jnp.float32
BF16 = jnp.bfloat16
HIGHEST = lax.Precision.HIGHEST

D_MODEL = 1024
DEPTH = 2
GLA_HEADS, GLA_DK, GLA_DV, GLA_RANK, GLA_TAU, GLA_CHUNK = 4, 64, 128, 16, 16.0, 64
DIFF_HEADS, DIFF_DH, DIFF_DV = 4, 64, 128
REL_BUCKETS, REL_MAX_DIST = 32, 128
N_GROUPS, EXPERTS_PER_GROUP = 4, 8
N_EXPERTS = N_GROUPS * EXPERTS_PER_GROUP
D_EXPERT = D_MODEL // 2
RMS_EPS = 1e-6
LANES = 128
VMEM_LIMIT = 52 * 1024 * 1024
EXPERT_TILE = 256
DMA_ISSUE_UNROLL = 8

COL_GQK, COL_GV, COL_GA, COL_GB, COL_DKA, COL_GG, COL_DQ, COL_DV = 0, 4, 8, 16, 24, 32, 36, 40
N_PROJ = 44 * LANES
PROJ_TILE = 512
N_FEAT = 3
SOFTMAX_SAFE_RANGE = 60.0
VALUE_MATMUL_TASKS = 8


def _cparams(sem, vmem=VMEM_LIMIT):
    return pltpu.CompilerParams(dimension_semantics=sem, vmem_limit_bytes=vmem)


def _pack_rows(x):
    w = x.shape[1] // 2
    bits = lax.bitcast_convert_type(x.astype(BF16).astype(F32), jnp.uint32)
    return bits[:, :w] | (bits[:, w:] >> 16)


def _unpack_rows(p):
    hi = lax.bitcast_convert_type(p & jnp.uint32(0xFFFF0000), F32)
    lo = lax.bitcast_convert_type(p << 16, F32)
    return hi, lo


def _ada_kernel(c_ref, w_ref, b_ref, o_ref):
    c = c_ref[...]
    ca = c * jax.nn.sigmoid(c)
    o_ref[0] = jnp.dot(ca, w_ref[0], precision=HIGHEST, preferred_element_type=F32) + b_ref[0]


def _ada(c, w_ada, b_ada):
    depth, d, n = w_ada.shape
    bsz = c.shape[0]
    tn = 1536
    return pl.pallas_call(
        _ada_kernel,
        grid=(depth, n // tn),
        in_specs=[pl.BlockSpec((bsz, d), lambda l, j: (0, 0)),
                  pl.BlockSpec((1, d, tn), lambda l, j: (l, 0, j)),
                  pl.BlockSpec((1, 1, tn), lambda l, j: (l, 0, j))],
        out_specs=pl.BlockSpec((1, bsz, tn), lambda l, j: (l, 0, j)),
        out_shape=jax.ShapeDtypeStruct((depth, bsz, n), F32),
        compiler_params=_cparams(("parallel", "parallel")),
        name="ada",
    )(c, w_ada, b_ada.reshape(depth, 1, n))


def _inproj_kernel(x_ref, mod_ref, g_ref, w_ref, wlr_ref, gmat_ref, qkg_ref, proj_ref, lr_ref, vt_ref):
    x = x_ref[...]
    ms = jnp.mean(x * x, axis=-1, keepdims=True)
    h = x * lax.rsqrt(ms + RMS_EPS) * g_ref[...]
    h = h * (1.0 + mod_ref[0, 1:2, :]) + mod_ref[0, 0:1, :]
    hb = h.astype(BF16)
    w = PROJ_TILE
    qk_tiles = {COL_DQ * LANES // w: 0, COL_DKA * LANES // w: 1, COL_DKA * LANES // w + 1: 2}
    for j in range(N_PROJ // w):
        acc = jnp.dot(hb, w_ref[:, j * w:(j + 1) * w], preferred_element_type=F32)
        if j in qk_tiles:
            ms2 = jnp.dot((acc * acc).astype(BF16), gmat_ref[...], preferred_element_type=F32)
            acc = acc * lax.rsqrt(ms2 + RMS_EPS) * qkg_ref[qk_tiles[j], 0:1, :] + qkg_ref[qk_tiles[j], 1:2, :]
        proj_ref[:, j * w:(j + 1) * w] = acc.astype(BF16)
        if j == COL_DV * LANES // w:
            vt_ref[0] = acc.T.astype(BF16)
    lr_ref[...] = jnp.dot(hb, wlr_ref[...], preferred_element_type=F32)


def _inproj(x, mod, g, w, wlr, gmat, qkg, seq):
    t, d = x.shape
    tm = min(256, seq)
    tpb = seq // tm
    return pl.pallas_call(
        _inproj_kernel,
        grid=(t // tm,),
        in_specs=[pl.BlockSpec((tm, d), lambda i: (i, 0)),
                  pl.BlockSpec((1, 6, d), lambda i: (i // tpb, 0, 0)),
                  pl.BlockSpec((1, d), lambda i: (0, 0)),
                  pl.BlockSpec((d, N_PROJ), lambda i: (0, 0)),
                  pl.BlockSpec((d, LANES), lambda i: (0, 0)),
                  pl.BlockSpec((PROJ_TILE, PROJ_TILE), lambda i: (0, 0)),
                  pl.BlockSpec((3, 2, PROJ_TILE), lambda i: (0, 0, 0))],
        out_specs=[pl.BlockSpec((tm, N_PROJ), lambda i: (i, 0)),
                   pl.BlockSpec((tm, LANES), lambda i: (i, 0)),
                   pl.BlockSpec((1, DIFF_HEADS * DIFF_DV, tm), lambda i: (i, 0, 0))],
        out_shape=[jax.ShapeDtypeStruct((t, N_PROJ), BF16),
                   jax.ShapeDtypeStruct((t, LANES), F32),
                   jax.ShapeDtypeStruct((t // tm, DIFF_HEADS * DIFF_DV, tm), BF16)],
        compiler_params=_cparams(("parallel",)),
        name="inproj",
    )(x, mod, g, w, wlr, gmat, qkg)


class _GlaDirection:
    def __init__(self, qk_ref, v_ref, lr, wup, bup, o_ref, s_ref, oi_ref, kv_ref, n_chunks, reverse):
        self.qk_ref, self.v_ref, self.lr, self.wup, self.bup = qk_ref, v_ref, lr, wup, bup
        self.o_ref, self.s_ref, self.oi_ref, self.kv_ref = o_ref, s_ref, oi_ref, kv_ref
        self.n_chunks, self.reverse = n_chunks, reverse
        c = GLA_CHUNK
        row = lax.broadcasted_iota(jnp.int32, (c, c), 0)
        col = lax.broadcasted_iota(jnp.int32, (c, c), 1)
        if reverse:
            self.smat, self.mask, self.tot_row = (col >= row).astype(BF16), col > row, 0
        else:
            self.smat, self.mask, self.tot_row = (col <= row).astype(BF16), col <= row, c - 1

    def _rows(self, n):
        return slice(n * GLA_CHUNK, (n + 1) * GLA_CHUNK)

    def _head(self, h, width):
        return slice(h * width, (h + 1) * width)

    def cumulative_decay(self):
        z = jnp.dot(self.lr, self.wup, precision=HIGHEST, preferred_element_type=F32) + self.bup
        lg = (jnp.minimum(z, 0.0) - jnp.log(1.0 + jnp.exp(-jnp.abs(z)))) * (1.0 / GLA_TAU)
        lg_hi = lg.astype(BF16)
        lg_lo = (lg - lg_hi.astype(F32)).astype(BF16)
        self.cums = [jnp.dot(self.smat, lg_hi[self._rows(n)], preferred_element_type=F32)
                     + jnp.dot(self.smat, lg_lo[self._rows(n)], preferred_element_type=F32)
                     for n in range(self.n_chunks)]

    def scale_operands(self):
        hd = GLA_HEADS * GLA_DK
        self.decs, self.q_dec, self.k_inv, self.k_end = [], [], [], []
        for n, cum in enumerate(self.cums):
            dec = jnp.exp(cum[self.tot_row:self.tot_row + 1, :])
            qk = self.qk_ref[self._rows(n), :].astype(F32)
            k_inv = qk[:, hd:] * jnp.exp(-cum)
            self.decs.append(dec)
            self.q_dec.append((qk[:, :hd] * jnp.exp(cum) * (GLA_DK ** -0.5)).astype(BF16))
            self.k_end.append((k_inv * dec).astype(BF16))
            self.k_inv.append(k_inv.astype(BF16))

    def scores_and_outer_products(self):
        self.scores = []
        for n in range(self.n_chunks):
            v = self.v_ref[self._rows(n), :]
            for h in range(GLA_HEADS):
                sl = self._head(h, GLA_DK)
                self.scores.append(lax.dot_general(self.q_dec[n][:, sl], self.k_inv[n][:, sl],
                                                   (((1,), (1,)), ((), ())), preferred_element_type=F32))
                self.kv_ref[n, h] = lax.dot_general(v[:, self._head(h, GLA_DV)], self.k_end[n][:, sl],
                                                    (((0,), (0,)), ((), ())), preferred_element_type=F32)

    def intra_chunk(self):
        for n in range(self.n_chunks):
            v = self.v_ref[self._rows(n), :]
            for h in range(GLA_HEADS):
                a = jnp.where(self.mask, self.scores[n * GLA_HEADS + h], 0.0).astype(BF16)
                self.oi_ref[self._rows(n), self._head(h, GLA_DV)] = jnp.dot(
                    a, v[:, self._head(h, GLA_DV)], preferred_element_type=F32)

    def inter_chunk(self):
        states = [self.s_ref[h] for h in range(GLA_HEADS)]
        order = range(self.n_chunks - 1, -1, -1) if self.reverse else range(self.n_chunks)
        for n in order:
            for h in range(GLA_HEADS):
                sl = self._head(h, GLA_DK)
                o = self.oi_ref[self._rows(n), self._head(h, GLA_DV)] + lax.dot_general(
                    self.q_dec[n][:, sl], states[h].astype(BF16), (((1,), (1,)), ((), ())),
                    preferred_element_type=F32)
                self.o_ref[self._rows(n), self._head(h, GLA_DV)] = o.astype(self.o_ref.dtype)
                states[h] = states[h] * self.decs[n][:, sl] + self.kv_ref[n, h]
        for h in range(GLA_HEADS):
            self.s_ref[h] = states[h]


def _gla_kernel(qkf_ref, vf_ref, lrf_ref, qkb_ref, vb_ref, lrb_ref, wup_ref, bup_ref,
                of_ref, ob_ref, sf_ref, sb_ref, oi_ref, kv_ref, *, n_chunks):
    @pl.when(pl.program_id(1) == 0)
    def _():
        sf_ref[...] = jnp.zeros_like(sf_ref)
        sb_ref[...] = jnp.zeros_like(sb_ref)

    fwd = _GlaDirection(qkf_ref, vf_ref, lrf_ref[:, 0:GLA_RANK], wup_ref[0], bup_ref[0], of_ref, sf_ref,
                        oi_ref.at[0], kv_ref.at[0], n_chunks, False)
    bwd = _GlaDirection(qkb_ref, vb_ref, lrb_ref[:, GLA_RANK:2 * GLA_RANK], wup_ref[1], bup_ref[1], ob_ref,
                        sb_ref, oi_ref.at[1], kv_ref.at[1], n_chunks, True)
    for phase in ("cumulative_decay", "scale_operands", "scores_and_outer_products", "intra_chunk",
                  "inter_chunk"):
        getattr(fwd, phase)()
        getattr(bwd, phase)()


def _gla(proj, lr, wup, bup, bsz, seq):
    t = proj.shape[0]
    lb = min(512, seq)
    nblk = seq // lb
    vw = GLA_HEADS * GLA_DV
    fwd = lambda b, i: b * nblk + i
    bwd = lambda b, i: b * nblk + nblk - 1 - i
    return pl.pallas_call(
        functools.partial(_gla_kernel, n_chunks=lb // GLA_CHUNK),
        grid=(bsz, nblk),
        in_specs=[pl.BlockSpec((lb, 512), lambda b, i: (fwd(b, i), COL_GQK // 4)),
                  pl.BlockSpec((lb, 512), lambda b, i: (fwd(b, i), COL_GV // 4)),
                  pl.BlockSpec((lb, LANES), lambda b, i: (fwd(b, i), 0)),
                  pl.BlockSpec((lb, 512), lambda b, i: (bwd(b, i), COL_GQK // 4)),
                  pl.BlockSpec((lb, 512), lambda b, i: (bwd(b, i), COL_GV // 4)),
                  pl.BlockSpec((lb, LANES), lambda b, i: (bwd(b, i), 0)),
                  pl.BlockSpec((2, GLA_RANK, GLA_HEADS * GLA_DK), lambda b, i: (0, 0, 0)),
                  pl.BlockSpec((2, 1, GLA_HEADS * GLA_DK), lambda b, i: (0, 0, 0))],
        out_specs=[pl.BlockSpec((lb, vw), lambda b, i: (fwd(b, i), 0)),
                   pl.BlockSpec((lb, vw), lambda b, i: (bwd(b, i), 0))],
        out_shape=[jax.ShapeDtypeStruct((t, vw), BF16), jax.ShapeDtypeStruct((t, vw), BF16)],
        scratch_shapes=[pltpu.VMEM((GLA_HEADS, GLA_DV, GLA_DK), F32),
                        pltpu.VMEM((GLA_HEADS, GLA_DV, GLA_DK), F32),
                        pltpu.VMEM((2, lb, vw), F32),
                        pltpu.VMEM((2, lb // GLA_CHUNK, GLA_HEADS, GLA_DV, GLA_DK), F32)],
        compiler_params=_cparams(("parallel", "arbitrary")),
        name="gla",
    )(proj, proj, lr, proj, proj, lr, wup, bup)


def _t5_bucket_np(rel):
    nb = REL_BUCKETS // 2
    max_exact = nb // 2
    ret = np.where(rel > 0, nb, 0)
    n = np.abs(rel)
    nf = np.maximum(n, 1).astype(np.float64)
    large = max_exact + (np.log(nf / max_exact) / math.log(REL_MAX_DIST / max_exact)
                         * (nb - max_exact)).astype(np.int64)
    large = np.minimum(large, nb - 1)
    return ret + np.where(n < max_exact, n, large)


def _attn_kernel(fast_ref, q_ref, k_ref, vt_ref, tile_ref, feat_ref, lam_ref, sg_ref, o_ref,
                 m_ref, l_ref, acc_ref, qa_ref, *, tq, tk, nk, lam_init):
    qi = pl.program_id(2)
    q = q_ref[...]
    n_sub = tk // vt_ref.shape[-1]
    ts = tk // n_sub
    tasks = [(u, c) for u in range(n_sub) for c in range(2)]
    far_tiles = (tile_ref.shape[1] - 1) // 2

    band, left, right = 0, 1, 2
    q_t = q.astype(F32).T
    frow = lax.broadcasted_iota(jnp.int32, (DIFF_DH, tq), 0)
    for var in (band, left, right):
        feat = jnp.zeros((DIFF_DH, tq), F32)
        for n in range(N_FEAT):
            feat = jnp.where(frow == n, feat_ref[pl.program_id(1), var * N_FEAT + n], feat)
        for c in range(2):
            qa_ref[var, c] = jnp.concatenate([q_t[c * DIFF_DH:(c + 1) * DIFF_DH], feat], axis=0).astype(BF16)

    def logits(kb, var, u, c):
        r0 = pl.multiple_of(kb * tk + u * ts, ts)
        k = k_ref[pl.ds(r0, ts), c * LANES:(c + 1) * LANES]
        return jnp.dot(k, qa_ref[var, c], preferred_element_type=F32)

    def bias_tile(kb):
        return tile_ref.at[0, jnp.clip(kb - qi, -far_tiles, far_tiles) + far_tiles]

    def pipelined(blocks, step):
        work = [(kb, var, bias, u, c) for kb, var, bias in blocks for u, c in tasks]
        s_next = logits(work[0][0], work[0][1], work[0][3], work[0][4])
        for t, (kb, var, bias, u, c) in enumerate(work):
            s = s_next
            if t + 1 < len(work):
                nxt = work[t + 1]
                s_next = logits(nxt[0], nxt[1], nxt[3], nxt[4])
            if bias is not None:
                s = s + bias[u * ts:(u + 1) * ts, :]
            step(s, c, vt_ref[kb * n_sub + u])

    pending = {0: [], 1: []}

    def flush_values(c):
        if pending[c]:
            parts, pending[c] = pending[c], []
            acc_ref[c] = acc_ref[c] + jnp.dot(jnp.concatenate([v for v, _ in parts], axis=1),
                                              jnp.concatenate([w for _, w in parts], axis=0),
                                              preferred_element_type=F32)

    def bounded_step(s, c, vt):
        p = jnp.exp2(s)
        l_ref[c] = l_ref[c] + jnp.sum(p, axis=0, keepdims=True)
        pending[c].append((vt, p.astype(BF16)))
        if len(pending[c]) == VALUE_MATMUL_TASKS:
            flush_values(c)

    def online_step(s, c, vt):
        m_old = m_ref[c]
        m_new = jnp.maximum(m_old, jnp.max(s, axis=0, keepdims=True))
        p = jnp.exp2(s - m_new)
        alpha = jnp.exp2(m_old - m_new)
        l_ref[c] = alpha * l_ref[c] + jnp.sum(p, axis=0, keepdims=True)
        acc_ref[c] = alpha * acc_ref[c] + jnp.dot(vt, p.astype(BF16), preferred_element_type=F32)
        m_ref[c] = m_new

    m_ref[...] = jnp.full_like(m_ref, -1e30)
    l_ref[...] = jnp.zeros_like(l_ref)
    acc_ref[...] = jnp.zeros_like(acc_ref)

    @pl.when(fast_ref[0] == 1)
    def _():
        offsets = list(range(nk)) if nk < 3 else [-1, 0, 1] + list(range(2, nk - 1))
        blocks = []
        for d in offsets:
            kb = qi + d
            kb = jnp.where(kb >= nk, kb - nk, jnp.where(kb < 0, kb + nk, kb))
            if 2 <= d <= nk - 2:
                blocks.append((kb, jnp.where(kb > qi, right, left), None))
            else:
                blocks.append((kb, band, bias_tile(kb)))
        pipelined(blocks, bounded_step)
        flush_values(0)
        flush_values(1)

    @pl.when(fast_ref[0] == 0)
    def _():
        group = 2 if nk % 2 == 0 else 1

        def body(i, carry):
            pipelined([(group * i + g, band, bias_tile(group * i + g)) for g in range(group)], online_step)
            return carry

        lax.fori_loop(0, nk // group, body, 0)

    lv = lam_ref[...]
    lam = (jnp.exp(jnp.sum(lv[0:1] * lv[1:2], axis=-1, keepdims=True))
           - jnp.exp(jnp.sum(lv[2:3] * lv[3:4], axis=-1, keepdims=True)) + lam_init)
    out_t = acc_ref[0] * (1.0 / l_ref[0]) - acc_ref[1] * (lam / l_ref[1])
    out = out_t.T
    ms = jnp.mean(out * out, axis=-1, keepdims=True)
    out = out * lax.rsqrt(ms + RMS_EPS) * sg_ref[...] * (1.0 - lam_init)
    o_ref[...] = out.astype(o_ref.dtype)


def _attn_tile(seq):
    return min(512, seq)


def _attn(proj, vt, tiles, fast, feat, lam_vecs, subn_g, bsz, seq, lam_init):
    t = proj.shape[0]
    tq = tk = _attn_tile(seq)
    nq = seq // tq
    nk = seq // tk
    vb = vt.shape[-1]
    return pl.pallas_call(
        functools.partial(_attn_kernel, tq=tq, tk=tk, nk=nk, lam_init=lam_init),
        grid=(bsz, DIFF_HEADS, nq),
        in_specs=[pl.BlockSpec(memory_space=pltpu.SMEM),
                  pl.BlockSpec((tq, LANES), lambda b, h, i: (b * nq + i, COL_DQ + h)),
                  pl.BlockSpec((seq, 2 * LANES), lambda b, h, i: (b, COL_DKA // 2 + h)),
                  pl.BlockSpec((seq // vb, DIFF_DV, vb), lambda b, h, i: (b, h, 0)),
                  pl.BlockSpec((1, tiles.shape[1], tk, tq), lambda b, h, i: (h, 0, 0, 0)),
                  pl.BlockSpec(memory_space=pltpu.SMEM),
                  pl.BlockSpec((4, DIFF_DH), lambda b, h, i: (0, 0)),
                  pl.BlockSpec((1, DIFF_DV), lambda b, h, i: (0, 0))],
        out_specs=pl.BlockSpec((tq, DIFF_DV), lambda b, h, i: (b * nq + i, h)),
        out_shape=jax.ShapeDtypeStruct((t, DIFF_HEADS * DIFF_DV), BF16),
        scratch_shapes=[pltpu.VMEM((2, 1, tq), F32), pltpu.VMEM((2, 1, tq), F32),
                        pltpu.VMEM((2, DIFF_DV, tq), F32), pltpu.VMEM((3, 2, LANES, tq), BF16)],
        compiler_params=_cparams(("parallel", "parallel", "arbitrary")),
        name="attn",
    )(fast, proj, proj, vt, tiles, feat, lam_vecs, subn_g)


def _bias_tables(rel_bias, tq, tk):
    period = 2 * REL_MAX_DIST
    assert tq % period == 0 and tk % period == 0
    log2e = math.log2(math.e)
    nb = REL_BUCKETS // 2
    m = np.arange(period)
    rel_rep = np.where(m <= period // 2, -m, period - m)
    onehot = np.eye(REL_BUCKETS, dtype=np.float32)[_t5_bucket_np(rel_rep)]
    y = jnp.dot(jnp.asarray(onehot), rel_bias.astype(F32), precision=HIGHEST).T * log2e
    stride = 2 * period - 1
    block = jnp.tile(y, (1, 2 * period))[:, :period * stride].reshape(-1, period, stride)[:, :, :period]
    band_tile = jnp.tile(block, (1, tk // period, tq // period))
    c_left = (rel_bias[nb - 1].astype(F32) * log2e)[:, None, None]
    c_right = (rel_bias[2 * nb - 1].astype(F32) * log2e)[:, None, None]
    tiles = []
    for d in (-2, -1, 0, 1, 2):
        rel = (d * tk + lax.broadcasted_iota(jnp.int32, (1, tk, tq), 1)
               - lax.broadcasted_iota(jnp.int32, (1, tk, tq), 2))
        tiles.append(jnp.where(jnp.abs(rel) < REL_MAX_DIST, band_tile, jnp.where(rel < 0, c_left, c_right)))
    return jnp.stack(tiles, axis=1)


def _softmax_features(rel_bias, qn_g, kn_g):
    log2e = math.log2(math.e)
    bound = (1.02 * DIFF_DH ** 0.5 * log2e * jnp.max(jnp.abs(qn_g)) * jnp.max(jnp.abs(kn_g))
             + log2e * jnp.max(jnp.abs(rel_bias)))
    fast = bound <= SOFTMAX_SAFE_RANGE
    nb = REL_BUCKETS // 2
    far = jnp.stack([rel_bias[nb - 1], rel_bias[2 * nb - 1]], axis=-1).astype(F32) * log2e
    c_hi = far.astype(BF16).astype(F32)
    c_lo = (far - c_hi).astype(BF16).astype(F32)
    zeros = jnp.zeros_like(c_hi[:, 0])
    nbv = jnp.broadcast_to(-bound.astype(BF16).astype(F32), zeros.shape)
    rows = jnp.stack([jnp.stack([nbv, zeros, zeros], -1),
                      jnp.stack([nbv, c_hi[:, 0], c_lo[:, 0]], -1),
                      jnp.stack([nbv, c_hi[:, 1], c_lo[:, 1]], -1)], axis=1)
    feat = jnp.where(fast, rows, 0.0).reshape(DIFF_HEADS, 3 * N_FEAT)
    return fast.astype(jnp.int32).reshape(1), feat


def _mix_kernel(x_ref, of_ref, ob_ref, gg_ref, ob2_ref, ga_ref, gb_ref, mod_ref, gn_ref, n2_ref,
                wa_ref, wb_ref, wo_ref, wr_ref, br_ref,
                x1_ref, h2_ref, info_ref, info_t_ref, cnt_ref, base_ref, *, tm):
    @pl.when(pl.program_id(0) == 0)
    def _():
        base_ref[...] = jnp.zeros_like(base_ref)

    osum = of_ref[...].astype(F32) + ob_ref[...].astype(F32)
    parts = []
    for h in range(GLA_HEADS):
        sl = osum[:, h * GLA_DV:(h + 1) * GLA_DV]
        ms = jnp.mean(sl * sl, axis=-1, keepdims=True)
        parts.append(sl * lax.rsqrt(ms + RMS_EPS) * gn_ref[...])
    gg = gg_ref[...].astype(F32)
    o_a = jnp.concatenate(parts, axis=-1) * (gg * jax.nn.sigmoid(gg))
    y_a = jnp.dot(o_a.astype(BF16), wa_ref[...], preferred_element_type=F32)
    y_b = jnp.dot(ob2_ref[...], wb_ref[...], preferred_element_type=F32)
    merged = (jax.nn.sigmoid(ga_ref[...].astype(F32)) * y_a
              + jax.nn.sigmoid(gb_ref[...].astype(F32)) * y_b)
    y = jnp.dot(merged.astype(BF16), wo_ref[...], preferred_element_type=F32)
    x1 = x_ref[...] + mod_ref[0, 2:3, :] * y
    x1_ref[...] = x1

    ms = jnp.mean(x1 * x1, axis=-1, keepdims=True)
    h2 = x1 * lax.rsqrt(ms + RMS_EPS) * n2_ref[...]
    h2 = h2 * (1.0 + mod_ref[0, 4:5, :]) + mod_ref[0, 3:4, :]
    h2_ref[...] = _pack_rows(h2)

    h2_hi = h2.astype(BF16)
    h2_lo = (h2 - h2_hi.astype(F32)).astype(BF16)
    logits = (jnp.dot(h2_hi, wr_ref[0], preferred_element_type=F32)
              + jnp.dot(h2_lo, wr_ref[0], preferred_element_type=F32)
              + jnp.dot(h2_hi, wr_ref[1], preferred_element_type=F32)) + br_ref[...]
    lane_i = lax.broadcasted_iota(jnp.int32, logits.shape, 1)
    lane = lane_i.astype(F32)
    neg = jnp.float32(-3e38)
    big = jnp.float32(1 << 20)
    is_g = lane < N_GROUPS
    gl = jnp.where(is_g, logits, neg)
    gmax = jnp.max(gl, axis=-1, keepdims=True)
    grp = jnp.min(jnp.where(jnp.logical_and(is_g, gl == gmax), lane, big), axis=-1, keepdims=True)
    p_grp = 1.0 / jnp.sum(jnp.where(is_g, jnp.exp(gl - gmax), 0.0), axis=-1, keepdims=True)
    lo = N_GROUPS + grp * EXPERTS_PER_GROUP
    in_grp = jnp.logical_and(lane >= lo, lane < lo + EXPERTS_PER_GROUP)
    el = jnp.where(in_grp, logits, neg)
    e1 = jnp.max(el, axis=-1, keepdims=True)
    i1 = jnp.min(jnp.where(jnp.logical_and(in_grp, el == e1), lane, big), axis=-1, keepdims=True)
    rest = jnp.logical_and(in_grp, lane != i1)
    el2 = jnp.where(rest, logits, neg)
    e2 = jnp.max(el2, axis=-1, keepdims=True)
    i2 = jnp.min(jnp.where(jnp.logical_and(rest, el2 == e2), lane, big), axis=-1, keepdims=True)
    r = jnp.exp(e2 - e1)
    w1 = p_grp / (1.0 + r)
    w2 = w1 * r

    oh1 = (lane == i1).astype(F32)
    oh2 = (lane == i2).astype(F32)
    rr = lax.broadcasted_iota(jnp.int32, (tm, tm), 0)
    cc = lax.broadcasted_iota(jnp.int32, (tm, tm), 1)
    tril = (cc < rr).astype(BF16)
    before = jnp.dot(tril, (oh1 + oh2).astype(BF16), preferred_element_type=F32) + base_ref[...]
    rank1 = jnp.sum(oh1 * before, axis=-1, keepdims=True)
    rank2 = jnp.sum(oh2 * before, axis=-1, keepdims=True)
    base_new = base_ref[...] + jnp.sum(oh1 + oh2, axis=0, keepdims=True)
    base_ref[...] = base_new
    cnt_ref[...] = jnp.broadcast_to(base_new, cnt_ref.shape)

    ex1 = i1 - N_GROUPS
    ex2 = i2 - N_GROUPS
    info = jnp.zeros(logits.shape, F32)
    for idx, val in enumerate((ex1, ex2, rank1, rank2, w1, w2)):
        info = jnp.where(lane_i == idx, val, info)
    info_ref[...] = info
    info_t_ref[...] = info.T[0:info_t_ref.shape[0], :]


def _mix(x, of, ob, proj, ob2, mod, gn, n2, wa, wb, wo, wr, br, seq):
    t, d = x.shape
    tm = min(512, seq)
    tpb = seq // tm
    row = lambda i: (i, 0)
    const = lambda i: (0, 0)
    return pl.pallas_call(
        functools.partial(_mix_kernel, tm=tm),
        grid=(t // tm,),
        in_specs=[pl.BlockSpec((tm, d), row),
                  pl.BlockSpec((tm, 512), row),
                  pl.BlockSpec((tm, 512), row),
                  pl.BlockSpec((tm, 512), lambda i: (i, COL_GG // 4)),
                  pl.BlockSpec((tm, 512), row),
                  pl.BlockSpec((tm, 1024), lambda i: (i, COL_GA // 8)),
                  pl.BlockSpec((tm, 1024), lambda i: (i, COL_GB // 8)),
                  pl.BlockSpec((1, 6, d), lambda i: (i // tpb, 0, 0)),
                  pl.BlockSpec((1, GLA_DV), const),
                  pl.BlockSpec((1, d), const),
                  pl.BlockSpec((512, d), const),
                  pl.BlockSpec((512, d), const),
                  pl.BlockSpec((d, d), const),
                  pl.BlockSpec((2, d, LANES), lambda i: (0, 0, 0)),
                  pl.BlockSpec((1, LANES), const)],
        out_specs=[pl.BlockSpec((tm, d), row),
                   pl.BlockSpec((tm, d // 2), row),
                   pl.BlockSpec((tm, LANES), row),
                   pl.BlockSpec((8, tm), lambda i: (0, i)),
                   pl.BlockSpec((8, LANES), const)],
        out_shape=[jax.ShapeDtypeStruct((t, d), F32),
                   jax.ShapeDtypeStruct((t, d // 2), jnp.uint32),
                   jax.ShapeDtypeStruct((t, LANES), F32),
                   jax.ShapeDtypeStruct((8, t), F32),
                   jax.ShapeDtypeStruct((8, LANES), F32)],
        scratch_shapes=[pltpu.VMEM((1, LANES), F32)],
        compiler_params=_cparams(("arbitrary",)),
        name="mix",
    )(x, of, ob, proj, ob2, proj, proj, mod, gn, n2, wa, wb, wo, wr, br)


def _dispatch_kernel(pos_ref, h_ref, zeros_ref, xs_ref, sem_ref, *, tb):
    del zeros_ref

    def row_copy(r, k):
        return pltpu.make_async_copy(h_ref.at[pl.ds(r, 1), :],
                                     xs_ref.at[pl.ds(pos_ref[0, 0, k * tb + r], 1), :], sem_ref)

    def start(r, carry):
        row_copy(r, 0).start(priority=0)
        row_copy(r, 1).start(priority=1)
        return carry

    def wait(r, carry):
        row_copy(r, 0).wait()
        row_copy(r, 1).wait()
        return carry

    lax.fori_loop(0, tb, start, 0, unroll=DMA_ISSUE_UNROLL)
    lax.fori_loop(0, tb, wait, 0, unroll=True)


def _dispatch(h2, pos3, n_rows, tb):
    t, d = h2.shape
    return pl.pallas_call(
        functools.partial(_dispatch_kernel, tb=tb),
        grid_spec=pltpu.PrefetchScalarGridSpec(
            num_scalar_prefetch=0,
            grid=(t // tb,),
            in_specs=[pl.BlockSpec((1, 1, 2 * tb), lambda i: (i, 0, 0), memory_space=pltpu.SMEM),
                      pl.BlockSpec((tb, d), lambda i: (i, 0)),
                      pl.BlockSpec(memory_space=pl.ANY)],
            out_specs=pl.BlockSpec(memory_space=pl.ANY),
            scratch_shapes=[pltpu.SemaphoreType.DMA(())]),
        out_shape=jax.ShapeDtypeStruct((n_rows, d), h2.dtype),
        input_output_aliases={2: 0},
        compiler_params=_cparams(("arbitrary",)),
        name="dispatch",
    )(pos3, h2, jnp.zeros((n_rows, d), h2.dtype))


def _expert_kernel(te_ref, nv_ref, xs_ref, w1_ref, w2_ref, ys_ref, w1b_ref, w2b_ref):
    i = pl.program_id(0)

    @pl.when(i >= nv_ref[0])
    def _():
        ys_ref[...] = jnp.zeros_like(ys_ref)

    new_expert = jnp.logical_or(i == 0, te_ref[i] != te_ref[jnp.maximum(i - 1, 0)])

    @pl.when(jnp.logical_and(i < nv_ref[0], new_expert))
    def _():
        w1b_ref[...] = w1_ref[0].astype(BF16)
        w2b_ref[...] = w2_ref[0].astype(BF16)

    @pl.when(i < nv_ref[0])
    def _():
        x_hi, x_lo = _unpack_rows(xs_ref[...])
        half = w1b_ref.shape[0] // 2
        hu = (jnp.dot(x_hi.astype(BF16), w1b_ref[:half, :], preferred_element_type=F32)
              + jnp.dot(x_lo.astype(BF16), w1b_ref[half:, :], preferred_element_type=F32))
        hg = hu[:, :D_EXPERT]
        act = (hg * jax.nn.sigmoid(hg)) * hu[:, D_EXPERT:]
        ys_ref[...] = _pack_rows(jnp.dot(act.astype(BF16), w2b_ref[...], preferred_element_type=F32))


def _experts(xs, tile_expert, n_valid, w1, w2, n_tiles):
    d = w1.shape[1]
    dp = xs.shape[1]
    blk = lambda i, te, nv: (jnp.maximum(jnp.minimum(i, nv[0] - 1), 0), 0)
    return pl.pallas_call(
        _expert_kernel,
        grid_spec=pltpu.PrefetchScalarGridSpec(
            num_scalar_prefetch=2,
            grid=(n_tiles,),
            in_specs=[pl.BlockSpec((EXPERT_TILE, dp), blk),
                      pl.BlockSpec((1, d, 2 * D_EXPERT), lambda i, te, nv: (te[i], 0, 0)),
                      pl.BlockSpec((1, D_EXPERT, d), lambda i, te, nv: (te[i], 0, 0))],
            out_specs=pl.BlockSpec((EXPERT_TILE, dp), lambda i, te, nv: (i, 0)),
            scratch_shapes=[pltpu.VMEM((d, 2 * D_EXPERT), BF16), pltpu.VMEM((D_EXPERT, d), BF16)]),
        out_shape=jax.ShapeDtypeStruct((n_tiles * EXPERT_TILE, dp), xs.dtype),
        compiler_params=_cparams(("arbitrary",)),
        name="experts",
    )(tile_expert, n_valid, xs, w1, w2)


def _combine_kernel(pos_ref, ys_ref, x1_ref, info_ref, mod_ref, o_ref, buf_ref, sem_ref, *, tb):
    def row_copy(r, k):
        return pltpu.make_async_copy(ys_ref.at[pl.ds(pos_ref[0, 0, k * tb + r], 1), :],
                                     buf_ref.at[k, pl.ds(r, 1), :], sem_ref)

    def start(r, carry):
        row_copy(r, 0).start(priority=0)
        row_copy(r, 1).start(priority=1)
        return carry

    def wait(r, carry):
        row_copy(r, 0).wait()
        row_copy(r, 1).wait()
        return carry

    lax.fori_loop(0, tb, start, 0, unroll=DMA_ISSUE_UNROLL)
    lax.fori_loop(0, tb, wait, 0, unroll=True)
    info = info_ref[...]
    hi0, lo0 = _unpack_rows(buf_ref[0])
    hi1, lo1 = _unpack_rows(buf_ref[1])
    w0, w1 = info[:, 4:5], info[:, 5:6]
    y = jnp.concatenate([hi0 * w0 + hi1 * w1, lo0 * w0 + lo1 * w1], axis=1)
    o_ref[...] = x1_ref[...] + mod_ref[0, 5:6, :] * y


def _combine(ys, pos3, x1, info, mod, seq, tb):
    t, d = x1.shape
    tpb = seq // tb
    return pl.pallas_call(
        functools.partial(_combine_kernel, tb=tb),
        grid_spec=pltpu.PrefetchScalarGridSpec(
            num_scalar_prefetch=0,
            grid=(t // tb,),
            in_specs=[pl.BlockSpec((1, 1, 2 * tb), lambda i: (i, 0, 0), memory_space=pltpu.SMEM),
                      pl.BlockSpec(memory_space=pl.ANY),
                      pl.BlockSpec((tb, d), lambda i: (i, 0)),
                      pl.BlockSpec((tb, LANES), lambda i: (i, 0)),
                      pl.BlockSpec((1, 6, d), lambda i: (i // tpb, 0, 0))],
            out_specs=pl.BlockSpec((tb, d), lambda i: (i, 0)),
            scratch_shapes=[pltpu.VMEM((2, tb, ys.shape[1]), ys.dtype), pltpu.SemaphoreType.DMA(())]),
        out_shape=jax.ShapeDtypeStruct((t, d), F32),
        compiler_params=_cparams(("arbitrary",)),
        name="combine",
    )(pos3, ys, x1, info, mod)


def _moe(h2, x1, info, info_t, counts, mod, w1, w2, layer, seq):
    t, d = h2.shape
    tb = min(256, seq)
    n_tiles = (2 * t + N_EXPERTS * (EXPERT_TILE - 1)) // EXPERT_TILE
    cnt = counts[0, N_GROUPS:N_GROUPS + N_EXPERTS].astype(jnp.int32)
    padded = ((cnt + EXPERT_TILE - 1) // EXPERT_TILE) * EXPERT_TILE
    ends = jnp.cumsum(padded)
    starts = ends - padded
    n_valid = (ends[-1] // EXPERT_TILE).astype(jnp.int32).reshape(1)
    tile_start = jnp.arange(n_tiles, dtype=jnp.int32) * EXPERT_TILE
    tile_expert = jnp.sum(ends[None, :] <= tile_start[:, None], axis=1).astype(jnp.int32)
    tile_expert = jnp.minimum(tile_expert, tile_expert[jnp.maximum(n_valid[0] - 1, 0)])
    tile_expert = tile_expert + layer * N_EXPERTS
    eid = info_t[0:2].astype(jnp.int32)
    experts = jnp.arange(N_EXPERTS, dtype=jnp.int32)[:, None, None]
    pos = jnp.sum(jnp.where(eid[None] == experts, starts[:, None, None], 0), axis=0) + info_t[2:4].astype(jnp.int32)
    pos3 = pos.reshape(2, t // tb, tb).transpose(1, 0, 2).reshape(t // tb, 1, 2 * tb)
    xs = _dispatch(h2, pos3, n_tiles * EXPERT_TILE, tb)
    ys = _experts(xs, tile_expert, n_valid, w1, w2, n_tiles)
    return _combine(ys, pos3, x1, info, mod, seq, tb)


def _qk_norm_rows(qn_g, kn_g):
    zeros = jnp.zeros((DIFF_DH,), F32)
    feat = jnp.asarray(np.arange(DIFF_DH) < N_FEAT, F32)
    q_gain = jnp.tile(qn_g, 2 * DIFF_HEADS) * (DIFF_DH ** -0.5 * math.log2(math.e))
    k_gain = jnp.tile(jnp.concatenate([kn_g, zeros]), DIFF_HEADS)
    k_add = jnp.tile(jnp.concatenate([zeros, feat]), DIFF_HEADS)
    return jnp.stack([jnp.stack([q_gain, jnp.zeros_like(q_gain)]),
                      jnp.stack([k_gain, k_add]), jnp.stack([k_gain, k_add])])


def _prep_w_in(w_in_l):
    w = w_in_l
    c = np.cumsum([0, 256, 256, 512, 512, 16, 16, 512, 512, 512, 1024, 1024])
    seg = lambda i: w[:, c[i]:c[i + 1]]
    gq, gk, gv, gg, lrf, lrb, dq, dk, dv, ga, gb = (seg(i) for i in range(11))
    dk_aug = jnp.pad(dk.reshape(-1, 2 * DIFF_HEADS, 1, DIFF_DH), ((0, 0), (0, 0), (0, 1), (0, 0)))
    dk_aug = dk_aug.reshape(-1, 4 * DIFF_HEADS * DIFF_DH)
    main = jnp.concatenate([gq, gk, gv, ga, gb, dk_aug, gg, dq, dv], axis=1).astype(BF16)
    lr = jnp.concatenate([lrf, lrb, jnp.zeros((w.shape[0], LANES - 2 * GLA_RANK), w.dtype)], axis=1)
    return main, lr.astype(BF16)


def kernel(x, c, w_ada, b_ada, norm1_g, norm2_g, w_in, gla_w_up, gla_b_up, gla_norm_g, diff_qnorm_g,
           diff_knorm_g, diff_lambda, diff_subnorm_g, rel_bias, w_branch_a, w_branch_b, w_out,
           w_router_group, b_router_group, w_router_expert, b_router_expert, w_expert_in, w_expert_out):
    bsz, seq, d = x.shape
    t = bsz * seq
    depth = w_ada.shape[0]
    mod_all = _ada(c, w_ada, b_ada).reshape(depth, bsz, 6, d)

    tq = _attn_tile(seq)
    tiles = _bias_tables(rel_bias, tq, tq)
    grp = np.arange(PROJ_TILE) // DIFF_DH
    gmat = jnp.asarray((grp[:, None] == grp[None, :]).astype(np.float32) / DIFF_DH, dtype=BF16)

    xf = x.reshape(t, d)
    for l in range(depth):
        lam_init = 0.8 - 0.6 * math.exp(-0.3 * l)
        mod = mod_all[l]
        w_main, w_lr = _prep_w_in(w_in[l])
        qkg = _qk_norm_rows(diff_qnorm_g[l], diff_knorm_g[l])
        proj, lr, vt = _inproj(xf, mod, norm1_g[l].reshape(1, d), w_main, w_lr, gmat, qkg, seq)
        o_f, o_b = _gla(proj, lr, gla_w_up[l], gla_b_up[l].reshape(2, 1, -1), bsz, seq)
        fast, feat = _softmax_features(rel_bias, diff_qnorm_g[l], diff_knorm_g[l])
        o_b2 = _attn(proj, vt, tiles, fast, feat, diff_lambda[l], diff_subnorm_g[l].reshape(1, -1),
                     bsz, seq, lam_init)
        w_r = jnp.concatenate([w_router_group[l], w_router_expert[l],
                               jnp.zeros((d, LANES - N_GROUPS - N_EXPERTS), F32)], axis=1)
        w_r_hi = w_r.astype(BF16)
        w_r = jnp.stack([w_r_hi, (w_r - w_r_hi.astype(F32)).astype(BF16)])
        b_r = jnp.concatenate([b_router_group[l], b_router_expert[l],
                               jnp.zeros((LANES - N_GROUPS - N_EXPERTS,), F32)]).reshape(1, LANES)
        x1, h2, info, info_t, counts = _mix(xf, o_f, o_b, proj, o_b2, mod, gla_norm_g[l].reshape(1, -1),
                                    norm2_g[l].reshape(1, d), w_branch_a[l].astype(BF16),
                                    w_branch_b[l].astype(BF16), w_out[l].astype(BF16), w_r, b_r, seq)
        xf = _moe(h2, x1, info, info_t, counts, mod, w_expert_in.reshape(-1, d, 2 * D_EXPERT),
                  w_expert_out.reshape(-1, D_EXPERT, d), l, seq)
    return xf.reshape(bsz, seq, d)
```

```python
import functools
import math

import numpy as np
import jax
import jax.numpy as jnp
from jax import lax
from jax.experimental import pallas as pl
from jax.experimental.pallas import tpu as pltpu

F32 = jnp.float32
BF16 = jnp.bfloat16
HIGHEST = lax.Precision.HIGHEST

D_MODEL = 1024
DEPTH = 2
GLA_HEADS, GLA_DK, GLA_DV, GLA_RANK, GLA_TAU, GLA_CHUNK = 4, 64, 128, 16, 16.0, 64
DIFF_HEADS, DIFF_DH, DIFF_DV = 4, 64, 128
REL_BUCKETS, REL_MAX_DIST = 32, 128
N_GROUPS, EXPERTS_PER_GROUP = 4, 8
N_EXPERTS = N_GROUPS * EXPERTS_PER_GROUP
D_EXPERT = D_MODEL // 2
RMS_EPS = 1e-6
LANES = 128
VMEM_LIMIT = 52 * 1024 * 1024
EXPERT_TILE = 512
DMA_ISSUE_UNROLL = 8

COL_GQK, COL_GV, COL_GA, COL_GB, COL_DKA, COL_GG, COL_DQ, COL_DV = 0, 4, 8, 16, 24, 32, 36, 40
N_PROJ = 44 * LANES
PROJ_TILE = 512
VT_BLOCK = 256
N_FEAT = 3
SOFTMAX_SAFE_RANGE = 60.0
VALUE_MATMUL_TASKS = 8


def _cparams(sem, vmem=VMEM_LIMIT):
    return pltpu.CompilerParams(dimension_semantics=sem, vmem_limit_bytes=vmem)


def _pack_rows(x):
    w = x.shape[1] // 2
    bits = lax.bitcast_convert_type(x.astype(BF16).astype(F32), jnp.uint32)
    return bits[:, :w] | (bits[:, w:] >> 16)


def _unpack_rows(p):
    hi = lax.bitcast_convert_type(p & jnp.uint32(0xFFFF0000), F32)
    lo = lax.bitcast_convert_type(p << 16, F32)
    return hi, lo


def _ada_kernel(c_ref, w_ref, b_ref, o_ref):
    c = c_ref[...]
    ca = c * jax.nn.sigmoid(c)
    o_ref[0] = jnp.dot(ca, w_ref[0], precision=HIGHEST, preferred_element_type=F32) + b_ref[0]


def _ada(c, w_ada, b_ada):
    depth, d, n = w_ada.shape
    bsz = c.shape[0]
    tn = 1536
    return pl.pallas_call(
        _ada_kernel,
        grid=(depth, n // tn),
        in_specs=[pl.BlockSpec((bsz, d), lambda l, j: (0, 0)),
                  pl.BlockSpec((1, d, tn), lambda l, j: (l, 0, j)),
                  pl.BlockSpec((1, 1, tn), lambda l, j: (l, 0, j))],
        out_specs=pl.BlockSpec((1, bsz, tn), lambda l, j: (l, 0, j)),
        out_shape=jax.ShapeDtypeStruct((depth, bsz, n), F32),
        compiler_params=_cparams(("parallel", "parallel")),
        name="ada",
    )(c, w_ada, b_ada.reshape(depth, 1, n))


def _inproj_kernel(x_ref, mod_ref, g_ref, w_ref, wlr_ref, gmat_ref, qkg_ref, proj_ref, lr_ref, vt_ref):
    x = x_ref[...]
    ms = jnp.mean(x * x, axis=-1, keepdims=True)
    h = x * lax.rsqrt(ms + RMS_EPS) * g_ref[...]
    h = h * (1.0 + mod_ref[0, 1:2, :]) + mod_ref[0, 0:1, :]
    hb = h.astype(BF16)
    w = PROJ_TILE
    qk_tiles = {COL_DQ * LANES // w: 0, COL_DKA * LANES // w: 1, COL_DKA * LANES // w + 1: 2}
    for j in range(N_PROJ // w):
        acc = jnp.dot(hb, w_ref[:, j * w:(j + 1) * w], preferred_element_type=F32)
        if j in qk_tiles:
            ms2 = jnp.dot((acc * acc).astype(BF16), gmat_ref[...], preferred_element_type=F32)
            acc = acc * lax.rsqrt(ms2 + RMS_EPS) * qkg_ref[qk_tiles[j], 0:1, :] + qkg_ref[qk_tiles[j], 1:2, :]
        proj_ref[:, j * w:(j + 1) * w] = acc.astype(BF16)
        if j == COL_DV * LANES // w:
            vb = vt_ref.shape[-1]
            for u in range(vt_ref.shape[0]):
                vt_ref[u] = acc[u * vb:(u + 1) * vb].T.astype(BF16)
    lr_ref[...] = jnp.dot(hb, wlr_ref[...], preferred_element_type=F32)


def _inproj(x, mod, g, w, wlr, gmat, qkg, seq):
    t, d = x.shape
    tm = min(512, seq)
    vb = min(VT_BLOCK, tm)
    tpb = seq // tm
    return pl.pallas_call(
        _inproj_kernel,
        grid=(t // tm,),
        in_specs=[pl.BlockSpec((tm, d), lambda i: (i, 0)),
                  pl.BlockSpec((1, 6, d), lambda i: (i // tpb, 0, 0)),
                  pl.BlockSpec((1, d), lambda i: (0, 0)),
                  pl.BlockSpec((d, N_PROJ), lambda i: (0, 0), pipeline_mode=pl.Buffered(1)),
                  pl.BlockSpec((d, LANES), lambda i: (0, 0)),
                  pl.BlockSpec((PROJ_TILE, PROJ_TILE), lambda i: (0, 0)),
                  pl.BlockSpec((3, 2, PROJ_TILE), lambda i: (0, 0, 0))],
        out_specs=[pl.BlockSpec((tm, N_PROJ), lambda i: (i, 0)),
                   pl.BlockSpec((tm, LANES), lambda i: (i, 0)),
                   pl.BlockSpec((tm // vb, DIFF_HEADS * DIFF_DV, vb), lambda i: (i, 0, 0))],
        out_shape=[jax.ShapeDtypeStruct((t, N_PROJ), BF16),
                   jax.ShapeDtypeStruct((t, LANES), F32),
                   jax.ShapeDtypeStruct((t // vb, DIFF_HEADS * DIFF_DV, vb), BF16)],
        compiler_params=_cparams(("parallel",)),
        name="inproj",
    )(x, mod, g, w, wlr, gmat, qkg)


class _GlaDirection:
    def __init__(self, qk_ref, v_ref, lr, wup, bup, o_ref, s_ref, oi_ref, kv_ref, n_chunks, reverse):
        self.qk_ref, self.v_ref, self.lr, self.wup, self.bup = qk_ref, v_ref, lr, wup, bup
        self.o_ref, self.s_ref, self.oi_ref, self.kv_ref = o_ref, s_ref, oi_ref, kv_ref
        self.n_chunks, self.reverse = n_chunks, reverse
        c = GLA_CHUNK
        row = lax.broadcasted_iota(jnp.int32, (c, c), 0)
        col = lax.broadcasted_iota(jnp.int32, (c, c), 1)
        if reverse:
            self.smat, self.mask, self.tot_row = (col >= row).astype(BF16), col > row, 0
        else:
            self.smat, self.mask, self.tot_row = (col <= row).astype(BF16), col <= row, c - 1

    def _rows(self, n):
        return slice(n * GLA_CHUNK, (n + 1) * GLA_CHUNK)

    def _head(self, h, width):
        return slice(h * width, (h + 1) * width)

    def cumulative_decay(self):
        z = jnp.dot(self.lr, self.wup, precision=HIGHEST, preferred_element_type=F32) + self.bup
        lg = (jnp.minimum(z, 0.0) - jnp.log(1.0 + jnp.exp(-jnp.abs(z)))) * (1.0 / GLA_TAU)
        lg_hi = lg.astype(BF16)
        lg_lo = (lg - lg_hi.astype(F32)).astype(BF16)
        self.cums = [jnp.dot(self.smat, lg_hi[self._rows(n)], preferred_element_type=F32)
                     + jnp.dot(self.smat, lg_lo[self._rows(n)], preferred_element_type=F32)
                     for n in range(self.n_chunks)]

    def scale_operands(self):
        hd = GLA_HEADS * GLA_DK
        self.decs, self.q_dec, self.k_inv, self.k_end = [], [], [], []
        for n, cum in enumerate(self.cums):
            dec = jnp.exp(cum[self.tot_row:self.tot_row + 1, :])
            qk = self.qk_ref[self._rows(n), :].astype(F32)
            k_inv = qk[:, hd:] * jnp.exp(-cum)
            self.decs.append(dec)
            self.q_dec.append((qk[:, :hd] * jnp.exp(cum) * (GLA_DK ** -0.5)).astype(BF16))
            self.k_end.append((k_inv * dec).astype(BF16))
            self.k_inv.append(k_inv.astype(BF16))

    def scores_and_outer_products(self):
        self.scores = []
        for n in range(self.n_chunks):
            v = self.v_ref[self._rows(n), :]
            for h in range(GLA_HEADS):
                sl = self._head(h, GLA_DK)
                self.scores.append(lax.dot_general(self.q_dec[n][:, sl], self.k_inv[n][:, sl],
                                                   (((1,), (1,)), ((), ())), preferred_element_type=F32))
                self.kv_ref[n, h] = lax.dot_general(v[:, self._head(h, GLA_DV)], self.k_end[n][:, sl],
                                                    (((0,), (0,)), ((), ())), preferred_element_type=F32)

    def intra_chunk(self):
        for n in range(self.n_chunks):
            v = self.v_ref[self._rows(n), :]
            for h in range(GLA_HEADS):
                a = jnp.where(self.mask, self.scores[n * GLA_HEADS + h], 0.0).astype(BF16)
                self.oi_ref[self._rows(n), self._head(h, GLA_DV)] = jnp.dot(
                    a, v[:, self._head(h, GLA_DV)], preferred_element_type=F32)

    def inter_chunk(self):
        states = [self.s_ref[h] for h in range(GLA_HEADS)]
        order = range(self.n_chunks - 1, -1, -1) if self.reverse else range(self.n_chunks)
        for n in order:
            for h in range(GLA_HEADS):
                sl = self._head(h, GLA_DK)
                o = self.oi_ref[self._rows(n), self._head(h, GLA_DV)] + lax.dot_general(
                    self.q_dec[n][:, sl], states[h].astype(BF16), (((1,), (1,)), ((), ())),
                    preferred_element_type=F32)
                self.o_ref[self._rows(n), self._head(h, GLA_DV)] = o.astype(self.o_ref.dtype)
                states[h] = states[h] * self.decs[n][:, sl] + self.kv_ref[n, h]
        for h in range(GLA_HEADS):
            self.s_ref[h] = states[h]


def _gla_kernel(qkf_ref, vf_ref, lrf_ref, qkb_ref, vb_ref, lrb_ref, wup_ref, bup_ref,
                of_ref, ob_ref, sf_ref, sb_ref, oi_ref, kv_ref, *, n_chunks):
    @pl.when(pl.program_id(1) == 0)
    def _():
        sf_ref[...] = jnp.zeros_like(sf_ref)
        sb_ref[...] = jnp.zeros_like(sb_ref)

    fwd = _GlaDirection(qkf_ref, vf_ref, lrf_ref[:, 0:GLA_RANK], wup_ref[0], bup_ref[0], of_ref, sf_ref,
                        oi_ref.at[0], kv_ref.at[0], n_chunks, False)
    bwd = _GlaDirection(qkb_ref, vb_ref, lrb_ref[:, GLA_RANK:2 * GLA_RANK], wup_ref[1], bup_ref[1], ob_ref,
                        sb_ref, oi_ref.at[1], kv_ref.at[1], n_chunks, True)
    for phase in ("cumulative_decay", "scale_operands", "scores_and_outer_products", "intra_chunk",
                  "inter_chunk"):
        getattr(fwd, phase)()
        getattr(bwd, phase)()


def _gla(proj, lr, wup, bup, bsz, seq):
    t = proj.shape[0]
    lb = min(512, seq)
    nblk = seq // lb
    vw = GLA_HEADS * GLA_DV
    fwd = lambda b, i: b * nblk + i
    bwd = lambda b, i: b * nblk + nblk - 1 - i
    return pl.pallas_call(
        functools.partial(_gla_kernel, n_chunks=lb // GLA_CHUNK),
        grid=(bsz, nblk),
        in_specs=[pl.BlockSpec((lb, 512), lambda b, i: (fwd(b, i), COL_GQK // 4)),
                  pl.BlockSpec((lb, 512), lambda b, i: (fwd(b, i), COL_GV // 4)),
                  pl.BlockSpec((lb, LANES), lambda b, i: (fwd(b, i), 0)),
                  pl.BlockSpec((lb, 512), lambda b, i: (bwd(b, i), COL_GQK // 4)),
                  pl.BlockSpec((lb, 512), lambda b, i: (bwd(b, i), COL_GV // 4)),
                  pl.BlockSpec((lb, LANES), lambda b, i: (bwd(b, i), 0)),
                  pl.BlockSpec((2, GLA_RANK, GLA_HEADS * GLA_DK), lambda b, i: (0, 0, 0)),
                  pl.BlockSpec((2, 1, GLA_HEADS * GLA_DK), lambda b, i: (0, 0, 0))],
        out_specs=[pl.BlockSpec((lb, vw), lambda b, i: (fwd(b, i), 0)),
                   pl.BlockSpec((lb, vw), lambda b, i: (bwd(b, i), 0))],
        out_shape=[jax.ShapeDtypeStruct((t, vw), BF16), jax.ShapeDtypeStruct((t, vw), BF16)],
        scratch_shapes=[pltpu.VMEM((GLA_HEADS, GLA_DV, GLA_DK), F32),
                        pltpu.VMEM((GLA_HEADS, GLA_DV, GLA_DK), F32),
                        pltpu.VMEM((2, lb, vw), F32),
                        pltpu.VMEM((2, lb // GLA_CHUNK, GLA_HEADS, GLA_DV, GLA_DK), F32)],
        compiler_params=_cparams(("parallel", "arbitrary")),
        name="gla",
    )(proj, proj, lr, proj, proj, lr, wup, bup)


def _t5_bucket_np(rel):
    nb = REL_BUCKETS // 2
    max_exact = nb // 2
    ret = np.where(rel > 0, nb, 0)
    n = np.abs(rel)
    nf = np.maximum(n, 1).astype(np.float64)
    large = max_exact + (np.log(nf / max_exact) / math.log(REL_MAX_DIST / max_exact)
                         * (nb - max_exact)).astype(np.int64)
    large = np.minimum(large, nb - 1)
    return ret + np.where(n < max_exact, n, large)


def _attn_kernel(fast_ref, q_ref, k_ref, vt_ref, tile_ref, feat_ref, lam_ref, sg_ref, o_ref,
                 m_ref, l_ref, acc_ref, qa_ref, *, tq, tk, nk, lam_init):
    qi = pl.program_id(2)
    q = q_ref[...]
    n_sub = tk // vt_ref.shape[-1]
    ts = tk // n_sub
    tasks = [(u, c) for u in range(n_sub) for c in range(2)]
    far_tiles = (tile_ref.shape[1] - 1) // 2

    band, left, right = 0, 1, 2
    q_t = q.astype(F32).T
    frow = lax.broadcasted_iota(jnp.int32, (DIFF_DH, tq), 0)
    for var in (band, left, right):
        feat = jnp.zeros((DIFF_DH, tq), F32)
        for n in range(N_FEAT):
            feat = jnp.where(frow == n, feat_ref[pl.program_id(1), var * N_FEAT + n], feat)
        for c in range(2):
            qa_ref[var, c] = jnp.concatenate([q_t[c * DIFF_DH:(c + 1) * DIFF_DH], feat], axis=0).astype(BF16)

    def logits(kb, var, u, c):
        r0 = pl.multiple_of(kb * tk + u * ts, ts)
        k = k_ref[pl.ds(r0, ts), c * LANES:(c + 1) * LANES]
        return jnp.dot(k, qa_ref[var, c], preferred_element_type=F32)

    def bias_tile(kb):
        return tile_ref.at[0, jnp.clip(kb - qi, -far_tiles, far_tiles) + far_tiles]

    def pipelined(blocks, step):
        work = [(kb, var, bias, u, c) for kb, var, bias in blocks for u, c in tasks]
        s_next = logits(work[0][0], work[0][1], work[0][3], work[0][4])
        for t, (kb, var, bias, u, c) in enumerate(work):
            s = s_next
            if t + 1 < len(work):
                nxt = work[t + 1]
                s_next = logits(nxt[0], nxt[1], nxt[3], nxt[4])
            if bias is not None:
                s = s + bias[u * ts:(u + 1) * ts, :]
            step(s, c, vt_ref[kb * n_sub + u])

    pending = {0: [], 1: []}

    def flush_values(c):
        if pending[c]:
            parts, pending[c] = pending[c], []
            acc_ref[c] = acc_ref[c] + jnp.dot(jnp.concatenate([v for v, _ in parts], axis=1),
                                              jnp.concatenate([w for _, w in parts], axis=0),
                                              preferred_element_type=F32)

    def bounded_step(s, c, vt):
        p = jnp.exp2(s)
        l_ref[c] = l_ref[c] + jnp.sum(p, axis=0, keepdims=True)
        pending[c].append((vt, p.astype(BF16)))
        if len(pending[c]) == VALUE_MATMUL_TASKS:
            flush_values(c)

    def online_step(s, c, vt):
        m_old = m_ref[c]
        m_new = jnp.maximum(m_old, jnp.max(s, axis=0, keepdims=True))
        p = jnp.exp2(s - m_new)
        alpha = jnp.exp2(m_old - m_new)
        l_ref[c] = alpha * l_ref[c] + jnp.sum(p, axis=0, keepdims=True)
        acc_ref[c] = alpha * acc_ref[c] + jnp.dot(vt, p.astype(BF16), preferred_element_type=F32)
        m_ref[c] = m_new

    m_ref[...] = jnp.full_like(m_ref, -1e30)
    l_ref[...] = jnp.zeros_like(l_ref)
    acc_ref[...] = jnp.zeros_like(acc_ref)

    @pl.when(fast_ref[0] == 1)
    def _():
        offsets = list(range(nk)) if nk < 3 else [-1, 0, 1] + list(range(2, nk - 1))
        blocks = []
        for d in offsets:
            kb = qi + d
            kb = jnp.where(kb >= nk, kb - nk, jnp.where(kb < 0, kb + nk, kb))
            if 2 <= d <= nk - 2:
                blocks.append((kb, jnp.where(kb > qi, right, left), None))
            else:
                blocks.append((kb, band, bias_tile(kb)))
        pipelined(blocks, bounded_step)
        flush_values(0)
        flush_values(1)

    @pl.when(fast_ref[0] == 0)
    def _():
        group = 2 if nk % 2 == 0 else 1

        def body(i, carry):
            pipelined([(group * i + g, band, bias_tile(group * i + g)) for g in range(group)], online_step)
            return carry

        lax.fori_loop(0, nk // group, body, 0)

    lv = lam_ref[...]
    lam = (jnp.exp(jnp.sum(lv[0:1] * lv[1:2], axis=-1, keepdims=True))
           - jnp.exp(jnp.sum(lv[2:3] * lv[3:4], axis=-1, keepdims=True)) + lam_init)
    out_t = acc_ref[0] * (1.0 / l_ref[0]) - acc_ref[1] * (lam / l_ref[1])
    out = out_t.T
    ms = jnp.mean(out * out, axis=-1, keepdims=True)
    out = out * lax.rsqrt(ms + RMS_EPS) * sg_ref[...] * (1.0 - lam_init)
    o_ref[...] = out.astype(o_ref.dtype)


def _attn_tile(seq):
    return min(512, seq)


def _attn(proj, vt, tiles, fast, feat, lam_vecs, subn_g, bsz, seq, lam_init):
    t = proj.shape[0]
    tq = tk = _attn_tile(seq)
    nq = seq // tq
    nk = seq // tk
    vb = vt.shape[-1]
    return pl.pallas_call(
        functools.partial(_attn_kernel, tq=tq, tk=tk, nk=nk, lam_init=lam_init),
        grid=(bsz, DIFF_HEADS, nq),
        in_specs=[pl.BlockSpec(memory_space=pltpu.SMEM),
                  pl.BlockSpec((tq, LANES), lambda b, h, i: (b * nq + i, COL_DQ + h)),
                  pl.BlockSpec((seq, 2 * LANES), lambda b, h, i: (b, COL_DKA // 2 + h)),
                  pl.BlockSpec((seq // vb, DIFF_DV, vb), lambda b, h, i: (b, h, 0)),
                  pl.BlockSpec((1, tiles.shape[1], tk, tq), lambda b, h, i: (h, 0, 0, 0)),
                  pl.BlockSpec(memory_space=pltpu.SMEM),
                  pl.BlockSpec((4, DIFF_DH), lambda b, h, i: (0, 0)),
                  pl.BlockSpec((1, DIFF_DV), lambda b, h, i: (0, 0))],
        out_specs=pl.BlockSpec((tq, DIFF_DV), lambda b, h, i: (b * nq + i, h)),
        out_shape=jax.ShapeDtypeStruct((t, DIFF_HEADS * DIFF_DV), BF16),
        scratch_shapes=[pltpu.VMEM((2, 1, tq), F32), pltpu.VMEM((2, 1, tq), F32),
                        pltpu.VMEM((2, DIFF_DV, tq), F32), pltpu.VMEM((3, 2, LANES, tq), BF16)],
        compiler_params=_cparams(("parallel", "parallel", "arbitrary")),
        name="attn",
    )(fast, proj, proj, vt, tiles, feat, lam_vecs, subn_g)


def _bias_tables(rel_bias, tq, tk):
    period = 2 * REL_MAX_DIST
    assert tq % period == 0 and tk % period == 0
    log2e = math.log2(math.e)
    nb = REL_BUCKETS // 2
    m = np.arange(period)
    rel_rep = np.where(m <= period // 2, -m, period - m)
    onehot = np.eye(REL_BUCKETS, dtype=np.float32)[_t5_bucket_np(rel_rep)]
    y = jnp.dot(jnp.asarray(onehot), rel_bias.astype(F32), precision=HIGHEST).T * log2e
    stride = 2 * period - 1
    block = jnp.tile(y, (1, 2 * period))[:, :period * stride].reshape(-1, period, stride)[:, :, :period]
    band_tile = jnp.tile(block, (1, tk // period, tq // period))
    c_left = (rel_bias[nb - 1].astype(F32) * log2e)[:, None, None]
    c_right = (rel_bias[2 * nb - 1].astype(F32) * log2e)[:, None, None]
    tiles = []
    for d in (-2, -1, 0, 1, 2):
        rel = (d * tk + lax.broadcasted_iota(jnp.int32, (1, tk, tq), 1)
               - lax.broadcasted_iota(jnp.int32, (1, tk, tq), 2))
        tiles.append(jnp.where(jnp.abs(rel) < REL_MAX_DIST, band_tile, jnp.where(rel < 0, c_left, c_right)))
    return jnp.stack(tiles, axis=1)


def _softmax_features(rel_bias, qn_g, kn_g):
    log2e = math.log2(math.e)
    bound = (1.02 * DIFF_DH ** 0.5 * log2e * jnp.max(jnp.abs(qn_g)) * jnp.max(jnp.abs(kn_g))
             + log2e * jnp.max(jnp.abs(rel_bias)))
    fast = bound <= SOFTMAX_SAFE_RANGE
    nb = REL_BUCKETS // 2
    far = jnp.stack([rel_bias[nb - 1], rel_bias[2 * nb - 1]], axis=-1).astype(F32) * log2e
    c_hi = far.astype(BF16).astype(F32)
    c_lo = (far - c_hi).astype(BF16).astype(F32)
    zeros = jnp.zeros_like(c_hi[:, 0])
    nbv = jnp.broadcast_to(-bound.astype(BF16).astype(F32), zeros.shape)
    rows = jnp.stack([jnp.stack([nbv, zeros, zeros], -1),
                      jnp.stack([nbv, c_hi[:, 0], c_lo[:, 0]], -1),
                      jnp.stack([nbv, c_hi[:, 1], c_lo[:, 1]], -1)], axis=1)
    feat = jnp.where(fast, rows, 0.0).reshape(DIFF_HEADS, 3 * N_FEAT)
    return fast.astype(jnp.int32).reshape(1), feat


def _mix_kernel(x_ref, of_ref, ob_ref, gg_ref, ob2_ref, ga_ref, gb_ref, mod_ref, gn_ref, n2_ref,
                wa_ref, wb_ref, wo_ref, wr_ref, br_ref,
                x1_ref, h2_ref, info_ref, info_t_ref, cnt_ref, base_ref, *, tm):
    @pl.when(pl.program_id(0) == 0)
    def _():
        base_ref[...] = jnp.zeros_like(base_ref)

    osum = of_ref[...].astype(F32) + ob_ref[...].astype(F32)
    parts = []
    for h in range(GLA_HEADS):
        sl = osum[:, h * GLA_DV:(h + 1) * GLA_DV]
        ms = jnp.mean(sl * sl, axis=-1, keepdims=True)
        parts.append(sl * lax.rsqrt(ms + RMS_EPS) * gn_ref[...])
    gg = gg_ref[...].astype(F32)
    o_a = jnp.concatenate(parts, axis=-1) * (gg * jax.nn.sigmoid(gg))
    y_a = jnp.dot(o_a.astype(BF16), wa_ref[...], preferred_element_type=F32)
    y_b = jnp.dot(ob2_ref[...], wb_ref[...], preferred_element_type=F32)
    merged = (jax.nn.sigmoid(ga_ref[...].astype(F32)) * y_a
              + jax.nn.sigmoid(gb_ref[...].astype(F32)) * y_b)
    y = jnp.dot(merged.astype(BF16), wo_ref[...], preferred_element_type=F32)
    x1 = x_ref[...] + mod_ref[0, 2:3, :] * y
    x1_ref[...] = x1

    ms = jnp.mean(x1 * x1, axis=-1, keepdims=True)
    h2 = x1 * lax.rsqrt(ms + RMS_EPS) * n2_ref[...]
    h2 = h2 * (1.0 + mod_ref[0, 4:5, :]) + mod_ref[0, 3:4, :]
    h2_ref[...] = _pack_rows(h2)

    h2_hi = h2.astype(BF16)
    h2_lo = (h2 - h2_hi.astype(F32)).astype(BF16)
    logits = (jnp.dot(h2_hi, wr_ref[0], preferred_element_type=F32)
              + jnp.dot(h2_lo, wr_ref[0], preferred_element_type=F32)
              + jnp.dot(h2_hi, wr_ref[1], preferred_element_type=F32)) + br_ref[...]
    lane_i = lax.broadcasted_iota(jnp.int32, logits.shape, 1)
    lane = lane_i.astype(F32)
    neg = jnp.float32(-3e38)
    big = jnp.float32(1 << 20)
    is_g = lane < N_GROUPS
    gl = jnp.where(is_g, logits, neg)
    gmax = jnp.max(gl, axis=-1, keepdims=True)
    grp = jnp.min(jnp.where(jnp.logical_and(is_g, gl == gmax), lane, big), axis=-1, keepdims=True)
    p_grp = 1.0 / jnp.sum(jnp.where(is_g, jnp.exp(gl - gmax), 0.0), axis=-1, keepdims=True)
    lo = N_GROUPS + grp * EXPERTS_PER_GROUP
    in_grp = jnp.logical_and(lane >= lo, lane < lo + EXPERTS_PER_GROUP)
    el = jnp.where(in_grp, logits, neg)
    e1 = jnp.max(el, axis=-1, keepdims=True)
    i1 = jnp.min(jnp.where(jnp.logical_and(in_grp, el == e1), lane, big), axis=-1, keepdims=True)
    rest = jnp.logical_and(in_grp, lane != i1)
    el2 = jnp.where(rest, logits, neg)
    e2 = jnp.max(el2, axis=-1, keepdims=True)
    i2 = jnp.min(jnp.where(jnp.logical_and(rest, el2 == e2), lane, big), axis=-1, keepdims=True)
    r = jnp.exp(e2 - e1)
    w1 = p_grp / (1.0 + r)
    w2 = w1 * r

    oh1 = (lane == i1).astype(F32)
    oh2 = (lane == i2).astype(F32)
    rr = lax.broadcasted_iota(jnp.int32, (tm, tm), 0)
    cc = lax.broadcasted_iota(jnp.int32, (tm, tm), 1)
    tril = (cc < rr).astype(BF16)
    before = jnp.dot(tril, (oh1 + oh2).astype(BF16), preferred_element_type=F32) + base_ref[...]
    rank1 = jnp.sum(oh1 * before, axis=-1, keepdims=True)
    rank2 = jnp.sum(oh2 * before, axis=-1, keepdims=True)
    base_new = base_ref[...] + jnp.sum(oh1 + oh2, axis=0, keepdims=True)
    base_ref[...] = base_new
    cnt_ref[...] = jnp.broadcast_to(base_new, cnt_ref.shape)

    ex1 = i1 - N_GROUPS
    ex2 = i2 - N_GROUPS
    info = jnp.zeros(logits.shape, F32)
    for idx, val in enumerate((ex1, ex2, rank1, rank2, w1, w2)):
        info = jnp.where(lane_i == idx, val, info)
    info_ref[...] = info
    info_t_ref[...] = info.T[0:info_t_ref.shape[0], :]


def _mix(x, of, ob, proj, ob2, mod, gn, n2, wa, wb, wo, wr, br, seq):
    t, d = x.shape
    tm = min(512, seq)
    tpb = seq // tm
    row = lambda i: (i, 0)
    const = lambda i: (0, 0)
    return pl.pallas_call(
        functools.partial(_mix_kernel, tm=tm),
        grid=(t // tm,),
        in_specs=[pl.BlockSpec((tm, d), row),
                  pl.BlockSpec((tm, 512), row),
                  pl.BlockSpec((tm, 512), row),
                  pl.BlockSpec((tm, 512), lambda i: (i, COL_GG // 4)),
                  pl.BlockSpec((tm, 512), row),
                  pl.BlockSpec((tm, 1024), lambda i: (i, COL_GA // 8)),
                  pl.BlockSpec((tm, 1024), lambda i: (i, COL_GB // 8)),
                  pl.BlockSpec((1, 6, d), lambda i: (i // tpb, 0, 0)),
                  pl.BlockSpec((1, GLA_DV), const),
                  pl.BlockSpec((1, d), const),
                  pl.BlockSpec((512, d), const),
                  pl.BlockSpec((512, d), const),
                  pl.BlockSpec((d, d), const),
                  pl.BlockSpec((2, d, LANES), lambda i: (0, 0, 0)),
                  pl.BlockSpec((1, LANES), const)],
        out_specs=[pl.BlockSpec((tm, d), row),
                   pl.BlockSpec((tm, d // 2), row),
                   pl.BlockSpec((tm, LANES), row),
                   pl.BlockSpec((8, tm), lambda i: (0, i)),
                   pl.BlockSpec((8, LANES), const)],
        out_shape=[jax.ShapeDtypeStruct((t, d), F32),
                   jax.ShapeDtypeStruct((t, d // 2), jnp.uint32),
                   jax.ShapeDtypeStruct((t, LANES), F32),
                   jax.ShapeDtypeStruct((8, t), F32),
                   jax.ShapeDtypeStruct((8, LANES), F32)],
        scratch_shapes=[pltpu.VMEM((1, LANES), F32)],
        compiler_params=_cparams(("arbitrary",)),
        name="mix",
    )(x, of, ob, proj, ob2, proj, proj, mod, gn, n2, wa, wb, wo, wr, br)


def _dispatch_kernel(pos_ref, h_ref, zeros_ref, xs_ref, sem_ref, *, tb):
    del zeros_ref

    def row_copy(r, k):
        return pltpu.make_async_copy(h_ref.at[pl.ds(r, 1), :],
                                     xs_ref.at[pl.ds(pos_ref[0, 0, k * tb + r], 1), :], sem_ref)

    def start(r, carry):
        row_copy(r, 0).start(priority=0)
        row_copy(r, 1).start(priority=1)
        return carry

    def wait(r, carry):
        row_copy(r, 0).wait()
        row_copy(r, 1).wait()
        return carry

    lax.fori_loop(0, tb, start, 0, unroll=DMA_ISSUE_UNROLL)
    lax.fori_loop(0, tb, wait, 0, unroll=True)


def _dispatch(h2, pos3, n_rows, tb):
    t, d = h2.shape
    return pl.pallas_call(
        functools.partial(_dispatch_kernel, tb=tb),
        grid_spec=pltpu.PrefetchScalarGridSpec(
            num_scalar_prefetch=0,
            grid=(t // tb,),
            in_specs=[pl.BlockSpec((1, 1, 2 * tb), lambda i: (i, 0, 0), memory_space=pltpu.SMEM),
                      pl.BlockSpec((tb, d), lambda i: (i, 0)),
                      pl.BlockSpec(memory_space=pl.ANY)],
            out_specs=pl.BlockSpec(memory_space=pl.ANY),
            scratch_shapes=[pltpu.SemaphoreType.DMA(())]),
        out_shape=jax.ShapeDtypeStruct((n_rows, d), h2.dtype),
        input_output_aliases={2: 0},
        compiler_params=_cparams(("arbitrary",)),
        name="dispatch",
    )(pos3, h2, jnp.zeros((n_rows, d), h2.dtype))


def _expert_kernel(te_ref, nv_ref, xs_ref, w1_ref, w2_ref, ys_ref, w1b_ref, w2b_ref):
    i = pl.program_id(0)

    @pl.when(i >= nv_ref[0])
    def _():
        ys_ref[...] = jnp.zeros_like(ys_ref)

    new_expert = jnp.logical_or(i == 0, te_ref[i] != te_ref[jnp.maximum(i - 1, 0)])

    @pl.when(jnp.logical_and(i < nv_ref[0], new_expert))
    def _():
        w1b_ref[...] = w1_ref[0].astype(BF16)
        w2b_ref[...] = w2_ref[0].astype(BF16)

    @pl.when(i < nv_ref[0])
    def _():
        x_hi, x_lo = _unpack_rows(xs_ref[...])
        half = w1b_ref.shape[0] // 2
        hu = (jnp.dot(x_hi.astype(BF16), w1b_ref[:half, :], preferred_element_type=F32)
              + jnp.dot(x_lo.astype(BF16), w1b_ref[half:, :], preferred_element_type=F32))
        hg = hu[:, :D_EXPERT]
        act = (hg * jax.nn.sigmoid(hg)) * hu[:, D_EXPERT:]
        ys_ref[...] = _pack_rows(jnp.dot(act.astype(BF16), w2b_ref[...], preferred_element_type=F32))


def _experts(xs, tile_expert, n_valid, w1, w2, n_tiles):
    d = w1.shape[1]
    dp = xs.shape[1]
    blk = lambda i, te, nv: (jnp.maximum(jnp.minimum(i, nv[0] - 1), 0), 0)
    return pl.pallas_call(
        _expert_kernel,
        grid_spec=pltpu.PrefetchScalarGridSpec(
            num_scalar_prefetch=2,
            grid=(n_tiles,),
            in_specs=[pl.BlockSpec((EXPERT_TILE, dp), blk),
                      pl.BlockSpec((1, d, 2 * D_EXPERT), lambda i, te, nv: (te[i], 0, 0)),
                      pl.BlockSpec((1, D_EXPERT, d), lambda i, te, nv: (te[i], 0, 0))],
            out_specs=pl.BlockSpec((EXPERT_TILE, dp), lambda i, te, nv: (i, 0)),
            scratch_shapes=[pltpu.VMEM((d, 2 * D_EXPERT), BF16), pltpu.VMEM((D_EXPERT, d), BF16)]),
        out_shape=jax.ShapeDtypeStruct((n_tiles * EXPERT_TILE, dp), xs.dtype),
        compiler_params=_cparams(("arbitrary",)),
        name="experts",
    )(tile_expert, n_valid, xs, w1, w2)


def _combine_kernel(pos_ref, ys_ref, x1_ref, info_ref, mod_ref, o_ref, buf_ref, sem_ref, *, tb):
    def row_copy(r, k):
        return pltpu.make_async_copy(ys_ref.at[pl.ds(pos_ref[0, 0, k * tb + r], 1), :],
                                     buf_ref.at[k, pl.ds(r, 1), :], sem_ref)

    def start(r, carry):
        row_copy(r, 0).start(priority=0)
        row_copy(r, 1).start(priority=1)
        return carry

    def wait(r, carry):
        row_copy(r, 0).wait()
        row_copy(r, 1).wait()
        return carry

    lax.fori_loop(0, tb, start, 0, unroll=DMA_ISSUE_UNROLL)
    lax.fori_loop(0, tb, wait, 0, unroll=True)
    info = info_ref[...]
    hi0, lo0 = _unpack_rows(buf_ref[0])
    hi1, lo1 = _unpack_rows(buf_ref[1])
    w0, w1 = info[:, 4:5], info[:, 5:6]
    y = jnp.concatenate([hi0 * w0 + hi1 * w1, lo0 * w0 + lo1 * w1], axis=1)
    o_ref[...] = x1_ref[...] + mod_ref[0, 5:6, :] * y


def _combine(ys, pos3, x1, info, mod, seq, tb):
    t, d = x1.shape
    tpb = seq // tb
    return pl.pallas_call(
        functools.partial(_combine_kernel, tb=tb),
        grid_spec=pltpu.PrefetchScalarGridSpec(
            num_scalar_prefetch=0,
            grid=(t // tb,),
            in_specs=[pl.BlockSpec((1, 1, 2 * tb), lambda i: (i, 0, 0), memory_space=pltpu.SMEM),
                      pl.BlockSpec(memory_space=pl.ANY),
                      pl.BlockSpec((tb, d), lambda i: (i, 0)),
                      pl.BlockSpec((tb, LANES), lambda i: (i, 0)),
                      pl.BlockSpec((1, 6, d), lambda i: (i // tpb, 0, 0))],
            out_specs=pl.BlockSpec((tb, d), lambda i: (i, 0)),
            scratch_shapes=[pltpu.VMEM((2, tb, ys.shape[1]), ys.dtype), pltpu.SemaphoreType.DMA(())]),
        out_shape=jax.ShapeDtypeStruct((t, d), F32),
        compiler_params=_cparams(("arbitrary",)),
        name="combine",
    )(pos3, ys, x1, info, mod)


def _moe(h2, x1, info, info_t, counts, mod, w1, w2, layer, seq):
    t, d = h2.shape
    tb = min(256, seq)
    n_tiles = (2 * t + N_EXPERTS * (EXPERT_TILE - 1)) // EXPERT_TILE
    cnt = counts[0, N_GROUPS:N_GROUPS + N_EXPERTS].astype(jnp.int32)
    padded = ((cnt + EXPERT_TILE - 1) // EXPERT_TILE) * EXPERT_TILE
    ends = jnp.cumsum(padded)
    starts = ends - padded
    n_valid = (ends[-1] // EXPERT_TILE).astype(jnp.int32).reshape(1)
    tile_start = jnp.arange(n_tiles, dtype=jnp.int32) * EXPERT_TILE
    tile_expert = jnp.sum(ends[None, :] <= tile_start[:, None], axis=1).astype(jnp.int32)
    tile_expert = jnp.minimum(tile_expert, tile_expert[jnp.maximum(n_valid[0] - 1, 0)])
    tile_expert = tile_expert + layer * N_EXPERTS
    eid = info_t[0:2].astype(jnp.int32)
    experts = jnp.arange(N_EXPERTS, dtype=jnp.int32)[:, None, None]
    pos = jnp.sum(jnp.where(eid[None] == experts, starts[:, None, None], 0), axis=0) + info_t[2:4].astype(jnp.int32)
    pos3 = pos.reshape(2, t // tb, tb).transpose(1, 0, 2).reshape(t // tb, 1, 2 * tb)
    xs = _dispatch(h2, pos3, n_tiles * EXPERT_TILE, tb)
    ys = _experts(xs, tile_expert, n_valid, w1, w2, n_tiles)
    return _combine(ys, pos3, x1, info, mod, seq, tb)


def _qk_norm_rows(qn_g, kn_g):
    zeros = jnp.zeros((DIFF_DH,), F32)
    feat = jnp.asarray(np.arange(DIFF_DH) < N_FEAT, F32)
    q_gain = jnp.tile(qn_g, 2 * DIFF_HEADS) * (DIFF_DH ** -0.5 * math.log2(math.e))
    k_gain = jnp.tile(jnp.concatenate([kn_g, zeros]), DIFF_HEADS)
    k_add = jnp.tile(jnp.concatenate([zeros, feat]), DIFF_HEADS)
    return jnp.stack([jnp.stack([q_gain, jnp.zeros_like(q_gain)]),
                      jnp.stack([k_gain, k_add]), jnp.stack([k_gain, k_add])])


def _prep_w_in(w_in_l):
    w = w_in_l
    c = np.cumsum([0, 256, 256, 512, 512, 16, 16, 512, 512, 512, 1024, 1024])
    seg = lambda i: w[:, c[i]:c[i + 1]]
    gq, gk, gv, gg, lrf, lrb, dq, dk, dv, ga, gb = (seg(i) for i in range(11))
    dk_aug = jnp.pad(dk.reshape(-1, 2 * DIFF_HEADS, 1, DIFF_DH), ((0, 0), (0, 0), (0, 1), (0, 0)))
    dk_aug = dk_aug.reshape(-1, 4 * DIFF_HEADS * DIFF_DH)
    main = jnp.concatenate([gq, gk, gv, ga, gb, dk_aug, gg, dq, dv], axis=1).astype(BF16)
    lr = jnp.concatenate([lrf, lrb, jnp.zeros((w.shape[0], LANES - 2 * GLA_RANK), w.dtype)], axis=1)
    return main, lr.astype(BF16)


def kernel(x, c, w_ada, b_ada, norm1_g, norm2_g, w_in, gla_w_up, gla_b_up, gla_norm_g, diff_qnorm_g,
           diff_knorm_g, diff_lambda, diff_subnorm_g, rel_bias, w_branch_a, w_branch_b, w_out,
           w_router_group, b_router_group, w_router_expert, b_router_expert, w_expert_in, w_expert_out):
    bsz, seq, d = x.shape
    t = bsz * seq
    depth = w_ada.shape[0]
    mod_all = _ada(c, w_ada, b_ada).reshape(depth, bsz, 6, d)

    tq = _attn_tile(seq)
    tiles = _bias_tables(rel_bias, tq, tq)
    grp = np.arange(PROJ_TILE) // DIFF_DH
    gmat = jnp.asarray((grp[:, None] == grp[None, :]).astype(np.float32) / DIFF_DH, dtype=BF16)

    xf = x.reshape(t, d)
    for l in range(depth):
        lam_init = 0.8 - 0.6 * math.exp(-0.3 * l)
        mod = mod_all[l]
        w_main, w_lr = _prep_w_in(w_in[l])
        qkg = _qk_norm_rows(diff_qnorm_g[l], diff_knorm_g[l])
        proj, lr, vt = _inproj(xf, mod, norm1_g[l].reshape(1, d), w_main, w_lr, gmat, qkg, seq)
        o_f, o_b = _gla(proj, lr, gla_w_up[l], gla_b_up[l].reshape(2, 1, -1), bsz, seq)
        fast, feat = _softmax_features(rel_bias, diff_qnorm_g[l], diff_knorm_g[l])
        o_b2 = _attn(proj, vt, tiles, fast, feat, diff_lambda[l], diff_subnorm_g[l].reshape(1, -1),
                     bsz, seq, lam_init)
        w_r = jnp.concatenate([w_router_group[l], w_router_expert[l],
                               jnp.zeros((d, LANES - N_GROUPS - N_EXPERTS), F32)], axis=1)
        w_r_hi = w_r.astype(BF16)
        w_r = jnp.stack([w_r_hi, (w_r - w_r_hi.astype(F32)).astype(BF16)])
        b_r = jnp.concatenate([b_router_group[l], b_router_expert[l],
                               jnp.zeros((LANES - N_GROUPS - N_EXPERTS,), F32)]).reshape(1, LANES)
        x1, h2, info, info_t, counts = _mix(xf, o_f, o_b, proj, o_b2, mod, gla_norm_g[l].reshape(1, -1),
                                    norm2_g[l].reshape(1, d), w_branch_a[l].astype(BF16),
                                    w_branch_b[l].astype(BF16), w_out[l].astype(BF16), w_r, b_r, seq)
        xf = _moe(h2, x1, info, info_t, counts, mod, w_expert_in.reshape(-1, d, 2 * D_EXPERT),
                  w_expert_out.reshape(-1, D_EXPERT, d), l, seq)
    return xf.reshape(bsz, seq, d)
```

```python
import functools
import math

import numpy as np
import jax
import jax.numpy as jnp
from jax import lax
from jax.experimental import pallas as pl
from jax.experimental.pallas import tpu as pltpu

F32 = jnp.float32
BF16 = jnp.bfloat16
HIGHEST = lax.Precision.HIGHEST

D_MODEL = 1024
DEPTH = 2
GLA_HEADS, GLA_DK, GLA_DV, GLA_RANK, GLA_TAU, GLA_CHUNK = 4, 64, 128, 16, 16.0, 64
DIFF_HEADS, DIFF_DH, DIFF_DV = 4, 64, 128
REL_BUCKETS, REL_MAX_DIST = 32, 128
N_GROUPS, EXPERTS_PER_GROUP = 4, 8
N_EXPERTS = N_GROUPS * EXPERTS_PER_GROUP
D_EXPERT = D_MODEL // 2
RMS_EPS = 1e-6
LANES = 128
VMEM_LIMIT = 52 * 1024 * 1024
EXPERT_TILE = 512
DMA_ISSUE_UNROLL = 8

COL_GQK, COL_GV, COL_GA, COL_GB, COL_DKA, COL_GG, COL_DQ, COL_DV = 0, 4, 8, 16, 24, 32, 36, 40
N_PROJ = 44 * LANES
PROJ_TILE = 512
VT_BLOCK = 256
N_FEAT = 3
SOFTMAX_SAFE_RANGE = 60.0
VALUE_MATMUL_TASKS = 8


def _cparams(sem, vmem=VMEM_LIMIT):
    return pltpu.CompilerParams(dimension_semantics=sem, vmem_limit_bytes=vmem)


def _pack_rows(x):
    w = x.shape[1] // 2
    bits = lax.bitcast_convert_type(x.astype(BF16).astype(F32), jnp.uint32)
    return bits[:, :w] | (bits[:, w:] >> 16)


def _unpack_rows(p):
    hi = lax.bitcast_convert_type(p & jnp.uint32(0xFFFF0000), F32)
    lo = lax.bitcast_convert_type(p << 16, F32)
    return hi, lo


def _ada_kernel(c_ref, w_ref, b_ref, o_ref):
    c = c_ref[...]
    ca = c * jax.nn.sigmoid(c)
    o_ref[0] = jnp.dot(ca, w_ref[0], precision=HIGHEST, preferred_element_type=F32) + b_ref[0]


def _ada(c, w_ada, b_ada):
    depth, d, n = w_ada.shape
    bsz = c.shape[0]
    tn = 1536
    return pl.pallas_call(
        _ada_kernel,
        grid=(depth, n // tn),
        in_specs=[pl.BlockSpec((bsz, d), lambda l, j: (0, 0)),
                  pl.BlockSpec((1, d, tn), lambda l, j: (l, 0, j)),
                  pl.BlockSpec((1, 1, tn), lambda l, j: (l, 0, j))],
        out_specs=pl.BlockSpec((1, bsz, tn), lambda l, j: (l, 0, j)),
        out_shape=jax.ShapeDtypeStruct((depth, bsz, n), F32),
        compiler_params=_cparams(("parallel", "parallel")),
        name="ada",
    )(c, w_ada, b_ada.reshape(depth, 1, n))


def _inproj_kernel(x_ref, mod_ref, g_ref, w_ref, wlr_ref, gmat_ref, qkg_ref, proj_ref, lr_ref, vt_ref):
    x = x_ref[...]
    ms = jnp.mean(x * x, axis=-1, keepdims=True)
    h = x * lax.rsqrt(ms + RMS_EPS) * g_ref[...]
    h = h * (1.0 + mod_ref[0, 1:2, :]) + mod_ref[0, 0:1, :]
    hb = h.astype(BF16)
    w = PROJ_TILE
    qk_tiles = {COL_DQ * LANES // w: 0, COL_DKA * LANES // w: 1, COL_DKA * LANES // w + 1: 2}
    for j in range(N_PROJ // w):
        acc = jnp.dot(hb, w_ref[:, j * w:(j + 1) * w], preferred_element_type=F32)
        if j in qk_tiles:
            ms2 = jnp.dot((acc * acc).astype(BF16), gmat_ref[...], preferred_element_type=F32)
            acc = acc * lax.rsqrt(ms2 + RMS_EPS) * qkg_ref[qk_tiles[j], 0:1, :] + qkg_ref[qk_tiles[j], 1:2, :]
        proj_ref[:, j * w:(j + 1) * w] = acc.astype(BF16)
        if j == COL_DV * LANES // w:
            vb = vt_ref.shape[-1]
            for u in range(vt_ref.shape[0]):
                vt_ref[u] = acc[u * vb:(u + 1) * vb].T.astype(BF16)
    lr_ref[...] = jnp.dot(hb, wlr_ref[...], preferred_element_type=F32)


def _inproj(x, mod, g, w, wlr, gmat, qkg, seq):
    t, d = x.shape
    tm = min(512, seq)
    vb = min(VT_BLOCK, tm)
    tpb = seq // tm
    return pl.pallas_call(
        _inproj_kernel,
        grid=(t // tm,),
        in_specs=[pl.BlockSpec((tm, d), lambda i: (i, 0)),
                  pl.BlockSpec((1, 6, d), lambda i: (i // tpb, 0, 0)),
                  pl.BlockSpec((1, d), lambda i: (0, 0)),
                  pl.BlockSpec((d, N_PROJ), lambda i: (0, 0), pipeline_mode=pl.Buffered(1)),
                  pl.BlockSpec((d, LANES), lambda i: (0, 0)),
                  pl.BlockSpec((PROJ_TILE, PROJ_TILE), lambda i: (0, 0)),
                  pl.BlockSpec((3, 2, PROJ_TILE), lambda i: (0, 0, 0))],
        out_specs=[pl.BlockSpec((tm, N_PROJ), lambda i: (i, 0)),
                   pl.BlockSpec((tm, LANES), lambda i: (i, 0)),
                   pl.BlockSpec((tm // vb, DIFF_HEADS * DIFF_DV, vb), lambda i: (i, 0, 0))],
        out_shape=[jax.ShapeDtypeStruct((t, N_PROJ), BF16),
                   jax.ShapeDtypeStruct((t, LANES), F32),
                   jax.ShapeDtypeStruct((t // vb, DIFF_HEADS * DIFF_DV, vb), BF16)],
        compiler_params=_cparams(("parallel",)),
        name="inproj",
    )(x, mod, g, w, wlr, gmat, qkg)


class _GlaDirection:
    def __init__(self, qk_ref, v_ref, lr, wup, bup, o_ref, s_ref, oi_ref, kv_ref, n_chunks, reverse):
        self.qk_ref, self.v_ref, self.lr, self.wup, self.bup = qk_ref, v_ref, lr, wup, bup
        self.o_ref, self.s_ref, self.oi_ref, self.kv_ref = o_ref, s_ref, oi_ref, kv_ref
        self.n_chunks, self.reverse = n_chunks, reverse
        c = GLA_CHUNK
        row = lax.broadcasted_iota(jnp.int32, (c, c), 0)
        col = lax.broadcasted_iota(jnp.int32, (c, c), 1)
        if reverse:
            self.smat, self.mask, self.tot_row = (col >= row).astype(BF16), col > row, 0
        else:
            self.smat, self.mask, self.tot_row = (col <= row).astype(BF16), col <= row, c - 1

    def _rows(self, n):
        return slice(n * GLA_CHUNK, (n + 1) * GLA_CHUNK)

    def _head(self, h, width):
        return slice(h * width, (h + 1) * width)

    def cumulative_decay(self):
        z = jnp.dot(self.lr, self.wup, precision=HIGHEST, preferred_element_type=F32) + self.bup
        lg = (jnp.minimum(z, 0.0) - jnp.log(1.0 + jnp.exp(-jnp.abs(z)))) * (1.0 / GLA_TAU)
        lg_hi = lg.astype(BF16)
        lg_lo = (lg - lg_hi.astype(F32)).astype(BF16)
        self.cums = [jnp.dot(self.smat, lg_hi[self._rows(n)], preferred_element_type=F32)
                     + jnp.dot(self.smat, lg_lo[self._rows(n)], preferred_element_type=F32)
                     for n in range(self.n_chunks)]

    def scale_operands(self):
        hd = GLA_HEADS * GLA_DK
        self.decs, self.q_dec, self.k_inv, self.k_end = [], [], [], []
        for n, cum in enumerate(self.cums):
            dec = jnp.exp(cum[self.tot_row:self.tot_row + 1, :])
            qk = self.qk_ref[self._rows(n), :].astype(F32)
            k_inv = qk[:, hd:] * jnp.exp(-cum)
            self.decs.append(dec)
            self.q_dec.append((qk[:, :hd] * jnp.exp(cum) * (GLA_DK ** -0.5)).astype(BF16))
            self.k_end.append((k_inv * dec).astype(BF16))
            self.k_inv.append(k_inv.astype(BF16))

    def scores_and_outer_products(self):
        self.scores = []
        for n in range(self.n_chunks):
            v = self.v_ref[self._rows(n), :]
            for h in range(GLA_HEADS):
                sl = self._head(h, GLA_DK)
                self.scores.append(lax.dot_general(self.q_dec[n][:, sl], self.k_inv[n][:, sl],
                                                   (((1,), (1,)), ((), ())), preferred_element_type=F32))
                self.kv_ref[n, h] = lax.dot_general(v[:, self._head(h, GLA_DV)], self.k_end[n][:, sl],
                                                    (((0,), (0,)), ((), ())), preferred_element_type=F32)

    def intra_chunk(self):
        for n in range(self.n_chunks):
            v = self.v_ref[self._rows(n), :]
            for h in range(GLA_HEADS):
                a = jnp.where(self.mask, self.scores[n * GLA_HEADS + h], 0.0).astype(BF16)
                self.oi_ref[self._rows(n), self._head(h, GLA_DV)] = jnp.dot(
                    a, v[:, self._head(h, GLA_DV)], preferred_element_type=F32)

    def inter_chunk(self):
        states = [self.s_ref[h] for h in range(GLA_HEADS)]
        order = range(self.n_chunks - 1, -1, -1) if self.reverse else range(self.n_chunks)
        for n in order:
            for h in range(GLA_HEADS):
                sl = self._head(h, GLA_DK)
                o = self.oi_ref[self._rows(n), self._head(h, GLA_DV)] + lax.dot_general(
                    self.q_dec[n][:, sl], states[h].astype(BF16), (((1,), (1,)), ((), ())),
                    preferred_element_type=F32)
                self.o_ref[self._rows(n), self._head(h, GLA_DV)] = o.astype(self.o_ref.dtype)
                states[h] = states[h] * self.decs[n][:, sl] + self.kv_ref[n, h]
        for h in range(GLA_HEADS):
            self.s_ref[h] = states[h]


def _gla_kernel(qkf_ref, vf_ref, lrf_ref, qkb_ref, vb_ref, lrb_ref, wup_ref, bup_ref,
                of_ref, ob_ref, sf_ref, sb_ref, oi_ref, kv_ref, *, n_chunks):
    @pl.when(pl.program_id(1) == 0)
    def _():
        sf_ref[...] = jnp.zeros_like(sf_ref)
        sb_ref[...] = jnp.zeros_like(sb_ref)

    fwd = _GlaDirection(qkf_ref, vf_ref, lrf_ref[:, 0:GLA_RANK], wup_ref[0], bup_ref[0], of_ref, sf_ref,
                        oi_ref.at[0], kv_ref.at[0], n_chunks, False)
    bwd = _GlaDirection(qkb_ref, vb_ref, lrb_ref[:, GLA_RANK:2 * GLA_RANK], wup_ref[1], bup_ref[1], ob_ref,
                        sb_ref, oi_ref.at[1], kv_ref.at[1], n_chunks, True)
    for phase in ("cumulative_decay", "scale_operands", "scores_and_outer_products", "intra_chunk",
                  "inter_chunk"):
        getattr(fwd, phase)()
        getattr(bwd, phase)()


def _gla(proj, lr, wup, bup, bsz, seq):
    t = proj.shape[0]
    lb = min(1024, seq)
    nblk = seq // lb
    vw = GLA_HEADS * GLA_DV
    fwd = lambda b, i: b * nblk + i
    bwd = lambda b, i: b * nblk + nblk - 1 - i
    return pl.pallas_call(
        functools.partial(_gla_kernel, n_chunks=lb // GLA_CHUNK),
        grid=(bsz, nblk),
        in_specs=[pl.BlockSpec((lb, 512), lambda b, i: (fwd(b, i), COL_GQK // 4)),
                  pl.BlockSpec((lb, 512), lambda b, i: (fwd(b, i), COL_GV // 4)),
                  pl.BlockSpec((lb, LANES), lambda b, i: (fwd(b, i), 0)),
                  pl.BlockSpec((lb, 512), lambda b, i: (bwd(b, i), COL_GQK // 4)),
                  pl.BlockSpec((lb, 512), lambda b, i: (bwd(b, i), COL_GV // 4)),
                  pl.BlockSpec((lb, LANES), lambda b, i: (bwd(b, i), 0)),
                  pl.BlockSpec((2, GLA_RANK, GLA_HEADS * GLA_DK), lambda b, i: (0, 0, 0)),
                  pl.BlockSpec((2, 1, GLA_HEADS * GLA_DK), lambda b, i: (0, 0, 0))],
        out_specs=[pl.BlockSpec((lb, vw), lambda b, i: (fwd(b, i), 0)),
                   pl.BlockSpec((lb, vw), lambda b, i: (bwd(b, i), 0))],
        out_shape=[jax.ShapeDtypeStruct((t, vw), BF16), jax.ShapeDtypeStruct((t, vw), BF16)],
        scratch_shapes=[pltpu.VMEM((GLA_HEADS, GLA_DV, GLA_DK), F32),
                        pltpu.VMEM((GLA_HEADS, GLA_DV, GLA_DK), F32),
                        pltpu.VMEM((2, lb, vw), F32),
                        pltpu.VMEM((2, lb // GLA_CHUNK, GLA_HEADS, GLA_DV, GLA_DK), F32)],
        compiler_params=_cparams(("parallel", "arbitrary")),
        name="gla",
    )(proj, proj, lr, proj, proj, lr, wup, bup)


def _t5_bucket_np(rel):
    nb = REL_BUCKETS // 2
    max_exact = nb // 2
    ret = np.where(rel > 0, nb, 0)
    n = np.abs(rel)
    nf = np.maximum(n, 1).astype(np.float64)
    large = max_exact + (np.log(nf / max_exact) / math.log(REL_MAX_DIST / max_exact)
                         * (nb - max_exact)).astype(np.int64)
    large = np.minimum(large, nb - 1)
    return ret + np.where(n < max_exact, n, large)


def _attn_kernel(fast_ref, q_ref, k_ref, vt_ref, tile_ref, feat_ref, lam_ref, sg_ref, o_ref,
                 m_ref, l_ref, acc_ref, qa_ref, *, tq, tk, nk, lam_init):
    qi = pl.program_id(2)
    q = q_ref[...]
    n_sub = tk // vt_ref.shape[-1]
    ts = tk // n_sub
    tasks = [(u, c) for u in range(n_sub) for c in range(2)]
    far_tiles = (tile_ref.shape[1] - 1) // 2

    band, left, right = 0, 1, 2
    q_t = q.astype(F32).T
    frow = lax.broadcasted_iota(jnp.int32, (DIFF_DH, tq), 0)
    for var in (band, left, right):
        feat = jnp.zeros((DIFF_DH, tq), F32)
        for n in range(N_FEAT):
            feat = jnp.where(frow == n, feat_ref[pl.program_id(1), var * N_FEAT + n], feat)
        for c in range(2):
            qa_ref[var, c] = jnp.concatenate([q_t[c * DIFF_DH:(c + 1) * DIFF_DH], feat], axis=0).astype(BF16)

    def logits(kb, var, u, c):
        r0 = pl.multiple_of(kb * tk + u * ts, ts)
        k = k_ref[pl.ds(r0, ts), c * LANES:(c + 1) * LANES]
        return jnp.dot(k, qa_ref[var, c], preferred_element_type=F32)

    def bias_tile(kb):
        return tile_ref.at[0, jnp.clip(kb - qi, -far_tiles, far_tiles) + far_tiles]

    def pipelined(blocks, step):
        work = [(kb, var, bias, u, c) for kb, var, bias in blocks for u, c in tasks]
        s_next = logits(work[0][0], work[0][1], work[0][3], work[0][4])
        for t, (kb, var, bias, u, c) in enumerate(work):
            s = s_next
            if t + 1 < len(work):
                nxt = work[t + 1]
                s_next = logits(nxt[0], nxt[1], nxt[3], nxt[4])
            if bias is not None:
                s = s + bias[u * ts:(u + 1) * ts, :]
            step(s, c, vt_ref[kb * n_sub + u])

    pending = {0: [], 1: []}

    def flush_values(c):
        if pending[c]:
            parts, pending[c] = pending[c], []
            acc_ref[c] = acc_ref[c] + jnp.dot(jnp.concatenate([v for v, _ in parts], axis=1),
                                              jnp.concatenate([w for _, w in parts], axis=0),
                                              preferred_element_type=F32)

    def bounded_step(s, c, vt):
        p = jnp.exp2(s)
        l_ref[c] = l_ref[c] + jnp.sum(p, axis=0, keepdims=True)
        pending[c].append((vt, p.astype(BF16)))
        if len(pending[c]) == VALUE_MATMUL_TASKS:
            flush_values(c)

    def online_step(s, c, vt):
        m_old = m_ref[c]
        m_new = jnp.maximum(m_old, jnp.max(s, axis=0, keepdims=True))
        p = jnp.exp2(s - m_new)
        alpha = jnp.exp2(m_old - m_new)
        l_ref[c] = alpha * l_ref[c] + jnp.sum(p, axis=0, keepdims=True)
        acc_ref[c] = alpha * acc_ref[c] + jnp.dot(vt, p.astype(BF16), preferred_element_type=F32)
        m_ref[c] = m_new

    m_ref[...] = jnp.full_like(m_ref, -1e30)
    l_ref[...] = jnp.zeros_like(l_ref)
    acc_ref[...] = jnp.zeros_like(acc_ref)

    @pl.when(fast_ref[0] == 1)
    def _():
        offsets = list(range(nk)) if nk < 3 else [-1, 0, 1] + list(range(2, nk - 1))
        blocks = []
        for d in offsets:
            kb = qi + d
            kb = jnp.where(kb >= nk, kb - nk, jnp.where(kb < 0, kb + nk, kb))
            if 2 <= d <= nk - 2:
                blocks.append((kb, jnp.where(kb > qi, right, left), None))
            else:
                blocks.append((kb, band, bias_tile(kb)))
        pipelined(blocks, bounded_step)
        flush_values(0)
        flush_values(1)

    @pl.when(fast_ref[0] == 0)
    def _():
        group = 2 if nk % 2 == 0 else 1

        def body(i, carry):
            pipelined([(group * i + g, band, bias_tile(group * i + g)) for g in range(group)], online_step)
            return carry

        lax.fori_loop(0, nk // group, body, 0)

    lv = lam_ref[...]
    lam = (jnp.exp(jnp.sum(lv[0:1] * lv[1:2], axis=-1, keepdims=True))
           - jnp.exp(jnp.sum(lv[2:3] * lv[3:4], axis=-1, keepdims=True)) + lam_init)
    out_t = acc_ref[0] * (1.0 / l_ref[0]) - acc_ref[1] * (lam / l_ref[1])
    out = out_t.T
    ms = jnp.mean(out * out, axis=-1, keepdims=True)
    out = out * lax.rsqrt(ms + RMS_EPS) * sg_ref[...] * (1.0 - lam_init)
    o_ref[...] = out.astype(o_ref.dtype)


def _attn_tile(seq):
    return min(512, seq)


def _attn(proj, vt, tiles, fast, feat, lam_vecs, subn_g, bsz, seq, lam_init):
    t = proj.shape[0]
    tq = tk = _attn_tile(seq)
    nq = seq // tq
    nk = seq // tk
    vb = vt.shape[-1]
    return pl.pallas_call(
        functools.partial(_attn_kernel, tq=tq, tk=tk, nk=nk, lam_init=lam_init),
        grid=(bsz, DIFF_HEADS, nq),
        in_specs=[pl.BlockSpec(memory_space=pltpu.SMEM),
                  pl.BlockSpec((tq, LANES), lambda b, h, i: (b * nq + i, COL_DQ + h)),
                  pl.BlockSpec((seq, 2 * LANES), lambda b, h, i: (b, COL_DKA // 2 + h)),
                  pl.BlockSpec((seq // vb, DIFF_DV, vb), lambda b, h, i: (b, h, 0)),
                  pl.BlockSpec((1, tiles.shape[1], tk, tq), lambda b, h, i: (h, 0, 0, 0)),
                  pl.BlockSpec(memory_space=pltpu.SMEM),
                  pl.BlockSpec((4, DIFF_DH), lambda b, h, i: (0, 0)),
                  pl.BlockSpec((1, DIFF_DV), lambda b, h, i: (0, 0))],
        out_specs=pl.BlockSpec((tq, DIFF_DV), lambda b, h, i: (b * nq + i, h)),
        out_shape=jax.ShapeDtypeStruct((t, DIFF_HEADS * DIFF_DV), BF16),
        scratch_shapes=[pltpu.VMEM((2, 1, tq), F32), pltpu.VMEM((2, 1, tq), F32),
                        pltpu.VMEM((2, DIFF_DV, tq), F32), pltpu.VMEM((3, 2, LANES, tq), BF16)],
        compiler_params=_cparams(("parallel", "parallel", "arbitrary")),
        name="attn",
    )(fast, proj, proj, vt, tiles, feat, lam_vecs, subn_g)


def _bias_tables(rel_bias, tq, tk):
    period = 2 * REL_MAX_DIST
    assert tq % period == 0 and tk % period == 0
    log2e = math.log2(math.e)
    nb = REL_BUCKETS // 2
    m = np.arange(period)
    rel_rep = np.where(m <= period // 2, -m, period - m)
    onehot = np.eye(REL_BUCKETS, dtype=np.float32)[_t5_bucket_np(rel_rep)]
    y = jnp.dot(jnp.asarray(onehot), rel_bias.astype(F32), precision=HIGHEST).T * log2e
    stride = 2 * period - 1
    block = jnp.tile(y, (1, 2 * period))[:, :period * stride].reshape(-1, period, stride)[:, :, :period]
    band_tile = jnp.tile(block, (1, tk // period, tq // period))
    c_left = (rel_bias[nb - 1].astype(F32) * log2e)[:, None, None]
    c_right = (rel_bias[2 * nb - 1].astype(F32) * log2e)[:, None, None]
    tiles = []
    for d in (-2, -1, 0, 1, 2):
        rel = (d * tk + lax.broadcasted_iota(jnp.int32, (1, tk, tq), 1)
               - lax.broadcasted_iota(jnp.int32, (1, tk, tq), 2))
        tiles.append(jnp.where(jnp.abs(rel) < REL_MAX_DIST, band_tile, jnp.where(rel < 0, c_left, c_right)))
    return jnp.stack(tiles, axis=1)


def _softmax_features(rel_bias, qn_g, kn_g):
    log2e = math.log2(math.e)
    bound = (1.02 * DIFF_DH ** 0.5 * log2e * jnp.max(jnp.abs(qn_g)) * jnp.max(jnp.abs(kn_g))
             + log2e * jnp.max(jnp.abs(rel_bias)))
    fast = bound <= SOFTMAX_SAFE_RANGE
    nb = REL_BUCKETS // 2
    far = jnp.stack([rel_bias[nb - 1], rel_bias[2 * nb - 1]], axis=-1).astype(F32) * log2e
    c_hi = far.astype(BF16).astype(F32)
    c_lo = (far - c_hi).astype(BF16).astype(F32)
    zeros = jnp.zeros_like(c_hi[:, 0])
    nbv = jnp.broadcast_to(-bound.astype(BF16).astype(F32), zeros.shape)
    rows = jnp.stack([jnp.stack([nbv, zeros, zeros], -1),
                      jnp.stack([nbv, c_hi[:, 0], c_lo[:, 0]], -1),
                      jnp.stack([nbv, c_hi[:, 1], c_lo[:, 1]], -1)], axis=1)
    feat = jnp.where(fast, rows, 0.0).reshape(DIFF_HEADS, 3 * N_FEAT)
    return fast.astype(jnp.int32).reshape(1), feat


def _mix_kernel(x_ref, of_ref, ob_ref, gg_ref, ob2_ref, ga_ref, gb_ref, mod_ref, gn_ref, n2_ref,
                wa_ref, wb_ref, wo_ref, wr_ref, br_ref,
                x1_ref, h2_ref, info_ref, info_t_ref, cnt_ref, base_ref, *, tm):
    @pl.when(pl.program_id(0) == 0)
    def _():
        base_ref[...] = jnp.zeros_like(base_ref)

    osum = of_ref[...].astype(F32) + ob_ref[...].astype(F32)
    parts = []
    for h in range(GLA_HEADS):
        sl = osum[:, h * GLA_DV:(h + 1) * GLA_DV]
        ms = jnp.mean(sl * sl, axis=-1, keepdims=True)
        parts.append(sl * lax.rsqrt(ms + RMS_EPS) * gn_ref[...])
    gg = gg_ref[...].astype(F32)
    o_a = jnp.concatenate(parts, axis=-1) * (gg * jax.nn.sigmoid(gg))
    y_a = jnp.dot(o_a.astype(BF16), wa_ref[...], preferred_element_type=F32)
    y_b = jnp.dot(ob2_ref[...], wb_ref[...], preferred_element_type=F32)
    merged = (jax.nn.sigmoid(ga_ref[...].astype(F32)) * y_a
              + jax.nn.sigmoid(gb_ref[...].astype(F32)) * y_b)
    y = jnp.dot(merged.astype(BF16), wo_ref[...], preferred_element_type=F32)
    x1 = x_ref[...] + mod_ref[0, 2:3, :] * y
    x1_ref[...] = x1

    ms = jnp.mean(x1 * x1, axis=-1, keepdims=True)
    h2 = x1 * lax.rsqrt(ms + RMS_EPS) * n2_ref[...]
    h2 = h2 * (1.0 + mod_ref[0, 4:5, :]) + mod_ref[0, 3:4, :]
    h2_ref[...] = _pack_rows(h2)

    h2_hi = h2.astype(BF16)
    h2_lo = (h2 - h2_hi.astype(F32)).astype(BF16)
    logits = (jnp.dot(h2_hi, wr_ref[0], preferred_element_type=F32)
              + jnp.dot(h2_lo, wr_ref[0], preferred_element_type=F32)
              + jnp.dot(h2_hi, wr_ref[1], preferred_element_type=F32)) + br_ref[...]
    lane_i = lax.broadcasted_iota(jnp.int32, logits.shape, 1)
    lane = lane_i.astype(F32)
    neg = jnp.float32(-3e38)
    big = jnp.float32(1 << 20)
    is_g = lane < N_GROUPS
    gl = jnp.where(is_g, logits, neg)
    gmax = jnp.max(gl, axis=-1, keepdims=True)
    grp = jnp.min(jnp.where(jnp.logical_and(is_g, gl == gmax), lane, big), axis=-1, keepdims=True)
    p_grp = 1.0 / jnp.sum(jnp.where(is_g, jnp.exp(gl - gmax), 0.0), axis=-1, keepdims=True)
    lo = N_GROUPS + grp * EXPERTS_PER_GROUP
    in_grp = jnp.logical_and(lane >= lo, lane < lo + EXPERTS_PER_GROUP)
    el = jnp.where(in_grp, logits, neg)
    e1 = jnp.max(el, axis=-1, keepdims=True)
    i1 = jnp.min(jnp.where(jnp.logical_and(in_grp, el == e1), lane, big), axis=-1, keepdims=True)
    rest = jnp.logical_and(in_grp, lane != i1)
    el2 = jnp.where(rest, logits, neg)
    e2 = jnp.max(el2, axis=-1, keepdims=True)
    i2 = jnp.min(jnp.where(jnp.logical_and(rest, el2 == e2), lane, big), axis=-1, keepdims=True)
    r = jnp.exp(e2 - e1)
    w1 = p_grp / (1.0 + r)
    w2 = w1 * r

    oh1 = (lane == i1).astype(F32)
    oh2 = (lane == i2).astype(F32)
    rr = lax.broadcasted_iota(jnp.int32, (tm, tm), 0)
    cc = lax.broadcasted_iota(jnp.int32, (tm, tm), 1)
    tril = (cc < rr).astype(BF16)
    before = jnp.dot(tril, (oh1 + oh2).astype(BF16), preferred_element_type=F32) + base_ref[...]
    rank1 = jnp.sum(oh1 * before, axis=-1, keepdims=True)
    rank2 = jnp.sum(oh2 * before, axis=-1, keepdims=True)
    base_new = base_ref[...] + jnp.sum(oh1 + oh2, axis=0, keepdims=True)
    base_ref[...] = base_new
    cnt_ref[...] = jnp.broadcast_to(base_new, cnt_ref.shape)

    ex1 = i1 - N_GROUPS
    ex2 = i2 - N_GROUPS
    info = jnp.zeros(logits.shape, F32)
    for idx, val in enumerate((ex1, ex2, rank1, rank2, w1, w2)):
        info = jnp.where(lane_i == idx, val, info)
    info_ref[...] = info
    info_t_ref[...] = info.T[0:info_t_ref.shape[0], :]


def _mix(x, of, ob, proj, ob2, mod, gn, n2, wa, wb, wo, wr, br, seq):
    t, d = x.shape
    tm = min(512, seq)
    tpb = seq // tm
    row = lambda i: (i, 0)
    const = lambda i: (0, 0)
    return pl.pallas_call(
        functools.partial(_mix_kernel, tm=tm),
        grid=(t // tm,),
        in_specs=[pl.BlockSpec((tm, d), row),
                  pl.BlockSpec((tm, 512), row),
                  pl.BlockSpec((tm, 512), row),
                  pl.BlockSpec((tm, 512), lambda i: (i, COL_GG // 4)),
                  pl.BlockSpec((tm, 512), row),
                  pl.BlockSpec((tm, 1024), lambda i: (i, COL_GA // 8)),
                  pl.BlockSpec((tm, 1024), lambda i: (i, COL_GB // 8)),
                  pl.BlockSpec((1, 6, d), lambda i: (i // tpb, 0, 0)),
                  pl.BlockSpec((1, GLA_DV), const),
                  pl.BlockSpec((1, d), const),
                  pl.BlockSpec((512, d), const),
                  pl.BlockSpec((512, d), const),
                  pl.BlockSpec((d, d), const),
                  pl.BlockSpec((2, d, LANES), lambda i: (0, 0, 0)),
                  pl.BlockSpec((1, LANES), const)],
        out_specs=[pl.BlockSpec((tm, d), row),
                   pl.BlockSpec((tm, d // 2), row),
                   pl.BlockSpec((tm, LANES), row),
                   pl.BlockSpec((8, tm), lambda i: (0, i)),
                   pl.BlockSpec((8, LANES), const)],
        out_shape=[jax.ShapeDtypeStruct((t, d), F32),
                   jax.ShapeDtypeStruct((t, d // 2), jnp.uint32),
                   jax.ShapeDtypeStruct((t, LANES), F32),
                   jax.ShapeDtypeStruct((8, t), F32),
                   jax.ShapeDtypeStruct((8, LANES), F32)],
        scratch_shapes=[pltpu.VMEM((1, LANES), F32)],
        compiler_params=_cparams(("arbitrary",)),
        name="mix",
    )(x, of, ob, proj, ob2, proj, proj, mod, gn, n2, wa, wb, wo, wr, br)


def _dispatch_kernel(pos_ref, h_ref, zeros_ref, xs_ref, sem_ref, *, tb):
    del zeros_ref

    def row_copy(r, k):
        return pltpu.make_async_copy(h_ref.at[pl.ds(r, 1), :],
                                     xs_ref.at[pl.ds(pos_ref[0, 0, k * tb + r], 1), :], sem_ref)

    def start(r, carry):
        row_copy(r, 0).start(priority=0)
        row_copy(r, 1).start(priority=1)
        return carry

    def wait(r, carry):
        row_copy(r, 0).wait()
        row_copy(r, 1).wait()
        return carry

    lax.fori_loop(0, tb, start, 0, unroll=DMA_ISSUE_UNROLL)
    lax.fori_loop(0, tb, wait, 0, unroll=True)


def _dispatch(h2, pos3, n_rows, tb):
    t, d = h2.shape
    return pl.pallas_call(
        functools.partial(_dispatch_kernel, tb=tb),
        grid_spec=pltpu.PrefetchScalarGridSpec(
            num_scalar_prefetch=0,
            grid=(t // tb,),
            in_specs=[pl.BlockSpec((1, 1, 2 * tb), lambda i: (i, 0, 0), memory_space=pltpu.SMEM),
                      pl.BlockSpec((tb, d), lambda i: (i, 0)),
                      pl.BlockSpec(memory_space=pl.ANY)],
            out_specs=pl.BlockSpec(memory_space=pl.ANY),
            scratch_shapes=[pltpu.SemaphoreType.DMA(())]),
        out_shape=jax.ShapeDtypeStruct((n_rows, d), h2.dtype),
        input_output_aliases={2: 0},
        compiler_params=_cparams(("arbitrary",)),
        name="dispatch",
    )(pos3, h2, jnp.zeros((n_rows, d), h2.dtype))


def _expert_kernel(te_ref, nv_ref, xs_ref, w1_ref, w2_ref, ys_ref, w1b_ref, w2b_ref):
    i = pl.program_id(0)

    @pl.when(i >= nv_ref[0])
    def _():
        ys_ref[...] = jnp.zeros_like(ys_ref)

    new_expert = jnp.logical_or(i == 0, te_ref[i] != te_ref[jnp.maximum(i - 1, 0)])

    @pl.when(jnp.logical_and(i < nv_ref[0], new_expert))
    def _():
        w1b_ref[...] = w1_ref[0].astype(BF16)
        w2b_ref[...] = w2_ref[0].astype(BF16)

    @pl.when(i < nv_ref[0])
    def _():
        x_hi, x_lo = _unpack_rows(xs_ref[...])
        half = w1b_ref.shape[0] // 2
        hu = (jnp.dot(x_hi.astype(BF16), w1b_ref[:half, :], preferred_element_type=F32)
              + jnp.dot(x_lo.astype(BF16), w1b_ref[half:, :], preferred_element_type=F32))
        hg = hu[:, :D_EXPERT]
        act = (hg * jax.nn.sigmoid(hg)) * hu[:, D_EXPERT:]
        ys_ref[...] = _pack_rows(jnp.dot(act.astype(BF16), w2b_ref[...], preferred_element_type=F32))


def _experts(xs, tile_expert, n_valid, w1, w2, n_tiles):
    d = w1.shape[1]
    dp = xs.shape[1]
    blk = lambda i, te, nv: (jnp.maximum(jnp.minimum(i, nv[0] - 1), 0), 0)
    return pl.pallas_call(
        _expert_kernel,
        grid_spec=pltpu.PrefetchScalarGridSpec(
            num_scalar_prefetch=2,
            grid=(n_tiles,),
            in_specs=[pl.BlockSpec((EXPERT_TILE, dp), blk),
                      pl.BlockSpec((1, d, 2 * D_EXPERT), lambda i, te, nv: (te[i], 0, 0)),
                      pl.BlockSpec((1, D_EXPERT, d), lambda i, te, nv: (te[i], 0, 0))],
            out_specs=pl.BlockSpec((EXPERT_TILE, dp), lambda i, te, nv: (i, 0)),
            scratch_shapes=[pltpu.VMEM((d, 2 * D_EXPERT), BF16), pltpu.VMEM((D_EXPERT, d), BF16)]),
        out_shape=jax.ShapeDtypeStruct((n_tiles * EXPERT_TILE, dp), xs.dtype),
        compiler_params=_cparams(("arbitrary",)),
        name="experts",
    )(tile_expert, n_valid, xs, w1, w2)


def _combine_kernel(pos_ref, ys_ref, x1_ref, info_ref, mod_ref, o_ref, buf_ref, sem_ref, *, tb):
    def row_copy(r, k):
        return pltpu.make_async_copy(ys_ref.at[pl.ds(pos_ref[0, 0, k * tb + r], 1), :],
                                     buf_ref.at[k, pl.ds(r, 1), :], sem_ref)

    def start(r, carry):
        row_copy(r, 0).start(priority=0)
        row_copy(r, 1).start(priority=1)
        return carry

    def wait(r, carry):
        row_copy(r, 0).wait()
        row_copy(r, 1).wait()
        return carry

    lax.fori_loop(0, tb, start, 0, unroll=DMA_ISSUE_UNROLL)
    lax.fori_loop(0, tb, wait, 0, unroll=True)
    info = info_ref[...]
    hi0, lo0 = _unpack_rows(buf_ref[0])
    hi1, lo1 = _unpack_rows(buf_ref[1])
    w0, w1 = info[:, 4:5], info[:, 5:6]
    y = jnp.concatenate([hi0 * w0 + hi1 * w1, lo0 * w0 + lo1 * w1], axis=1)
    o_ref[...] = x1_ref[...] + mod_ref[0, 5:6, :] * y


def _combine(ys, pos3, x1, info, mod, seq, tb):
    t, d = x1.shape
    tpb = seq // tb
    return pl.pallas_call(
        functools.partial(_combine_kernel, tb=tb),
        grid_spec=pltpu.PrefetchScalarGridSpec(
            num_scalar_prefetch=0,
            grid=(t // tb,),
            in_specs=[pl.BlockSpec((1, 1, 2 * tb), lambda i: (i, 0, 0), memory_space=pltpu.SMEM),
                      pl.BlockSpec(memory_space=pl.ANY),
                      pl.BlockSpec((tb, d), lambda i: (i, 0)),
                      pl.BlockSpec((tb, LANES), lambda i: (i, 0)),
                      pl.BlockSpec((1, 6, d), lambda i: (i // tpb, 0, 0))],
            out_specs=pl.BlockSpec((tb, d), lambda i: (i, 0)),
            scratch_shapes=[pltpu.VMEM((2, tb, ys.shape[1]), ys.dtype), pltpu.SemaphoreType.DMA(())]),
        out_shape=jax.ShapeDtypeStruct((t, d), F32),
        compiler_params=_cparams(("arbitrary",)),
        name="combine",
    )(pos3, ys, x1, info, mod)


def _moe(h2, x1, info, info_t, counts, mod, w1, w2, layer, seq):
    t, d = h2.shape
    tb = min(512, seq)
    n_tiles = (2 * t + N_EXPERTS * (EXPERT_TILE - 1)) // EXPERT_TILE
    cnt = counts[0, N_GROUPS:N_GROUPS + N_EXPERTS].astype(jnp.int32)
    padded = ((cnt + EXPERT_TILE - 1) // EXPERT_TILE) * EXPERT_TILE
    ends = jnp.cumsum(padded)
    starts = ends - padded
    n_valid = (ends[-1] // EXPERT_TILE).astype(jnp.int32).reshape(1)
    tile_start = jnp.arange(n_tiles, dtype=jnp.int32) * EXPERT_TILE
    tile_expert = jnp.sum(ends[None, :] <= tile_start[:, None], axis=1).astype(jnp.int32)
    tile_expert = jnp.minimum(tile_expert, tile_expert[jnp.maximum(n_valid[0] - 1, 0)])
    tile_expert = tile_expert + layer * N_EXPERTS
    eid = info_t[0:2].astype(jnp.int32)
    experts = jnp.arange(N_EXPERTS, dtype=jnp.int32)[:, None, None]
    pos = jnp.sum(jnp.where(eid[None] == experts, starts[:, None, None], 0), axis=0) + info_t[2:4].astype(jnp.int32)
    pos3 = pos.reshape(2, t // tb, tb).transpose(1, 0, 2).reshape(t // tb, 1, 2 * tb)
    xs = _dispatch(h2, pos3, n_tiles * EXPERT_TILE, tb)
    ys = _experts(xs, tile_expert, n_valid, w1, w2, n_tiles)
    return _combine(ys, pos3, x1, info, mod, seq, tb)


def _qk_norm_rows(qn_g, kn_g):
    zeros = jnp.zeros((DIFF_DH,), F32)
    feat = jnp.asarray(np.arange(DIFF_DH) < N_FEAT, F32)
    q_gain = jnp.tile(qn_g, 2 * DIFF_HEADS) * (DIFF_DH ** -0.5 * math.log2(math.e))
    k_gain = jnp.tile(jnp.concatenate([kn_g, zeros]), DIFF_HEADS)
    k_add = jnp.tile(jnp.concatenate([zeros, feat]), DIFF_HEADS)
    return jnp.stack([jnp.stack([q_gain, jnp.zeros_like(q_gain)]),
                      jnp.stack([k_gain, k_add]), jnp.stack([k_gain, k_add])])


def _prep_w_in(w_in_l):
    w = w_in_l
    c = np.cumsum([0, 256, 256, 512, 512, 16, 16, 512, 512, 512, 1024, 1024])
    seg = lambda i: w[:, c[i]:c[i + 1]]
    gq, gk, gv, gg, lrf, lrb, dq, dk, dv, ga, gb = (seg(i) for i in range(11))
    dk_aug = jnp.pad(dk.reshape(-1, 2 * DIFF_HEADS, 1, DIFF_DH), ((0, 0), (0, 0), (0, 1), (0, 0)))
    dk_aug = dk_aug.reshape(-1, 4 * DIFF_HEADS * DIFF_DH)
    main = jnp.concatenate([gq, gk, gv, ga, gb, dk_aug, gg, dq, dv], axis=1).astype(BF16)
    lr = jnp.concatenate([lrf, lrb, jnp.zeros((w.shape[0], LANES - 2 * GLA_RANK), w.dtype)], axis=1)
    return main, lr.astype(BF16)


def kernel(x, c, w_ada, b_ada, norm1_g, norm2_g, w_in, gla_w_up, gla_b_up, gla_norm_g, diff_qnorm_g,
           diff_knorm_g, diff_lambda, diff_subnorm_g, rel_bias, w_branch_a, w_branch_b, w_out,
           w_router_group, b_router_group, w_router_expert, b_router_expert, w_expert_in, w_expert_out):
    bsz, seq, d = x.shape
    t = bsz * seq
    depth = w_ada.shape[0]
    mod_all = _ada(c, w_ada, b_ada).reshape(depth, bsz, 6, d)

    tq = _attn_tile(seq)
    tiles = _bias_tables(rel_bias, tq, tq)
    grp = np.arange(PROJ_TILE) // DIFF_DH
    gmat = jnp.asarray((grp[:, None] == grp[None, :]).astype(np.float32) / DIFF_DH, dtype=BF16)

    xf = x.reshape(t, d)
    for l in range(depth):
        lam_init = 0.8 - 0.6 * math.exp(-0.3 * l)
        mod = mod_all[l]
        w_main, w_lr = _prep_w_in(w_in[l])
        qkg = _qk_norm_rows(diff_qnorm_g[l], diff_knorm_g[l])
        proj, lr, vt = _inproj(xf, mod, norm1_g[l].reshape(1, d), w_main, w_lr, gmat, qkg, seq)
        o_f, o_b = _gla(proj, lr, gla_w_up[l], gla_b_up[l].reshape(2, 1, -1), bsz, seq)
        fast, feat = _softmax_features(rel_bias, diff_qnorm_g[l], diff_knorm_g[l])
        o_b2 = _attn(proj, vt, tiles, fast, feat, diff_lambda[l], diff_subnorm_g[l].reshape(1, -1),
                     bsz, seq, lam_init)
        w_r = jnp.concatenate([w_router_group[l], w_router_expert[l],
                               jnp.zeros((d, LANES - N_GROUPS - N_EXPERTS), F32)], axis=1)
        w_r_hi = w_r.astype(BF16)
        w_r = jnp.stack([w_r_hi, (w_r - w_r_hi.astype(F32)).astype(BF16)])
        b_r = jnp.concatenate([b_router_group[l], b_router_expert[l],
                               jnp.zeros((LANES - N_GROUPS - N_EXPERTS,), F32)]).reshape(1, LANES)
        x1, h2, info, info_t, counts = _mix(xf, o_f, o_b, proj, o_b2, mod, gla_norm_g[l].reshape(1, -1),
                                    norm2_g[l].reshape(1, d), w_branch_a[l].astype(BF16),
                                    w_branch_b[l].astype(BF16), w_out[l].astype(BF16), w_r, b_r, seq)
        xf = _moe(h2, x1, info, info_t, counts, mod, w_expert_in.reshape(-1, d, 2 * D_EXPERT),
                  w_expert_out.reshape(-1, D_EXPERT, d), l, seq)
    return xf.reshape(bsz, seq, d)
```
